```python
import math
import jax, jax.numpy as jnp
from jax import lax
import numpy as np

D_MODEL = 4096
BATCH = 4
SEQ = 2048
DEPTH = 2
DEC_BATCH = 128
DEC_SEQ = 4
PAST_LEN = 16384
PAGE_SIZE = 128

H_A = 6
DK_A = 128
DV_A = 256
H_B = 12
DK_B = 128
DV_B = 128
CONV_W = 4
GDN_CONV_DIM = 2 * H_B * DK_B + H_B * DV_B
H_C = 8
DK_C = 128
DV_C = 128
D_MIX = H_A * DV_A + H_B * DV_B + H_C * DV_C
CHUNK = 64
IN_SPLITS = (H_A * DK_A, H_A * DK_A, H_A * DV_A, H_A * DV_A, H_A, H_A,
             H_B * DK_B, H_B * DK_B, H_B * DV_B, H_B * DV_B, H_B, H_B,
             H_C * DK_C, H_C * DK_C, H_C * DV_C, H_C * DV_C)
N_IN = sum(IN_SPLITS)
PEER_HEADS = 8
N_KEYS = 128
N_EXPERTS = N_KEYS * N_KEYS
PEER_QDIM = 256
PEER_TOPK = 16
PEER_BLOCK = 64
EPS = 1e-6

kernel_name = 'hymba_style_mlstm_gdn_hgrn2_peer_step'


def rmsnorm(x, g):
    xf = x.astype(jnp.float32)
    y = xf * lax.rsqrt(jnp.mean(xf * xf, axis=-1, keepdims=True) + EPS)
    return (y * g.astype(jnp.float32)).astype(x.dtype)


def head_rmsnorm(x, g):
    y = x * lax.rsqrt(jnp.mean(x * x, axis=-1, keepdims=True) + EPS)
    return y.reshape(x.shape[:2] + (-1,)) * g.astype(jnp.float32)


def l2norm(x):
    return x * lax.rsqrt(jnp.sum(x * x, axis=-1, keepdims=True) + EPS)


def to_chunks(x, c):
    b, l = x.shape[:2]
    x = x.reshape((b, l // c, c) + x.shape[2:])
    return jnp.moveaxis(x, (1, 3), (0, 2))


def from_chunks(y):
    y = jnp.moveaxis(y, (0, 2), (1, 3))
    return y.reshape((y.shape[0], y.shape[1] * y.shape[2]) + y.shape[3:])


def mlstm_scan(q, k, v, ig, logf, C0, n0, m0):
    c = math.gcd(q.shape[1], CHUNK)
    causal = jnp.tril(jnp.ones((c, c), dtype=bool))

    def step(carry, inp):
        C, n, m = carry
        qc, kc, vc, ic, fc = inp
        b = jnp.cumsum(fc, axis=-1)
        log_d = jnp.where(causal, b[..., :, None] - b[..., None, :] + ic[..., None, :], -jnp.inf)
        m_inter = b + m[..., None]
        m_t = jnp.maximum(m_inter, jnp.max(log_d, axis=-1))
        s = jnp.einsum('bhtd,bhsd->bhts', qc, kc) * jnp.exp(log_d - m_t[..., None])
        inter = jnp.exp(m_inter - m_t)
        num = jnp.einsum('bhts,bhsv->bhtv', s, vc) + inter[..., None] * jnp.einsum('bhtd,bhdv->bhtv', qc, C)
        den = jnp.sum(s, axis=-1) + inter * jnp.einsum('bhtd,bhd->bht', qc, n)
        h = num / jnp.maximum(jnp.abs(den), jnp.exp(-m_t))[..., None]
        m_new = m_t[..., -1]
        w = jnp.exp(b[..., -1:] - b + ic - m_new[..., None])
        decay = inter[..., -1]
        C_new = decay[..., None, None] * C + jnp.einsum('bhsd,bhsv->bhdv', kc * w[..., None], vc)
        n_new = decay[..., None] * n + jnp.einsum('bhs,bhsd->bhd', w, kc)
        return (C_new, n_new, m_new), h

    xs = (to_chunks(q, c), to_chunks(k, c), to_chunks(v, c), to_chunks(ig, c), to_chunks(logf, c))
    (C1, n1, m1), hs = lax.scan(step, (C0, n0, m0), xs)
    return from_chunks(hs), C1, n1, m1


def gdn_scan(q, k, v, g, beta, S0):
    c = math.gcd(q.shape[1], CHUNK)
    causal = jnp.tril(jnp.ones((c, c), dtype=bool))
    strict = jnp.tril(jnp.ones((c, c), dtype=bool), -1)
    dv = v.shape[-1]

    def step(S, inp):
        qc, kc, vc, gc, bc = inp
        G = jnp.cumsum(gc, axis=-1)
        gam = jnp.exp(jnp.where(causal, G[..., :, None] - G[..., None, :], -jnp.inf))
        A = jnp.where(strict, bc[..., :, None] * jnp.einsum('bhtd,bhsd->bhts', kc, kc) * gam, 0.0)
        rhs = jnp.concatenate([bc[..., None] * vc, (bc * jnp.exp(G))[..., None] * kc], axis=-1)
        sol = lax.linalg.triangular_solve(A, rhs, left_side=True, lower=True, unit_diagonal=True)
        u = sol[..., :dv] - jnp.einsum('bhtd,bhdv->bhtv', sol[..., dv:], S)
        o = (jnp.exp(G)[..., None] * jnp.einsum('bhtd,bhdv->bhtv', qc, S)
             + jnp.einsum('bhts,bhsv->bhtv', jnp.einsum('bhtd,bhsd->bhts', qc, kc) * gam, u))
        S_new = (jnp.exp(G[..., -1])[..., None, None] * S
                 + jnp.einsum('bhsd,bhsv->bhdv', kc * jnp.exp(G[..., -1:] - G)[..., None], u))
        return S_new, o

    xs = (to_chunks(q, c), to_chunks(k, c), to_chunks(v, c), to_chunks(g, c), to_chunks(beta, c))
    S1, hs = lax.scan(step, S0, xs)
    return from_chunks(hs), S1


def hgrn_scan(q, logf, kk, i, S0):
    c = math.gcd(q.shape[1], CHUNK)
    causal = jnp.tril(jnp.ones((c, c), dtype=bool))[:, :, None]

    def step(S, inp):
        qc, fc, kc, ic = inp
        Bc = jnp.cumsum(fc, axis=2)
        dec = jnp.exp(jnp.where(causal, Bc[:, :, :, None, :] - Bc[:, :, None, :, :], -jnp.inf))
        att = jnp.einsum('bhtd,bhtsd,bhsd->bhts', qc, dec, kc)
        o = jnp.einsum('bhtd,bhdv->bhtv', qc * jnp.exp(Bc), S) + jnp.einsum('bhts,bhsv->bhtv', att, ic)
        S_new = (jnp.exp(Bc[:, :, -1])[..., None] * S
                 + jnp.einsum('bhsd,bhsv->bhdv', kc * jnp.exp(Bc[:, :, -1:] - Bc), ic))
        return S_new, o

    xs = (to_chunks(q, c), to_chunks(logf, c), to_chunks(kk, c), to_chunks(i, c))
    S1, hs = lax.scan(step, S0, xs)
    return from_chunks(hs), S1


def causal_conv(x, buf, w):
    L = x.shape[1]
    xp = jnp.concatenate([buf, x], axis=1)
    y = xp[:, 0:L] * w[0]
    for j in range(1, CONV_W):
        y = y + xp[:, j:j + L] * w[j]
    return y, xp[:, -(CONV_W - 1):]


def token_mixers(h, st, w_in, b_i, b_f, a_norm_g, conv_w, A_log, dt_bias, b_norm_g, lb, c_norm_g, w_out):
    f32 = jnp.float32
    Bn, L, _ = h.shape
    proj = (h @ w_in).astype(f32)
    offs = np.cumsum(IN_SPLITS)[:-1].tolist()
    (aq, ak, av, ao, ai, af, bq, bk, bv, bz, ba, bb, cq, cf, ci, cg) = jnp.split(proj, offs, axis=-1)
    C0, n0, m0, S0, buf0, H0 = st

    hA, C1, n1, m1 = mlstm_scan(
        aq.reshape(Bn, L, H_A, DK_A) * (DK_A ** -0.5), ak.reshape(Bn, L, H_A, DK_A),
        av.reshape(Bn, L, H_A, DV_A), ai + b_i.astype(f32), jax.nn.log_sigmoid(af + b_f.astype(f32)),
        C0.astype(f32), n0.astype(f32), m0.astype(f32))
    outA = head_rmsnorm(hA, a_norm_g) * jax.nn.sigmoid(ao)

    qkv, buf1 = causal_conv(jnp.concatenate([bq, bk, bv], axis=-1), buf0.astype(f32), conv_w.astype(f32))
    qkv = jax.nn.silu(qkv)
    q_b, k_b, v_b = jnp.split(qkv, [H_B * DK_B, 2 * H_B * DK_B], axis=-1)
    q_b = l2norm(q_b.reshape(Bn, L, H_B, DK_B)) * (DK_B ** -0.5)
    k_b = l2norm(k_b.reshape(Bn, L, H_B, DK_B))
    g_b = -jnp.exp(A_log.astype(f32)) * jax.nn.softplus(ba + dt_bias.astype(f32))
    hB, S1 = gdn_scan(q_b, k_b, v_b.reshape(Bn, L, H_B, DV_B), g_b, jax.nn.sigmoid(bb), S0.astype(f32))
    outB = head_rmsnorm(hB, b_norm_g) * jax.nn.silu(bz)

    logf = jnp.logaddexp(jnp.log1p(-lb) + jax.nn.log_sigmoid(cf), jnp.log(lb))
    k_c = (1.0 - lb) * jax.nn.sigmoid(-cf)
    hC, H1 = hgrn_scan(jax.nn.silu(cq).reshape(Bn, L, H_C, DK_C), logf.reshape(Bn, L, H_C, DK_C),
                       k_c.reshape(Bn, L, H_C, DK_C), ci.reshape(Bn, L, H_C, DV_C), H0.astype(f32))
    outC = head_rmsnorm(hC, c_norm_g) * jax.nn.silu(cg)

    mix = jnp.concatenate([outA, outB, outC], axis=-1).astype(h.dtype) @ w_out
    new_st = tuple(s.astype(o.dtype) for s, o in zip((C1, n1, m1, S1, buf1, H1), st))
    return mix, new_st


def peer(x, w_q, subkeys, u, v):
    Bn, L, D = x.shape
    t = x.reshape(-1, D)
    T = t.shape[0]
    qry = (t @ w_q).astype(jnp.float32).reshape(T, PEER_HEADS, 2, PEER_QDIM // 2)
    sc = jnp.einsum('thpd,hpnd->thpn', qry, subkeys.astype(jnp.float32))
    top_s, top_i = lax.top_k(sc, PEER_TOPK)
    cand_s = (top_s[:, :, 0, :, None] + top_s[:, :, 1, None, :]).reshape(T, PEER_HEADS, -1)
    cand_i = (top_i[:, :, 0, :, None] * N_KEYS + top_i[:, :, 1, None, :]).reshape(T, PEER_HEADS, -1)
    best_s, pos = lax.top_k(cand_s, PEER_TOPK)
    idx = jnp.take_along_axis(cand_i, pos, axis=-1)
    gate = jax.nn.softmax(best_s, axis=-1).astype(x.dtype)
    pad = (-T) % PEER_BLOCK
    nb = (T + pad) // PEER_BLOCK
    tp = jnp.pad(t, ((0, pad), (0, 0))).reshape(nb, PEER_BLOCK, D)
    ip = jnp.pad(idx, ((0, pad), (0, 0), (0, 0))).reshape(nb, PEER_BLOCK, PEER_HEADS, PEER_TOPK)
    gp = jnp.pad(gate, ((0, pad), (0, 0), (0, 0))).reshape(nb, PEER_BLOCK, PEER_HEADS, PEER_TOPK)

    def block(args):
        tb, ib, gb = args
        a = jnp.einsum('thkd,td->thk', u[ib], tb)
        coef = jax.nn.gelu(a) * gb
        return jnp.einsum('thk,thkd->td', coef, v[ib])

    y = lax.map(block, (tp, ip, gp))
    return y.reshape(-1, D)[:T].reshape(Bn, L, D)


def hgrn_lower_bounds(logits):
    p = jax.nn.softmax(logits.astype(jnp.float32), axis=0)
    cs = jnp.cumsum(p, axis=0)
    return cs - cs[0:1]


def trunk(x, states, weights, final_norm_g):
    (norm_mix_g, w_in, mlstm_b_i, mlstm_b_f, mlstm_norm_g, gdn_conv_w, gdn_A_log, gdn_dt_bias,
     gdn_norm_g, hgrn_lb_logits, hgrn_norm_g, w_out, norm_ffn_g, peer_w_q, peer_subkeys, peer_u, peer_v) = weights
    lbs = hgrn_lower_bounds(hgrn_lb_logits)
    new = [[] for _ in states]
    for l in range(DEPTH):
        st = tuple(s[l] for s in states)
        mix, st1 = token_mixers(rmsnorm(x, norm_mix_g[l]), st, w_in[l], mlstm_b_i[l], mlstm_b_f[l],
                                mlstm_norm_g[l], gdn_conv_w[l], gdn_A_log[l], gdn_dt_bias[l], gdn_norm_g[l],
                                lbs[l], hgrn_norm_g[l], w_out[l])
        x = x + mix
        x = x + peer(rmsnorm(x, norm_ffn_g[l]), peer_w_q[l], peer_subkeys[l], peer_u[l], peer_v[l])
        for lst, s in zip(new, st1):
            lst.append(s)
    return rmsnorm(x, final_norm_g), [jnp.stack(s, axis=0) for s in new]


def setup_inputs(seed: int = 0) -> dict:
    key = jax.random.key(seed)
    ks = jax.random.split(key, 32)
    nrm = lambda k, shape, s: jax.random.normal(k, shape, jnp.float32) * s
    dt = jnp.exp(jax.random.uniform(ks[14], (DEPTH, H_B), jnp.float32) * (math.log(0.1) - math.log(0.001)) + math.log(0.001))
    return {
        'x_prompt': nrm(ks[0], (BATCH, SEQ, D_MODEL), 1.0),
        'x_sample': nrm(ks[1], (DEC_BATCH, DEC_SEQ, D_MODEL), 1.0),
        'state_mlstm_C': nrm(ks[2], (DEPTH, DEC_BATCH, H_A, DK_A, DV_A), 0.5),
        'state_mlstm_n': nrm(ks[3], (DEPTH, DEC_BATCH, H_A, DK_A), 0.5),
        'state_mlstm_m': nrm(ks[4], (DEPTH, DEC_BATCH, H_A), 1.0),
        'state_gdn_S': nrm(ks[5], (DEPTH, DEC_BATCH, H_B, DK_B, DV_B), 0.1),
        'state_gdn_conv': nrm(ks[6], (DEPTH, DEC_BATCH, CONV_W - 1, GDN_CONV_DIM), 1.0),
        'state_hgrn_S': nrm(ks[7], (DEPTH, DEC_BATCH, H_C, DK_C, DV_C), 0.5),
        'norm_mix_g': 1.0 + nrm(ks[8], (DEPTH, D_MODEL), 0.02),
        'w_in': nrm(ks[9], (DEPTH, D_MODEL, N_IN), D_MODEL ** -0.5),
        'mlstm_b_i': nrm(ks[10], (DEPTH, H_A), 0.1),
        'mlstm_b_f': jnp.linspace(3.0, 6.0, H_A, dtype=jnp.float32)[None, :] + nrm(ks[11], (DEPTH, H_A), 0.1),
        'mlstm_norm_g': 1.0 + nrm(ks[12], (DEPTH, H_A * DV_A), 0.02),
        'gdn_conv_w': nrm(ks[13], (DEPTH, CONV_W, GDN_CONV_DIM), CONV_W ** -0.5),
        'gdn_A_log': jnp.log(jax.random.uniform(ks[15], (DEPTH, H_B), jnp.float32, 1.0, 16.0)),
        'gdn_dt_bias': dt + jnp.log(-jnp.expm1(-dt)),
        'gdn_norm_g': 1.0 + nrm(ks[16], (DEPTH, H_B * DV_B), 0.02),
        'hgrn_lb_logits': nrm(ks[17], (DEPTH, H_C * DK_C), 0.1),
        'hgrn_norm_g': 1.0 + nrm(ks[18], (DEPTH, H_C * DV_C), 0.02),
        'w_out': nrm(ks[19], (DEPTH, D_MIX, D_MODEL), D_MIX ** -0.5),
        'norm_ffn_g': 1.0 + nrm(ks[20], (DEPTH, D_MODEL), 0.02),
        'peer_w_q': nrm(ks[21], (DEPTH, D_MODEL, PEER_HEADS * PEER_QDIM), D_MODEL ** -0.5),
        'peer_subkeys': nrm(ks[22], (DEPTH, PEER_HEADS, 2, N_KEYS, PEER_QDIM // 2), (PEER_QDIM // 2) ** -0.5),
        'peer_u': nrm(ks[23], (DEPTH, N_EXPERTS, D_MODEL), D_MODEL ** -0.5),
        'peer_v': nrm(ks[24], (DEPTH, N_EXPERTS, D_MODEL), PEER_HEADS ** -0.5),
        'final_norm_g': 1.0 + nrm(ks[25], (D_MODEL,), 0.02),
    }


def reference(x_prompt, x_sample, state_mlstm_C, state_mlstm_n, state_mlstm_m, state_gdn_S, state_gdn_conv,
              state_hgrn_S, norm_mix_g, w_in, mlstm_b_i, mlstm_b_f, mlstm_norm_g, gdn_conv_w, gdn_A_log,
              gdn_dt_bias, gdn_norm_g, hgrn_lb_logits, hgrn_norm_g, w_out, norm_ffn_g, peer_w_q, peer_subkeys,
              peer_u, peer_v, final_norm_g):
    weights = (norm_mix_g, w_in, mlstm_b_i, mlstm_b_f, mlstm_norm_g, gdn_conv_w, gdn_A_log, gdn_dt_bias,
               gdn_norm_g, hgrn_lb_logits, hgrn_norm_g, w_out, norm_ffn_g, peer_w_q, peer_subkeys, peer_u, peer_v)
    bp = x_prompt.shape[0]
    dt = x_prompt.dtype
    prompt_states = (jnp.zeros((DEPTH, bp, H_A, DK_A, DV_A), dt), jnp.zeros((DEPTH, bp, H_A, DK_A), dt),
                     jnp.zeros((DEPTH, bp, H_A), dt), jnp.zeros((DEPTH, bp, H_B, DK_B, DV_B), dt),
                     jnp.zeros((DEPTH, bp, CONV_W - 1, GDN_CONV_DIM), dt), jnp.zeros((DEPTH, bp, H_C, DK_C, DV_C), dt))
    y_prompt, (pC, pn, pm, pS, pbuf, pH) = trunk(x_prompt, prompt_states, weights, final_norm_g)
    sample_states = (state_mlstm_C, state_mlstm_n, state_mlstm_m, state_gdn_S, state_gdn_conv, state_hgrn_S)
    y_sample, (sC, sn, sm, sS, sbuf, sH) = trunk(x_sample, sample_states, weights, final_norm_g)
    return (y_prompt, y_sample, pC, sC, pn, sn, pm, sm, pS, sS, pbuf, sbuf, pH, sH)
```

```python
import functools

import jax
import jax.numpy as jnp
from jax import lax
from jax.experimental import pallas as pl
from jax.experimental.pallas import tpu as pltpu

F32 = jnp.float32
BF16 = jnp.bfloat16
EPS = 1e-6
NEG_INF = float("-inf")

H_A, DK_A, DV_A = 6, 128, 256
H_B, DK_B, DV_B = 12, 128, 128
H_C, DK_C, DV_C = 8, 128, 128
CONV_W = 4
PEER_HEADS, N_KEYS, PEER_TOPK = 8, 128, 16
SAMPLE_PAD = 8
LANE = 128
VMEM_LIMIT = 56 * 1024 * 1024

A_W = H_A * DV_A
B_OFF = 3 * A_W
G_OFF = B_OFF + 4 * H_B * DV_B
C_OFF = G_OFF + 512
NP = C_OFF + 4 * H_C * DV_C
GATE_ROWS = 40
IN_SPLITS = (H_A * DK_A, H_A * DK_A, H_A * DV_A, H_A * DV_A, H_A, H_A,
             H_B * DK_B, H_B * DK_B, H_B * DV_B, H_B * DV_B, H_B, H_B,
             H_C * DK_C, H_C * DK_C, H_C * DV_C, H_C * DV_C)


def _cparams(sem):
    return pltpu.CompilerParams(dimension_semantics=sem, vmem_limit_bytes=VMEM_LIMIT)


def _mm(a, b):
    return jnp.dot(a.astype(BF16), b.astype(BF16), preferred_element_type=F32)


def _mm_nt(a, b):
    return lax.dot_general(a.astype(BF16), b.astype(BF16), (((1,), (1,)), ((), ())), preferred_element_type=F32)


def _mm_tn(a, b):
    return lax.dot_general(a.astype(BF16), b.astype(BF16), (((0,), (0,)), ((), ())), preferred_element_type=F32)


def _mmf(a, b):
    return jnp.dot(a, b, preferred_element_type=F32, precision=lax.Precision.HIGHEST)


def _mmf_nt(a, b):
    return lax.dot_general(a, b, (((1,), (1,)), ((), ())), preferred_element_type=F32,
                           precision=lax.Precision.HIGHEST)


def _softplus(x):
    return jnp.maximum(x, 0.0) + jnp.log1p(jnp.exp(-jnp.abs(x)))


def _log_sigmoid(x):
    return -_softplus(-x)


def _sigmoid(x):
    return 1.0 / (1.0 + jnp.exp(-x))


def _silu(x):
    return x * _sigmoid(x)


def _head_norm(x, g_row):
    return x * lax.rsqrt(jnp.mean(x * x, axis=-1, keepdims=True) + EPS) * g_row


def _rms_kernel(x_ref, g_ref, h_ref):
    x = x_ref[...]
    y = x * lax.rsqrt(jnp.mean(x * x, axis=-1, keepdims=True) + EPS)
    h_ref[...] = (y * g_ref[...]).astype(h_ref.dtype)


def rmsnorm_call(x, g, out_dtype, tb=256):
    t, d = x.shape
    return pl.pallas_call(
        _rms_kernel, grid=(t // tb,),
        in_specs=[pl.BlockSpec((tb, d), lambda i: (i, 0)), pl.BlockSpec((1, d), lambda i: (0, 0))],
        out_specs=pl.BlockSpec((tb, d), lambda i: (i, 0)),
        out_shape=jax.ShapeDtypeStruct((t, d), out_dtype),
        compiler_params=_cparams(("parallel",)), name="rmsnorm",
    )(x, g.reshape(1, d).astype(F32))


def _res_rms_kernel(x_ref, yt_ref, g_ref, xn_ref, h_ref):
    x = x_ref[...] + yt_ref[...].T
    xn_ref[...] = x
    y = x * lax.rsqrt(jnp.mean(x * x, axis=-1, keepdims=True) + EPS)
    h_ref[...] = (y * g_ref[...]).astype(h_ref.dtype)


def res_rmsnorm_call(x, y_t, g, out_dtype, tb=256):
    t, d = x.shape
    return pl.pallas_call(
        _res_rms_kernel, grid=(t // tb,),
        in_specs=[pl.BlockSpec((tb, d), lambda i: (i, 0)), pl.BlockSpec((d, tb), lambda i: (0, i)),
                  pl.BlockSpec((1, d), lambda i: (0, 0))],
        out_specs=[pl.BlockSpec((tb, d), lambda i: (i, 0)), pl.BlockSpec((tb, d), lambda i: (i, 0))],
        out_shape=[jax.ShapeDtypeStruct((t, d), F32), jax.ShapeDtypeStruct((t, d), out_dtype)],
        compiler_params=_cparams(("parallel",)), name="res_rmsnorm",
    )(x, y_t, g.reshape(1, d).astype(F32))


def _matmul_kernel(*refs, n_pairs, has_res):
    a_refs, w_refs = refs[:n_pairs], refs[n_pairs:2 * n_pairs]
    o_ref = refs[-1]
    acc = _mm(a_refs[0][...], w_refs[0][...])
    for a_ref, w_ref in zip(a_refs[1:], w_refs[1:]):
        acc = acc + _mm(a_ref[...], w_ref[...])
    if has_res:
        acc = acc + refs[2 * n_pairs][...]
    o_ref[...] = acc


def matmul_call(a_list, w_list, res=None, tm=512, tn=512, name="matmul"):
    t = a_list[0].shape[0]
    n = w_list[0].shape[1]
    in_specs = [pl.BlockSpec((tm, a.shape[1]), lambda i, j: (i, 0)) for a in a_list]
    in_specs += [pl.BlockSpec((w.shape[0], tn), lambda i, j: (0, j)) for w in w_list]
    args = list(a_list) + list(w_list)
    if res is not None:
        in_specs.append(pl.BlockSpec((tm, tn), lambda i, j: (i, j)))
        args.append(res)
    return pl.pallas_call(
        functools.partial(_matmul_kernel, n_pairs=len(a_list), has_res=res is not None),
        grid=(t // tm, n // tn), in_specs=in_specs,
        out_specs=pl.BlockSpec((tm, tn), lambda i, j: (i, j)),
        out_shape=jax.ShapeDtypeStruct((t, n), F32),
        compiler_params=_cparams(("parallel", "parallel")), name=name,
    )(*args)


def _tri_masks(c):
    row = lax.broadcasted_iota(jnp.int32, (c, c), 0)
    col = lax.broadcasted_iota(jnp.int32, (c, c), 1)
    return row, col


def _cumsum_col_row(x_c, x_r, row, col):
    cs_c = jnp.sum(jnp.where(row >= col, x_r, 0.0), axis=1, keepdims=True)
    cs_r = jnp.sum(jnp.where(row <= col, x_c, 0.0), axis=0, keepdims=True)
    return cs_c, cs_r


def _mlstm_kernel(qk_ref, v_ref, o_ref, gc_ref, gt_ref, brow_ref, bcol_ref, ng_ref, c0_ref, n0_ref, m0_ref,
                  out_ref, c_ref, n_ref, m_ref, *, c, n_valid, has_init):
    ci = pl.program_id(1)

    @pl.when(ci == 0)
    def _():
        if has_init:
            c_ref[...] = c0_ref[...]
            n_ref[...] = n0_ref[...]
            m_ref[...] = m0_ref[...]
        else:
            c_ref[...] = jnp.zeros_like(c_ref)
            n_ref[...] = jnp.zeros_like(n_ref)
            m_ref[...] = jnp.zeros_like(m_ref)

    gc = gc_ref[...] + brow_ref[...]
    gt = gt_ref[0] + bcol_ref[...]
    row, col = _tri_masks(c)
    tri = row >= col
    padded = n_valid < c
    if padded:
        valid_c = lax.broadcasted_iota(jnp.int32, (c, 1), 0) < n_valid
        valid_r = lax.broadcasted_iota(jnp.int32, (1, c), 1) < n_valid
    scale = DK_A ** -0.5
    for h in range(H_A):
        ig_c, lf_c = gc[:, h:h + 1], _log_sigmoid(gc[:, H_A + h:H_A + h + 1])
        ig_r, lf_r = gt[h:h + 1, :], _log_sigmoid(gt[H_A + h:H_A + h + 1, :])
        if padded:
            ig_c, lf_c = jnp.where(valid_c, ig_c, NEG_INF), jnp.where(valid_c, lf_c, 0.0)
            ig_r, lf_r = jnp.where(valid_r, ig_r, NEG_INF), jnp.where(valid_r, lf_r, 0.0)
        q = qk_ref[:, h * DK_A:(h + 1) * DK_A] * scale
        k = qk_ref[:, H_A * DK_A + h * DK_A:H_A * DK_A + (h + 1) * DK_A]
        v = v_ref[:, h * DV_A:(h + 1) * DV_A]
        cst = c_ref[0, h]
        nst = n_ref[0, h:h + 1, :]
        m_prev = m_ref[0, :, h:h + 1]
        b_c, b_r = _cumsum_col_row(lf_c, lf_r, row, col)
        log_d = jnp.where(tri, b_c - b_r + ig_r, NEG_INF)
        m_inter = b_c + m_prev
        m_t = jnp.maximum(m_inter, jnp.max(log_d, axis=1, keepdims=True))
        s = _mm_nt(q, k) * jnp.exp(log_d - m_t)
        inter = jnp.exp(m_inter - m_t)
        num = _mm(s, v) + inter * _mm(q, cst)
        den = jnp.sum(s, axis=1, keepdims=True) + inter * jnp.sum(q * nst, axis=1, keepdims=True)
        hh = num / jnp.maximum(jnp.abs(den), jnp.exp(-m_t))
        m_new = m_t[c - 1:c, :]
        w = jnp.exp(b_c[c - 1:c, :] - b_c + ig_c - m_new)
        decay = inter[c - 1:c, :]
        kw = k * w
        c_ref[0, h] = decay * cst + _mm_tn(kw, v)
        n_ref[0, h:h + 1, :] = decay * nst + jnp.sum(kw, axis=0, keepdims=True)
        m_ref[0, :, h:h + 1] = m_new
        gate = _sigmoid(o_ref[:, h * DV_A:(h + 1) * DV_A])
        out_ref[:, h * DV_A:(h + 1) * DV_A] = _head_norm(hh, ng_ref[:, h * DV_A:(h + 1) * DV_A]) * gate


def mlstm_call(proj, gates_t, brow, bcol, norm_g, state, *, row0, nb, nchunk, c, n_valid):
    has_init = state is not None
    rb = row0 // c
    tok = lambda b, ci: rb + b * nchunk + ci
    st_shapes = [jax.ShapeDtypeStruct((nb, H_A, DK_A, DV_A), F32), jax.ShapeDtypeStruct((nb, H_A, DK_A), F32),
                 jax.ShapeDtypeStruct((nb, 1, H_A), F32)]
    st_specs = [pl.BlockSpec((1, H_A, DK_A, DV_A), lambda b, ci: (b, 0, 0, 0)),
                pl.BlockSpec((1, H_A, DK_A), lambda b, ci: (b, 0, 0)),
                pl.BlockSpec((1, 1, H_A), lambda b, ci: (b, 0, 0))]
    if has_init:
        c0, n0, m0 = state[0], state[1], state[2].reshape(nb, 1, H_A)
    else:
        c0, n0, m0 = (jnp.zeros((1,) + s.shape[1:], F32) for s in st_shapes)
        st_in_specs = [pl.BlockSpec((1, H_A, DK_A, DV_A), lambda b, ci: (0, 0, 0, 0)),
                       pl.BlockSpec((1, H_A, DK_A), lambda b, ci: (0, 0, 0)),
                       pl.BlockSpec((1, 1, H_A), lambda b, ci: (0, 0, 0))]
    out, c1, n1, m1 = pl.pallas_call(
        functools.partial(_mlstm_kernel, c=c, n_valid=n_valid, has_init=has_init),
        grid=(nb, nchunk),
        in_specs=[pl.BlockSpec((c, A_W), lambda b, ci: (tok(b, ci), 0)),
                  pl.BlockSpec((c, A_W), lambda b, ci: (tok(b, ci), 1)),
                  pl.BlockSpec((c, A_W), lambda b, ci: (tok(b, ci), 2)),
                  pl.BlockSpec((c, LANE), lambda b, ci: (tok(b, ci), G_OFF // LANE)),
                  pl.BlockSpec((1, GATE_ROWS, c), lambda b, ci: (b * nchunk + ci, 0, 0)),
                  pl.BlockSpec((1, LANE), lambda b, ci: (0, 0)),
                  pl.BlockSpec((GATE_ROWS, 1), lambda b, ci: (0, 0)),
                  pl.BlockSpec((1, A_W), lambda b, ci: (0, 0))] + (st_specs if has_init else st_in_specs),
        out_specs=[pl.BlockSpec((c, A_W), lambda b, ci: (b * nchunk + ci, 0))] + st_specs,
        out_shape=[jax.ShapeDtypeStruct((nb * nchunk * c, A_W), F32)] + st_shapes,
        compiler_params=_cparams(("parallel", "arbitrary")), name="mlstm",
    )(proj, proj, proj, proj, gates_t, brow, bcol, norm_g.reshape(1, A_W).astype(F32), c0, n0, m0)
    return out, c1, n1, m1.reshape(nb, H_A)


def _unit_lower_inverse(a, row, col, c):
    eye = jnp.where(row == col, 1.0, 0.0)
    d = eye
    s = 1
    while s < c:
        blk = jnp.where(((row // (2 * s)) == (col // (2 * s))) & ((row % (2 * s)) >= s) & ((col % (2 * s)) < s),
                        a, 0.0)
        if s == 1:
            d = d - blk
        else:
            d = d - _mmf(_mmf(d, blk), d)
        s *= 2
    return d


def _gdn_kernel(qkv_ref, z_ref, gc_ref, gt_ref, brow_ref, bcol_ref, arow_ref, acol_ref, cw_ref, buf0_ref, ng_ref,
                s0_ref, out_ref, s_ref, xp_ref, *, c, n_valid, has_init):
    ci = pl.program_id(1)
    hk = H_B * DK_B

    @pl.when(ci == 0)
    def _():
        xp_ref[0:8, :] = jnp.zeros((8, xp_ref.shape[1]), F32)
        if has_init:
            s_ref[...] = s0_ref[...]
            xp_ref[8 - (CONV_W - 1):8, :] = buf0_ref[0]
        else:
            s_ref[...] = jnp.zeros_like(s_ref)

    @pl.when(ci > 0)
    def _():
        xp_ref[0:8, :] = xp_ref[c:c + 8, :]

    xp_ref[8:8 + c, :] = qkv_ref[...]

    gc = gc_ref[...] + brow_ref[...]
    gt = gt_ref[0] + bcol_ref[...]
    row, col = _tri_masks(c)
    tri = row >= col
    strict = row > col
    padded = n_valid < c
    if padded:
        valid_c = lax.broadcasted_iota(jnp.int32, (c, 1), 0) < n_valid
        valid_r = lax.broadcasted_iota(jnp.int32, (1, c), 1) < n_valid
    ga, gb = 2 * H_A, 2 * H_A + H_B

    def conv_act(c0):
        y = cw_ref[0:1, c0:c0 + DK_B] * xp_ref[8 - 3:8 - 3 + c, c0:c0 + DK_B]
        for j in range(1, CONV_W):
            y = y + cw_ref[j:j + 1, c0:c0 + DK_B] * xp_ref[8 - 3 + j:8 - 3 + j + c, c0:c0 + DK_B]
        return _silu(y)

    for h in range(H_B):
        g_c = -jnp.exp(arow_ref[:, ga + h:ga + h + 1]) * _softplus(gc[:, ga + h:ga + h + 1])
        g_r = -jnp.exp(acol_ref[ga + h:ga + h + 1, :]) * _softplus(gt[ga + h:ga + h + 1, :])
        beta = _sigmoid(gc[:, gb + h:gb + h + 1])
        if padded:
            g_c, g_r = jnp.where(valid_c, g_c, 0.0), jnp.where(valid_r, g_r, 0.0)
            beta = jnp.where(valid_c, beta, 0.0)
        q = conv_act(h * DK_B)
        k = conv_act(hk + h * DK_B)
        v = conv_act(2 * hk + h * DV_B)
        q = q * lax.rsqrt(jnp.sum(q * q, axis=1, keepdims=True) + EPS) * (DK_B ** -0.5)
        k = k * lax.rsqrt(jnp.sum(k * k, axis=1, keepdims=True) + EPS)
        sst = s_ref[0, h]
        gcum_c, gcum_r = _cumsum_col_row(g_c, g_r, row, col)
        gam = jnp.exp(jnp.where(tri, gcum_c - gcum_r, NEG_INF))
        eg = jnp.exp(gcum_c)
        a = jnp.where(strict, beta * _mm_nt(k, k) * gam, 0.0)
        tinv = _unit_lower_inverse(a, row, col, c)
        rhs = jnp.concatenate([beta * v, (beta * eg) * k], axis=1)
        sol = _mmf(tinv, rhs)
        u = sol[:, :DV_B] - _mm(sol[:, DV_B:], sst)
        o = eg * _mm(q, sst) + _mm(_mm_nt(q, k) * gam, u)
        g_last = gcum_c[c - 1:c, :]
        s_ref[0, h] = jnp.exp(g_last) * sst + _mm_tn(k * jnp.exp(g_last - gcum_c), u)
        gate = _silu(z_ref[:, h * DV_B:(h + 1) * DV_B])
        out_ref[:, h * DV_B:(h + 1) * DV_B] = _head_norm(o, ng_ref[:, h * DV_B:(h + 1) * DV_B]) * gate


def gdn_call(proj, gates_t, brow, bcol, arow, acol, conv_w, norm_g, state, *, row0, nb, nchunk, c, n_valid):
    has_init = state is not None
    rb = row0 // c
    tok = lambda b, ci: rb + b * nchunk + ci
    wb = 3 * H_B * DK_B
    s_spec = pl.BlockSpec((1, H_B, DK_B, DV_B), lambda b, ci: (b, 0, 0, 0))
    if has_init:
        s0, buf0 = state
        s0_spec, buf_spec = s_spec, pl.BlockSpec((1, CONV_W - 1, wb), lambda b, ci: (b, 0, 0))
    else:
        s0, buf0 = jnp.zeros((1, H_B, DK_B, DV_B), F32), jnp.zeros((1, CONV_W - 1, wb), F32)
        s0_spec = pl.BlockSpec((1, H_B, DK_B, DV_B), lambda b, ci: (0, 0, 0, 0))
        buf_spec = pl.BlockSpec((1, CONV_W - 1, wb), lambda b, ci: (0, 0, 0))
    out, s1 = pl.pallas_call(
        functools.partial(_gdn_kernel, c=c, n_valid=n_valid, has_init=has_init),
        grid=(nb, nchunk),
        in_specs=[pl.BlockSpec((c, wb), lambda b, ci: (tok(b, ci), B_OFF // wb)),
                  pl.BlockSpec((c, A_W), lambda b, ci: (tok(b, ci), (B_OFF + wb) // A_W)),
                  pl.BlockSpec((c, LANE), lambda b, ci: (tok(b, ci), G_OFF // LANE)),
                  pl.BlockSpec((1, GATE_ROWS, c), lambda b, ci: (b * nchunk + ci, 0, 0)),
                  pl.BlockSpec((1, LANE), lambda b, ci: (0, 0)),
                  pl.BlockSpec((GATE_ROWS, 1), lambda b, ci: (0, 0)),
                  pl.BlockSpec((1, LANE), lambda b, ci: (0, 0)),
                  pl.BlockSpec((GATE_ROWS, 1), lambda b, ci: (0, 0)),
                  pl.BlockSpec((CONV_W, wb), lambda b, ci: (0, 0)),
                  buf_spec,
                  pl.BlockSpec((1, H_B * DV_B), lambda b, ci: (0, 0)),
                  s0_spec],
        out_specs=[pl.BlockSpec((c, H_B * DV_B), lambda b, ci: (b * nchunk + ci, 0)), s_spec],
        out_shape=[jax.ShapeDtypeStruct((nb * nchunk * c, H_B * DV_B), F32),
                   jax.ShapeDtypeStruct((nb, H_B, DK_B, DV_B), F32)],
        scratch_shapes=[pltpu.VMEM((c + 8, wb), F32)],
        compiler_params=_cparams(("parallel", "arbitrary")), name="gdn",
    )(proj, proj, proj, gates_t, brow, bcol, arow, acol, conv_w.astype(F32), buf0,
      norm_g.reshape(1, H_B * DV_B).astype(F32), s0)
    return out, s1


def _hgrn_kernel(q_ref, f_ref, i_ref, g_ref, lb_ref, ng_ref, s0_ref, out_ref, s_ref, st_ref, *,
                 c, n_valid, has_init, nchunk):
    ci = pl.program_id(1)

    @pl.when(ci == 0)
    def _():
        for h in range(H_C):
            st_ref[h] = s0_ref[0, h].T if has_init else jnp.zeros((DV_C, DK_C), F32)

    padded = n_valid < c
    rowi = lax.broadcasted_iota(jnp.int32, (c, 1), 0)
    for h in range(H_C):
        hs = slice(h * DK_C, (h + 1) * DK_C)
        cf = f_ref[:, hs]
        lb = lb_ref[:, hs]
        la = jnp.log1p(-lb) + _log_sigmoid(cf)
        lbl = jnp.log(lb)
        logf = jnp.maximum(la, lbl) + jnp.log1p(jnp.exp(-jnp.abs(la - lbl)))
        k = (1.0 - lb) * _sigmoid(-cf)
        if padded:
            logf, k = jnp.where(rowi < n_valid, logf, 0.0), jnp.where(rowi < n_valid, k, 0.0)
        q = _silu(q_ref[:, hs])
        iv = i_ref[:, hs]
        bc = jnp.zeros((c, DK_C), F32)
        for s in range(c):
            bc = bc + jnp.where(rowi >= s, logf[s:s + 1, :], 0.0)
        o = _mm_nt(q * jnp.exp(bc), st_ref[h])
        for s in range(c):
            dec = jnp.exp(jnp.where(rowi >= s, bc - bc[s:s + 1, :], NEG_INF))
            att = jnp.sum(q * dec * k[s:s + 1, :], axis=1, keepdims=True)
            o = o + att * iv[s:s + 1, :]
        b_last = bc[c - 1:c, :]
        st_ref[h] = jnp.exp(b_last) * st_ref[h] + _mm_tn(iv, k * jnp.exp(b_last - bc))
        gate = _silu(g_ref[:, hs])
        out_ref[:, hs] = _head_norm(o, ng_ref[:, hs]) * gate

    @pl.when(ci == nchunk - 1)
    def _():
        for h in range(H_C):
            s_ref[0, h] = st_ref[h].T


def hgrn_call(proj, lb, norm_g, state, *, row0, nb, nchunk, c, n_valid):
    has_init = state is not None
    rb = row0 // c
    tok = lambda b, ci: rb + b * nchunk + ci
    w = H_C * DK_C
    cb = C_OFF // w
    s_spec = pl.BlockSpec((1, H_C, DK_C, DV_C), lambda b, ci: (b, 0, 0, 0))
    if has_init:
        s0, s0_spec = state, s_spec
    else:
        s0 = jnp.zeros((1, H_C, DK_C, DV_C), F32)
        s0_spec = pl.BlockSpec((1, H_C, DK_C, DV_C), lambda b, ci: (0, 0, 0, 0))
    out, s1 = pl.pallas_call(
        functools.partial(_hgrn_kernel, c=c, n_valid=n_valid, has_init=has_init, nchunk=nchunk),
        grid=(nb, nchunk),
        in_specs=[pl.BlockSpec((c, w), lambda b, ci: (tok(b, ci), cb)),
                  pl.BlockSpec((c, w), lambda b, ci: (tok(b, ci), cb + 1)),
                  pl.BlockSpec((c, w), lambda b, ci: (tok(b, ci), cb + 2)),
                  pl.BlockSpec((c, w), lambda b, ci: (tok(b, ci), cb + 3)),
                  pl.BlockSpec((1, w), lambda b, ci: (0, 0)),
                  pl.BlockSpec((1, w), lambda b, ci: (0, 0)),
                  s0_spec],
        out_specs=[pl.BlockSpec((c, w), lambda b, ci: (b * nchunk + ci, 0)), s_spec],
        out_shape=[jax.ShapeDtypeStruct((nb * nchunk * c, w), F32),
                   jax.ShapeDtypeStruct((nb, H_C, DK_C, DV_C), F32)],
        scratch_shapes=[pltpu.VMEM((H_C, DV_C, DK_C), F32)],
        compiler_params=_cparams(("parallel", "arbitrary")), name="hgrn",
    )(proj, proj, proj, proj, lb.reshape(1, w).astype(F32), norm_g.reshape(1, w).astype(F32), s0)
    return out, s1


def _top_values(x, k):
    rows = []
    for _ in range(k):
        mx = jnp.max(x, axis=0, keepdims=True)
        rows.append(mx)
        x = jnp.where(x == mx, NEG_INF, x)
    return jnp.concatenate(rows, axis=0)


def _route_kernel(q_ref, sk_ref, s1_ref, s2_ref, e1_ref, e2_ref, tau_ref):
    for h in range(PEER_HEADS):
        s1 = _mmf_nt(sk_ref[h, 0], q_ref[:, (2 * h) * LANE:(2 * h + 1) * LANE])
        s2 = _mmf_nt(sk_ref[h, 1], q_ref[:, (2 * h + 1) * LANE:(2 * h + 2) * LANE])
        a = _top_values(s1, PEER_TOPK)
        b = _top_values(s2, PEER_TOPK)
        cand = jnp.concatenate([a[i:i + 1, :] + b for i in range(PEER_TOPK)], axis=0)
        best = _top_values(cand, PEER_TOPK)
        z = jnp.sum(jnp.exp(best - best[0:1, :]), axis=0, keepdims=True)
        s1_ref[h] = s1
        s2_ref[h] = s2
        e1_ref[h] = jnp.exp(s1 - a[0:1, :])
        e2_ref[h] = jnp.exp(s2 - b[0:1, :]) / z
        tau_ref[h:h + 1, :] = best[PEER_TOPK - 1:PEER_TOPK, :]


def route_call(qry, subkeys, tb=256):
    t = qry.shape[0]
    big = jax.ShapeDtypeStruct((PEER_HEADS, N_KEYS, t), F32)
    big_spec = pl.BlockSpec((PEER_HEADS, N_KEYS, tb), lambda i: (0, 0, i))
    return pl.pallas_call(
        _route_kernel, grid=(t // tb,),
        in_specs=[pl.BlockSpec((tb, qry.shape[1]), lambda i: (i, 0)),
                  pl.BlockSpec(subkeys.shape, lambda i: (0, 0, 0, 0))],
        out_specs=[big_spec] * 4 + [pl.BlockSpec((PEER_HEADS, tb), lambda i: (0, i))],
        out_shape=[big] * 4 + [jax.ShapeDtypeStruct((PEER_HEADS, t), F32)],
        compiler_params=_cparams(("parallel",)), name="peer_route",
    )(qry, subkeys.astype(F32))


def _peer_kernel(t_ref, u_ref, vt_ref, s1_ref, s2_ref, e1_ref, e2_ref, tau_ref, yt_ref, *, eb):
    j = pl.program_id(1)

    @pl.when(j == 0)
    def _():
        yt_ref[...] = jnp.zeros_like(yt_ref)

    a_t = lax.dot_general(u_ref[...], t_ref[...], (((1,), (1,)), ((), ())), preferred_element_type=F32)
    coefs = []
    for r in range(eb // N_KEYS):
        i1 = j * (eb // N_KEYS) + r
        gsum = None
        for h in range(PEER_HEADS):
            sc = s1_ref[h, pl.ds(i1, 1), :] + s2_ref[h]
            gh = jnp.where(sc >= tau_ref[h:h + 1, :], e1_ref[h, pl.ds(i1, 1), :] * e2_ref[h], 0.0)
            gsum = gh if gsum is None else gsum + gh
        coefs.append((jax.nn.gelu(a_t[r * N_KEYS:(r + 1) * N_KEYS, :]) * gsum).astype(BF16))
    coef = coefs[0] if len(coefs) == 1 else jnp.concatenate(coefs, axis=0)
    yt_ref[...] += jnp.dot(vt_ref[...], coef, preferred_element_type=F32)


def peer_call(t_bf, u_bf, vt_bf, s1, s2, e1, e2, tau, tm=512, eb=256):
    t, d = t_bf.shape
    ne = u_bf.shape[0]
    big_spec = pl.BlockSpec((PEER_HEADS, N_KEYS, tm), lambda i, j: (0, 0, i))
    return pl.pallas_call(
        functools.partial(_peer_kernel, eb=eb),
        grid=(t // tm, ne // eb),
        in_specs=[pl.BlockSpec((tm, d), lambda i, j: (i, 0)),
                  pl.BlockSpec((eb, d), lambda i, j: (j, 0)),
                  pl.BlockSpec((d, eb), lambda i, j: (0, j)),
                  big_spec, big_spec, big_spec, big_spec,
                  pl.BlockSpec((PEER_HEADS, tm), lambda i, j: (0, i))],
        out_specs=pl.BlockSpec((d, tm), lambda i, j: (0, i)),
        out_shape=jax.ShapeDtypeStruct((d, t), F32),
        compiler_params=_cparams(("parallel", "arbitrary")), name="peer_dense",
    )(t_bf, u_bf, vt_bf, s1, s2, e1, e2, tau)


def _permute_w_in(w):
    offs = [0]
    for s in IN_SPLITS:
        offs.append(offs[-1] + s)
    a_main, a_gate = (offs[0], offs[4]), (offs[4], offs[6])
    b_main, b_gate = (offs[6], offs[10]), (offs[10], offs[12])
    c_main = (offs[12], offs[16])
    pad = jnp.zeros((w.shape[0], C_OFF - G_OFF - (a_gate[1] - a_gate[0]) - (b_gate[1] - b_gate[0])), w.dtype)
    parts = [w[:, a_main[0]:a_main[1]], w[:, b_main[0]:b_main[1]], w[:, a_gate[0]:a_gate[1]],
             w[:, b_gate[0]:b_gate[1]], pad, w[:, c_main[0]:c_main[1]]]
    return jnp.concatenate(parts, axis=1).astype(BF16)


def _gate_params(b_i, b_f, a_log, dt_bias):
    zeros = jnp.zeros((LANE - 2 * H_A - 2 * H_B,), F32)
    bias = jnp.concatenate([b_i.astype(F32), b_f.astype(F32), dt_bias.astype(F32), jnp.zeros((H_B,), F32), zeros])
    alog = jnp.concatenate([jnp.zeros((2 * H_A,), F32), a_log.astype(F32), jnp.zeros((H_B,), F32), zeros])
    return (bias.reshape(1, LANE), bias[:GATE_ROWS].reshape(GATE_ROWS, 1),
            alog.reshape(1, LANE), alog[:GATE_ROWS].reshape(GATE_ROWS, 1))


def _gates_rows(proj, c):
    g = proj[:, G_OFF:G_OFF + GATE_ROWS]
    return g.reshape(g.shape[0] // c, c, GATE_ROWS).transpose(0, 2, 1)


def _hgrn_lower_bounds(logits):
    p = jax.nn.softmax(logits.astype(F32), axis=0)
    cs = jnp.cumsum(p, axis=0)
    return cs - cs[0:1]


def kernel(x_prompt, x_sample, state_mlstm_C, state_mlstm_n, state_mlstm_m, state_gdn_S, state_gdn_conv,
           state_hgrn_S, norm_mix_g, w_in, mlstm_b_i, mlstm_b_f, mlstm_norm_g, gdn_conv_w, gdn_A_log,
           gdn_dt_bias, gdn_norm_g, hgrn_lb_logits, hgrn_norm_g, w_out, norm_ffn_g, peer_w_q, peer_subkeys,
           peer_u, peer_v, final_norm_g, *, chunk=64, hgrn_chunk=16, tm=512, tm_peer=512, eb_peer=256, tb=256):
    bp, lp, d = x_prompt.shape
    bs, ls, _ = x_sample.shape
    depth = w_in.shape[0]
    tp = bp * lp
    ts = bs * SAMPLE_PAD
    xs = jnp.pad(x_sample, ((0, 0), (0, SAMPLE_PAD - ls), (0, 0)))
    x = jnp.concatenate([x_prompt.reshape(tp, d), xs.reshape(ts, d)], axis=0).astype(F32)
    lbs = _hgrn_lower_bounds(hgrn_lb_logits)
    wb = 3 * H_B * DK_B
    new = [[] for _ in range(12)]
    y_t = None
    for l in range(depth):
        if l == 0:
            h = rmsnorm_call(x, norm_mix_g[l], BF16, tb=tb)
        else:
            x, h = res_rmsnorm_call(x, y_t, norm_mix_g[l], BF16, tb=tb)
        proj = matmul_call([h], [_permute_w_in(w_in[l])], tm=tm, name="in_proj")
        brow, bcol, arow, acol = _gate_params(mlstm_b_i[l], mlstm_b_f[l], gdn_A_log[l], gdn_dt_bias[l])
        gt_p = _gates_rows(proj[:tp], chunk)
        gt_s = _gates_rows(proj[tp:], SAMPLE_PAD)
        pk = dict(row0=0, nb=bp, nchunk=lp // chunk, c=chunk, n_valid=chunk)
        sk = dict(row0=tp, nb=bs, nchunk=1, c=SAMPLE_PAD, n_valid=ls)
        a_p, c_p, n_p, m_p = mlstm_call(proj, gt_p, brow, bcol, mlstm_norm_g[l], None, **pk)
        a_s, c_s, n_s, m_s = mlstm_call(proj, gt_s, brow, bcol, mlstm_norm_g[l],
                                        (state_mlstm_C[l], state_mlstm_n[l], state_mlstm_m[l]), **sk)
        b_p, s_p = gdn_call(proj, gt_p, brow, bcol, arow, acol, gdn_conv_w[l], gdn_norm_g[l], None, **pk)
        b_s, s_s = gdn_call(proj, gt_s, brow, bcol, arow, acol, gdn_conv_w[l], gdn_norm_g[l],
                            (state_gdn_S[l], state_gdn_conv[l]), **sk)
        hk = dict(pk, nchunk=lp // hgrn_chunk, c=hgrn_chunk, n_valid=hgrn_chunk)
        c_pp, h_p = hgrn_call(proj, lbs[l], hgrn_norm_g[l], None, **hk)
        c_ss, h_s = hgrn_call(proj, lbs[l], hgrn_norm_g[l], state_hgrn_S[l], **sk)
        buf_p = proj[:tp, B_OFF:B_OFF + wb].reshape(bp, lp, wb)[:, lp - (CONV_W - 1):]
        buf_s = proj[tp:, B_OFF:B_OFF + wb].reshape(bs, SAMPLE_PAD, wb)[:, ls - (CONV_W - 1):ls]
        for lst, s in zip(new, (c_p, c_s, n_p, n_s, m_p, m_s, s_p, s_s, buf_p, buf_s, h_p, h_s)):
            lst.append(s)
        mix_a = jnp.concatenate([a_p, a_s], axis=0)
        mix_b = jnp.concatenate([b_p, b_s], axis=0)
        mix_c = jnp.concatenate([c_pp, c_ss], axis=0)
        wo = w_out[l].astype(BF16)
        x = matmul_call([mix_a, mix_b, mix_c], [wo[:A_W], wo[A_W:2 * A_W], wo[2 * A_W:]], res=x, tm=tm,
                        name="out_proj")
        t_bf = rmsnorm_call(x, norm_ffn_g[l], BF16, tb=tb)
        qry = matmul_call([t_bf], [peer_w_q[l].astype(BF16)], tm=tm, name="peer_query")
        s1, s2, e1, e2, tau = route_call(qry, peer_subkeys[l], tb=tb)
        y_t = peer_call(t_bf, peer_u[l].astype(BF16), peer_v[l].astype(BF16).T, s1, s2, e1, e2, tau,
                        tm=tm_peer, eb=eb_peer)
    _, y = res_rmsnorm_call(x, y_t, final_norm_g, F32, tb=tb)
    y_prompt = y[:tp].reshape(bp, lp, d)
    y_sample = y[tp:].reshape(bs, SAMPLE_PAD, d)[:, :ls]
    states = [jnp.stack(s, axis=0) for s in new]
    return (y_prompt, y_sample, *states)
```

```python
import functools

import jax
import jax.numpy as jnp
from jax import lax
from jax.experimental import pallas as pl
from jax.experimental.pallas import tpu as pltpu

F32 = jnp.float32
BF16 = jnp.bfloat16
EPS = 1e-6
NEG_INF = float("-inf")

H_A, DK_A, DV_A = 6, 128, 256
H_B, DK_B, DV_B = 12, 128, 128
H_C, DK_C, DV_C = 8, 128, 128
CONV_W = 4
PEER_HEADS, N_KEYS, PEER_TOPK = 8, 128, 16
SAMPLE_PAD = 8
LANE = 128
VMEM_LIMIT = 56 * 1024 * 1024

A_W = H_A * DV_A
B_OFF = 3 * A_W
G_OFF = B_OFF + 4 * H_B * DV_B
C_OFF = G_OFF + 512
NP = C_OFF + 4 * H_C * DV_C
GATE_ROWS = 40
IN_SPLITS = (H_A * DK_A, H_A * DK_A, H_A * DV_A, H_A * DV_A, H_A, H_A,
             H_B * DK_B, H_B * DK_B, H_B * DV_B, H_B * DV_B, H_B, H_B,
             H_C * DK_C, H_C * DK_C, H_C * DV_C, H_C * DV_C)


def _cparams(sem):
    return pltpu.CompilerParams(dimension_semantics=sem, vmem_limit_bytes=VMEM_LIMIT)


def _mm(a, b):
    return jnp.dot(a.astype(BF16), b.astype(BF16), preferred_element_type=F32)


def _mm_nt(a, b):
    return lax.dot_general(a.astype(BF16), b.astype(BF16), (((1,), (1,)), ((), ())), preferred_element_type=F32)


def _mm_tn(a, b):
    return lax.dot_general(a.astype(BF16), b.astype(BF16), (((0,), (0,)), ((), ())), preferred_element_type=F32)


def _mmf(a, b):
    return jnp.dot(a, b, preferred_element_type=F32, precision=lax.Precision.HIGHEST)


def _mmf_nt(a, b):
    return lax.dot_general(a, b, (((1,), (1,)), ((), ())), preferred_element_type=F32,
                           precision=lax.Precision.HIGHEST)


def _softplus(x):
    return jnp.maximum(x, 0.0) + jnp.log1p(jnp.exp(-jnp.abs(x)))


def _log_sigmoid(x):
    return -_softplus(-x)


def _sigmoid(x):
    return 1.0 / (1.0 + jnp.exp(-x))


def _silu(x):
    return x * _sigmoid(x)


def _head_norm(x, g_row):
    return x * lax.rsqrt(jnp.mean(x * x, axis=-1, keepdims=True) + EPS) * g_row


def _rms_kernel(x_ref, g_ref, h_ref):
    x = x_ref[...]
    y = x * lax.rsqrt(jnp.mean(x * x, axis=-1, keepdims=True) + EPS)
    h_ref[...] = (y * g_ref[...]).astype(h_ref.dtype)


def rmsnorm_call(x, g, out_dtype, tb=256):
    t, d = x.shape
    return pl.pallas_call(
        _rms_kernel, grid=(t // tb,),
        in_specs=[pl.BlockSpec((tb, d), lambda i: (i, 0)), pl.BlockSpec((1, d), lambda i: (0, 0))],
        out_specs=pl.BlockSpec((tb, d), lambda i: (i, 0)),
        out_shape=jax.ShapeDtypeStruct((t, d), out_dtype),
        compiler_params=_cparams(("parallel",)), name="rmsnorm",
    )(x, g.reshape(1, d).astype(F32))


def _res_rms_kernel(x_ref, y_ref, g_ref, xn_ref, h_ref):
    x = x_ref[...] + y_ref[...]
    xn_ref[...] = x
    y = x * lax.rsqrt(jnp.mean(x * x, axis=-1, keepdims=True) + EPS)
    h_ref[...] = (y * g_ref[...]).astype(h_ref.dtype)


def res_rmsnorm_call(x, y, g, out_dtype, tb=256):
    t, d = x.shape
    return pl.pallas_call(
        _res_rms_kernel, grid=(t // tb,),
        in_specs=[pl.BlockSpec((tb, d), lambda i: (i, 0)), pl.BlockSpec((tb, d), lambda i: (i, 0)),
                  pl.BlockSpec((1, d), lambda i: (0, 0))],
        out_specs=[pl.BlockSpec((tb, d), lambda i: (i, 0)), pl.BlockSpec((tb, d), lambda i: (i, 0))],
        out_shape=[jax.ShapeDtypeStruct((t, d), F32), jax.ShapeDtypeStruct((t, d), out_dtype)],
        compiler_params=_cparams(("parallel",)), name="res_rmsnorm",
    )(x, y, g.reshape(1, d).astype(F32))


def _matmul_kernel(*refs, n_pairs, has_res):
    a_refs, w_refs = refs[:n_pairs], refs[n_pairs:2 * n_pairs]
    o_ref = refs[-1]
    acc = _mm(a_refs[0][...], w_refs[0][...])
    for a_ref, w_ref in zip(a_refs[1:], w_refs[1:]):
        acc = acc + _mm(a_ref[...], w_ref[...])
    if has_res:
        acc = acc + refs[2 * n_pairs][...]
    o_ref[...] = acc


def matmul_call(a_list, w_list, res=None, tm=512, tn=512, name="matmul"):
    t = a_list[0].shape[0]
    n = w_list[0].shape[1]
    in_specs = [pl.BlockSpec((tm, a.shape[1]), lambda i, j: (i, 0)) for a in a_list]
    in_specs += [pl.BlockSpec((w.shape[0], tn), lambda i, j: (0, j)) for w in w_list]
    args = list(a_list) + list(w_list)
    if res is not None:
        in_specs.append(pl.BlockSpec((tm, tn), lambda i, j: (i, j)))
        args.append(res)
    return pl.pallas_call(
        functools.partial(_matmul_kernel, n_pairs=len(a_list), has_res=res is not None),
        grid=(t // tm, n // tn), in_specs=in_specs,
        out_specs=pl.BlockSpec((tm, tn), lambda i, j: (i, j)),
        out_shape=jax.ShapeDtypeStruct((t, n), F32),
        compiler_params=_cparams(("parallel", "parallel")), name=name,
    )(*args)


def _tri_masks(c):
    row = lax.broadcasted_iota(jnp.int32, (c, c), 0)
    col = lax.broadcasted_iota(jnp.int32, (c, c), 1)
    return row, col


def _cumsum_col_row(x_c, x_r, row, col):
    cs_c = jnp.sum(jnp.where(row >= col, x_r, 0.0), axis=1, keepdims=True)
    cs_r = jnp.sum(jnp.where(row <= col, x_c, 0.0), axis=0, keepdims=True)
    return cs_c, cs_r


def _mlstm_kernel(qk_ref, v_ref, o_ref, gc_ref, gt_ref, brow_ref, bcol_ref, ng_ref, c0_ref, n0_ref, m0_ref,
                  out_ref, c_ref, n_ref, m_ref, *, c, n_valid, has_init):
    ci = pl.program_id(1)

    @pl.when(ci == 0)
    def _():
        if has_init:
            c_ref[...] = c0_ref[...]
            n_ref[...] = n0_ref[...]
            m_ref[...] = m0_ref[...]
        else:
            c_ref[...] = jnp.zeros_like(c_ref)
            n_ref[...] = jnp.zeros_like(n_ref)
            m_ref[...] = jnp.zeros_like(m_ref)

    gc = gc_ref[...] + brow_ref[...]
    gt = gt_ref[0] + bcol_ref[...]
    row, col = _tri_masks(c)
    tri = row >= col
    padded = n_valid < c
    if padded:
        valid_c = lax.broadcasted_iota(jnp.int32, (c, 1), 0) < n_valid
        valid_r = lax.broadcasted_iota(jnp.int32, (1, c), 1) < n_valid
    scale = DK_A ** -0.5
    for h in range(H_A):
        ig_c, lf_c = gc[:, h:h + 1], _log_sigmoid(gc[:, H_A + h:H_A + h + 1])
        ig_r, lf_r = gt[h:h + 1, :], _log_sigmoid(gt[H_A + h:H_A + h + 1, :])
        if padded:
            ig_c, lf_c = jnp.where(valid_c, ig_c, NEG_INF), jnp.where(valid_c, lf_c, 0.0)
            ig_r, lf_r = jnp.where(valid_r, ig_r, NEG_INF), jnp.where(valid_r, lf_r, 0.0)
        q = qk_ref[:, h * DK_A:(h + 1) * DK_A] * scale
        k = qk_ref[:, H_A * DK_A + h * DK_A:H_A * DK_A + (h + 1) * DK_A]
        v = v_ref[:, h * DV_A:(h + 1) * DV_A]
        cst = c_ref[0, h]
        nst = n_ref[0, h:h + 1, :]
        m_prev = m_ref[0, :, h:h + 1]
        b_c, b_r = _cumsum_col_row(lf_c, lf_r, row, col)
        log_d = jnp.where(tri, b_c - b_r + ig_r, NEG_INF)
        m_inter = b_c + m_prev
        m_t = jnp.maximum(m_inter, jnp.max(log_d, axis=1, keepdims=True))
        s = _mm_nt(q, k) * jnp.exp(log_d - m_t)
        inter = jnp.exp(m_inter - m_t)
        num = _mm(s, v) + inter * _mm(q, cst)
        den = jnp.sum(s, axis=1, keepdims=True) + inter * jnp.sum(q * nst, axis=1, keepdims=True)
        hh = num / jnp.maximum(jnp.abs(den), jnp.exp(-m_t))
        m_new = m_t[c - 1:c, :]
        w = jnp.exp(b_c[c - 1:c, :] - b_c + ig_c - m_new)
        decay = inter[c - 1:c, :]
        kw = k * w
        c_ref[0, h] = decay * cst + _mm_tn(kw, v)
        n_ref[0, h:h + 1, :] = decay * nst + jnp.sum(kw, axis=0, keepdims=True)
        m_ref[0, :, h:h + 1] = m_new
        gate = _sigmoid(o_ref[:, h * DV_A:(h + 1) * DV_A])
        out_ref[:, h * DV_A:(h + 1) * DV_A] = _head_norm(hh, ng_ref[:, h * DV_A:(h + 1) * DV_A]) * gate


def mlstm_call(proj, gates_t, brow, bcol, norm_g, state, *, row0, nb, nchunk, c, n_valid):
    has_init = state is not None
    rb = row0 // c
    tok = lambda b, ci: rb + b * nchunk + ci
    st_shapes = [jax.ShapeDtypeStruct((nb, H_A, DK_A, DV_A), F32), jax.ShapeDtypeStruct((nb, H_A, DK_A), F32),
                 jax.ShapeDtypeStruct((nb, 1, H_A), F32)]
    st_specs = [pl.BlockSpec((1, H_A, DK_A, DV_A), lambda b, ci: (b, 0, 0, 0)),
                pl.BlockSpec((1, H_A, DK_A), lambda b, ci: (b, 0, 0)),
                pl.BlockSpec((1, 1, H_A), lambda b, ci: (b, 0, 0))]
    if has_init:
        c0, n0, m0 = state[0], state[1], state[2].reshape(nb, 1, H_A)
    else:
        c0, n0, m0 = (jnp.zeros((1,) + s.shape[1:], F32) for s in st_shapes)
        st_in_specs = [pl.BlockSpec((1, H_A, DK_A, DV_A), lambda b, ci: (0, 0, 0, 0)),
                       pl.BlockSpec((1, H_A, DK_A), lambda b, ci: (0, 0, 0)),
                       pl.BlockSpec((1, 1, H_A), lambda b, ci: (0, 0, 0))]
    out, c1, n1, m1 = pl.pallas_call(
        functools.partial(_mlstm_kernel, c=c, n_valid=n_valid, has_init=has_init),
        grid=(nb, nchunk),
        in_specs=[pl.BlockSpec((c, A_W), lambda b, ci: (tok(b, ci), 0)),
                  pl.BlockSpec((c, A_W), lambda b, ci: (tok(b, ci), 1)),
                  pl.BlockSpec((c, A_W), lambda b, ci: (tok(b, ci), 2)),
                  pl.BlockSpec((c, LANE), lambda b, ci: (tok(b, ci), G_OFF // LANE)),
                  pl.BlockSpec((1, GATE_ROWS, c), lambda b, ci: (b * nchunk + ci, 0, 0)),
                  pl.BlockSpec((1, LANE), lambda b, ci: (0, 0)),
                  pl.BlockSpec((GATE_ROWS, 1), lambda b, ci: (0, 0)),
                  pl.BlockSpec((1, A_W), lambda b, ci: (0, 0))] + (st_specs if has_init else st_in_specs),
        out_specs=[pl.BlockSpec((c, A_W), lambda b, ci: (b * nchunk + ci, 0))] + st_specs,
        out_shape=[jax.ShapeDtypeStruct((nb * nchunk * c, A_W), F32)] + st_shapes,
        compiler_params=_cparams(("parallel", "arbitrary")), name="mlstm",
    )(proj, proj, proj, proj, gates_t, brow, bcol, norm_g.reshape(1, A_W).astype(F32), c0, n0, m0)
    return out, c1, n1, m1.reshape(nb, H_A)


def _bdot(a, b, dims):
    return lax.dot_general(a, b, (dims, ((0,), (0,))), preferred_element_type=F32)


def _bmm(a, b):
    return _bdot(a.astype(BF16), b.astype(BF16), ((2,), (1,)))


def _bmm_nt(a, b):
    return _bdot(a.astype(BF16), b.astype(BF16), ((2,), (2,)))


def _bmm_tn(a, b):
    return _bdot(jnp.swapaxes(a, 1, 2).astype(BF16), b.astype(BF16), ((2,), (1,)))


def _split_bf16(x):
    hi = x.astype(BF16)
    return hi, (x - hi.astype(F32)).astype(BF16)


def _bmm_hi(a, b):
    ah, al = _split_bf16(a)
    bh, bl = _split_bf16(b)
    dims = ((2,), (1,))
    return _bdot(ah, bh, dims) + _bdot(al, bh, dims) + _bdot(ah, bl, dims)


def _unit_lower_inverse(a, row, col, c):
    d = jnp.where(row == col, 1.0, 0.0)
    s = 1
    while s < c:
        blk = jnp.where(((row // (2 * s)) == (col // (2 * s))) & ((row % (2 * s)) >= s) & ((col % (2 * s)) < s),
                        a, 0.0)
        if s == 1:
            d = d - blk
        else:
            d = d - _bmm_hi(_bmm_hi(d, blk), d)
        s *= 2
    return d


def _gdn_kernel(qkv_ref, z_ref, gc_ref, gt_ref, brow_ref, bcol_ref, arow_ref, acol_ref, cw_ref, buf0_ref, ng_ref,
                s0_ref, out_ref, s_ref, xp_ref, *, c, n_valid, has_init):
    ci = pl.program_id(1)
    hk = H_B * DK_B

    @pl.when(ci == 0)
    def _():
        xp_ref[0:8, :] = jnp.zeros((8, xp_ref.shape[1]), F32)
        if has_init:
            s_ref[...] = s0_ref[...]
            xp_ref[8 - (CONV_W - 1):8, :] = buf0_ref[0]
        else:
            s_ref[...] = jnp.zeros_like(s_ref)

    @pl.when(ci > 0)
    def _():
        xp_ref[0:8, :] = xp_ref[c:c + 8, :]

    xp_ref[8:8 + c, :] = qkv_ref[...]

    gc = gc_ref[...] + brow_ref[...]
    gt = gt_ref[0] + bcol_ref[...]
    row, col = _tri_masks(c)
    tri = row >= col
    strict = row > col
    padded = n_valid < c
    if padded:
        valid_c = lax.broadcasted_iota(jnp.int32, (c, 1), 0) < n_valid
        valid_r = lax.broadcasted_iota(jnp.int32, (1, c), 1) < n_valid
    ga, gb = 2 * H_A, 2 * H_A + H_B

    def conv_act(c0):
        heads = []
        for h in range(H_B):
            cs = slice(c0 + h * DK_B, c0 + (h + 1) * DK_B)
            y = cw_ref[0:1, cs] * xp_ref[8 - 3:8 - 3 + c, cs]
            for j in range(1, CONV_W):
                y = y + cw_ref[j:j + 1, cs] * xp_ref[8 - 3 + j:8 - 3 + j + c, cs]
            heads.append(_silu(y))
        return jnp.stack(heads, axis=0)

    heads_c = lambda x, off: jnp.stack([x[:, off + h:off + h + 1] for h in range(H_B)], axis=0)
    heads_r = lambda x, off: jnp.stack([x[off + h:off + h + 1, :] for h in range(H_B)], axis=0)
    g_c = heads_c(-jnp.exp(arow_ref[...]) * _softplus(gc), ga)
    g_r = heads_r(-jnp.exp(acol_ref[...]) * _softplus(gt), ga)
    beta = heads_c(_sigmoid(gc), gb)
    if padded:
        g_c, g_r = jnp.where(valid_c, g_c, 0.0), jnp.where(valid_r, g_r, 0.0)
        beta = jnp.where(valid_c, beta, 0.0)
    q = conv_act(0)
    k = conv_act(hk)
    v = conv_act(2 * hk)
    q = q * lax.rsqrt(jnp.sum(q * q, axis=2, keepdims=True) + EPS) * (DK_B ** -0.5)
    k = k * lax.rsqrt(jnp.sum(k * k, axis=2, keepdims=True) + EPS)
    sst = s_ref[0]
    gcum_c = jnp.sum(jnp.where(tri, g_r, 0.0), axis=2, keepdims=True)
    gcum_r = jnp.sum(jnp.where(row <= col, g_c, 0.0), axis=1, keepdims=True)
    gam = jnp.exp(jnp.where(tri, gcum_c - gcum_r, NEG_INF))
    eg = jnp.exp(gcum_c)
    a = jnp.where(strict, beta * _bmm_nt(k, k) * gam, 0.0)
    tinv = _unit_lower_inverse(a, row, col, c)
    rhs = jnp.concatenate([beta * v, (beta * eg) * k], axis=2)
    sol = _bmm_hi(tinv, rhs)
    u = sol[:, :, :DV_B] - _bmm(sol[:, :, DV_B:], sst)
    o = eg * _bmm(q, sst) + _bmm(_bmm_nt(q, k) * gam, u)
    g_last = gcum_c[:, c - 1:c, :]
    s_ref[0] = jnp.exp(g_last) * sst + _bmm_tn(k * jnp.exp(g_last - gcum_c), u)
    for h in range(H_B):
        hs = slice(h * DV_B, (h + 1) * DV_B)
        out_ref[:, hs] = _head_norm(o[h], ng_ref[:, hs]) * _silu(z_ref[:, hs])


def gdn_call(proj, gates_t, brow, bcol, arow, acol, conv_w, norm_g, state, *, row0, nb, nchunk, c, n_valid):
    has_init = state is not None
    rb = row0 // c
    tok = lambda b, ci: rb + b * nchunk + ci
    wb = 3 * H_B * DK_B
    s_spec = pl.BlockSpec((1, H_B, DK_B, DV_B), lambda b, ci: (b, 0, 0, 0))
    if has_init:
        s0, buf0 = state
        s0_spec, buf_spec = s_spec, pl.BlockSpec((1, CONV_W - 1, wb), lambda b, ci: (b, 0, 0))
    else:
        s0, buf0 = jnp.zeros((1, H_B, DK_B, DV_B), F32), jnp.zeros((1, CONV_W - 1, wb), F32)
        s0_spec = pl.BlockSpec((1, H_B, DK_B, DV_B), lambda b, ci: (0, 0, 0, 0))
        buf_spec = pl.BlockSpec((1, CONV_W - 1, wb), lambda b, ci: (0, 0, 0))
    out, s1 = pl.pallas_call(
        functools.partial(_gdn_kernel, c=c, n_valid=n_valid, has_init=has_init),
        grid=(nb, nchunk),
        in_specs=[pl.BlockSpec((c, wb), lambda b, ci: (tok(b, ci), B_OFF // wb)),
                  pl.BlockSpec((c, A_W), lambda b, ci: (tok(b, ci), (B_OFF + wb) // A_W)),
                  pl.BlockSpec((c, LANE), lambda b, ci: (tok(b, ci), G_OFF // LANE)),
                  pl.BlockSpec((1, GATE_ROWS, c), lambda b, ci: (b * nchunk + ci, 0, 0)),
                  pl.BlockSpec((1, LANE), lambda b, ci: (0, 0)),
                  pl.BlockSpec((GATE_ROWS, 1), lambda b, ci: (0, 0)),
                  pl.BlockSpec((1, LANE), lambda b, ci: (0, 0)),
                  pl.BlockSpec((GATE_ROWS, 1), lambda b, ci: (0, 0)),
                  pl.BlockSpec((CONV_W, wb), lambda b, ci: (0, 0)),
                  buf_spec,
                  pl.BlockSpec((1, H_B * DV_B), lambda b, ci: (0, 0)),
                  s0_spec],
        out_specs=[pl.BlockSpec((c, H_B * DV_B), lambda b, ci: (b * nchunk + ci, 0)), s_spec],
        out_shape=[jax.ShapeDtypeStruct((nb * nchunk * c, H_B * DV_B), F32),
                   jax.ShapeDtypeStruct((nb, H_B, DK_B, DV_B), F32)],
        scratch_shapes=[pltpu.VMEM((c + 8, wb), F32)],
        compiler_params=_cparams(("parallel", "arbitrary")), name="gdn",
    )(proj, proj, proj, gates_t, brow, bcol, arow, acol, conv_w.astype(F32), buf0,
      norm_g.reshape(1, H_B * DV_B).astype(F32), s0)
    return out, s1


def _hgrn_kernel(q_ref, f_ref, i_ref, g_ref, lb_ref, ng_ref, s0_ref, out_ref, s_ref, st_ref, *,
                 c, n_valid, has_init, nchunk):
    ci = pl.program_id(1)

    @pl.when(ci == 0)
    def _():
        for h in range(H_C):
            st_ref[h] = s0_ref[0, h].T if has_init else jnp.zeros((DV_C, DK_C), F32)

    padded = n_valid < c
    rowi = lax.broadcasted_iota(jnp.int32, (c, 1), 0)
    for h in range(H_C):
        hs = slice(h * DK_C, (h + 1) * DK_C)
        cf = f_ref[:, hs]
        lb = lb_ref[:, hs]
        la = jnp.log1p(-lb) + _log_sigmoid(cf)
        lbl = jnp.log(lb)
        logf = jnp.maximum(la, lbl) + jnp.log1p(jnp.exp(-jnp.abs(la - lbl)))
        k = (1.0 - lb) * _sigmoid(-cf)
        if padded:
            logf, k = jnp.where(rowi < n_valid, logf, 0.0), jnp.where(rowi < n_valid, k, 0.0)
        q = _silu(q_ref[:, hs])
        iv = i_ref[:, hs]
        bc = jnp.zeros((c, DK_C), F32)
        for s in range(c):
            bc = bc + jnp.where(rowi >= s, logf[s:s + 1, :], 0.0)
        o = _mm_nt(q * jnp.exp(bc), st_ref[h])
        for s in range(c):
            dec = jnp.exp(jnp.where(rowi >= s, bc - bc[s:s + 1, :], NEG_INF))
            att = jnp.sum(q * dec * k[s:s + 1, :], axis=1, keepdims=True)
            o = o + att * iv[s:s + 1, :]
        b_last = bc[c - 1:c, :]
        st_ref[h] = jnp.exp(b_last) * st_ref[h] + _mm_tn(iv, k * jnp.exp(b_last - bc))
        gate = _silu(g_ref[:, hs])
        out_ref[:, hs] = _head_norm(o, ng_ref[:, hs]) * gate

    @pl.when(ci == nchunk - 1)
    def _():
        for h in range(H_C):
            s_ref[0, h] = st_ref[h].T


def hgrn_call(proj, lb, norm_g, state, *, row0, nb, nchunk, c, n_valid):
    has_init = state is not None
    rb = row0 // c
    tok = lambda b, ci: rb + b * nchunk + ci
    w = H_C * DK_C
    cb = C_OFF // w
    s_spec = pl.BlockSpec((1, H_C, DK_C, DV_C), lambda b, ci: (b, 0, 0, 0))
    if has_init:
        s0, s0_spec = state, s_spec
    else:
        s0 = jnp.zeros((1, H_C, DK_C, DV_C), F32)
        s0_spec = pl.BlockSpec((1, H_C, DK_C, DV_C), lambda b, ci: (0, 0, 0, 0))
    out, s1 = pl.pallas_call(
        functools.partial(_hgrn_kernel, c=c, n_valid=n_valid, has_init=has_init, nchunk=nchunk),
        grid=(nb, nchunk),
        in_specs=[pl.BlockSpec((c, w), lambda b, ci: (tok(b, ci), cb)),
                  pl.BlockSpec((c, w), lambda b, ci: (tok(b, ci), cb + 1)),
                  pl.BlockSpec((c, w), lambda b, ci: (tok(b, ci), cb + 2)),
                  pl.BlockSpec((c, w), lambda b, ci: (tok(b, ci), cb + 3)),
                  pl.BlockSpec((1, w), lambda b, ci: (0, 0)),
                  pl.BlockSpec((1, w), lambda b, ci: (0, 0)),
                  s0_spec],
        out_specs=[pl.BlockSpec((c, w), lambda b, ci: (b * nchunk + ci, 0)), s_spec],
        out_shape=[jax.ShapeDtypeStruct((nb * nchunk * c, w), F32),
                   jax.ShapeDtypeStruct((nb, H_C, DK_C, DV_C), F32)],
        scratch_shapes=[pltpu.VMEM((H_C, DV_C, DK_C), F32)],
        compiler_params=_cparams(("parallel", "arbitrary")), name="hgrn",
    )(proj, proj, proj, proj, lb.reshape(1, w).astype(F32), norm_g.reshape(1, w).astype(F32), s0)
    return out, s1


def _top_values(x, k):
    rows = []
    for _ in range(k):
        mx = jnp.max(x, axis=0, keepdims=True)
        rows.append(mx)
        x = jnp.where(x == mx, NEG_INF, x)
    return jnp.concatenate(rows, axis=0)


def _route_kernel(q_ref, sk_ref, s1_ref, s2_ref, e1_ref, e2_ref, tau_ref):
    for h in range(PEER_HEADS):
        s1 = _mmf_nt(sk_ref[h, 0], q_ref[:, (2 * h) * LANE:(2 * h + 1) * LANE])
        s2 = _mmf_nt(sk_ref[h, 1], q_ref[:, (2 * h + 1) * LANE:(2 * h + 2) * LANE])
        a = _top_values(s1, PEER_TOPK)
        b = _top_values(s2, PEER_TOPK)
        cand = jnp.concatenate([a[i:i + 1, :] + b for i in range(PEER_TOPK)], axis=0)
        best = _top_values(cand, PEER_TOPK)
        z = jnp.sum(jnp.exp(best - best[0:1, :]), axis=0, keepdims=True)
        s1_ref[h] = s1
        s2_ref[h] = s2
        e1_ref[h] = jnp.exp(s1 - a[0:1, :])
        e2_ref[h] = jnp.exp(s2 - b[0:1, :]) / z
        tau_ref[h:h + 1, :] = best[PEER_TOPK - 1:PEER_TOPK, :]


def route_call(qry, subkeys, tb=256):
    t = qry.shape[0]
    big = jax.ShapeDtypeStruct((PEER_HEADS, N_KEYS, t), F32)
    big_spec = pl.BlockSpec((PEER_HEADS, N_KEYS, tb), lambda i: (0, 0, i))
    return pl.pallas_call(
        _route_kernel, grid=(t // tb,),
        in_specs=[pl.BlockSpec((tb, qry.shape[1]), lambda i: (i, 0)),
                  pl.BlockSpec(subkeys.shape, lambda i: (0, 0, 0, 0))],
        out_specs=[big_spec] * 4 + [pl.BlockSpec((PEER_HEADS, tb), lambda i: (0, i))],
        out_shape=[big] * 4 + [jax.ShapeDtypeStruct((PEER_HEADS, t), F32)],
        compiler_params=_cparams(("parallel",)), name="peer_route",
    )(qry, subkeys.astype(F32))


PEER_SUB = 256
PEER_LANES = 256


def _peer_kernel(t_ref, u_ref, v_ref, s1_ref, s2_ref, e1_ref, e2_ref, tau_ref, y_ref, *, eb):
    j = pl.program_id(1)
    tm = t_ref.shape[0]

    @pl.when(j == 0)
    def _():
        y_ref[...] = jnp.zeros_like(y_ref)

    for p in range(eb // PEER_SUB):
        es = slice(p * PEER_SUB, (p + 1) * PEER_SUB)
        a_t = lax.dot_general(u_ref[es, :], t_ref[...], (((1,), (1,)), ((), ())), preferred_element_type=F32)
        rows = []
        for r in range(PEER_SUB // N_KEYS):
            i1 = j * (eb // N_KEYS) + p * (PEER_SUB // N_KEYS) + r
            s1_rows = [s1_ref[h, pl.ds(i1, 1), :] for h in range(PEER_HEADS)]
            e1_rows = [e1_ref[h, pl.ds(i1, 1), :] for h in range(PEER_HEADS)]
            tiles = []
            for tc in range(tm // PEER_LANES):
                ls = slice(tc * PEER_LANES, (tc + 1) * PEER_LANES)
                gsum = None
                for h in range(PEER_HEADS):
                    sc = s1_rows[h][:, ls] + s2_ref[h, :, ls]
                    gh = jnp.where(sc >= tau_ref[h:h + 1, ls], e1_rows[h][:, ls] * e2_ref[h, :, ls], 0.0)
                    gsum = gh if gsum is None else gsum + gh
                tiles.append(jax.nn.gelu(a_t[r * N_KEYS:(r + 1) * N_KEYS, ls]) * gsum)
            rows.append(jnp.concatenate(tiles, axis=1))
        coef = jnp.concatenate(rows, axis=0)
        y_ref[...] += jnp.dot(coef.T.astype(BF16), v_ref[es, :], preferred_element_type=F32)


def peer_call(t_bf, u_bf, v_bf, s1, s2, e1, e2, tau, tm=512, eb=512):
    t, d = t_bf.shape
    ne = u_bf.shape[0]
    once = pl.Buffered(1)
    big_spec = pl.BlockSpec((PEER_HEADS, N_KEYS, tm), lambda i, j: (0, 0, i), pipeline_mode=once)
    return pl.pallas_call(
        functools.partial(_peer_kernel, eb=eb),
        grid=(t // tm, ne // eb),
        in_specs=[pl.BlockSpec((tm, d), lambda i, j: (i, 0), pipeline_mode=once),
                  pl.BlockSpec((eb, d), lambda i, j: (j, 0)),
                  pl.BlockSpec((eb, d), lambda i, j: (j, 0)),
                  big_spec, big_spec, big_spec, big_spec,
                  pl.BlockSpec((PEER_HEADS, tm), lambda i, j: (0, i), pipeline_mode=once)],
        out_specs=pl.BlockSpec((tm, d), lambda i, j: (i, 0)),
        out_shape=jax.ShapeDtypeStruct((t, d), F32),
        compiler_params=_cparams(("parallel", "arbitrary")), name="peer_dense",
    )(t_bf, u_bf, v_bf, s1, s2, e1, e2, tau)


def _permute_w_in(w):
    offs = [0]
    for s in IN_SPLITS:
        offs.append(offs[-1] + s)
    a_main, a_gate = (offs[0], offs[4]), (offs[4], offs[6])
    b_main, b_gate = (offs[6], offs[10]), (offs[10], offs[12])
    c_main = (offs[12], offs[16])
    pad = jnp.zeros((w.shape[0], C_OFF - G_OFF - (a_gate[1] - a_gate[0]) - (b_gate[1] - b_gate[0])), w.dtype)
    parts = [w[:, a_main[0]:a_main[1]], w[:, b_main[0]:b_main[1]], w[:, a_gate[0]:a_gate[1]],
             w[:, b_gate[0]:b_gate[1]], pad, w[:, c_main[0]:c_main[1]]]
    return jnp.concatenate(parts, axis=1).astype(BF16)


def _gate_params(b_i, b_f, a_log, dt_bias):
    zeros = jnp.zeros((LANE - 2 * H_A - 2 * H_B,), F32)
    bias = jnp.concatenate([b_i.astype(F32), b_f.astype(F32), dt_bias.astype(F32), jnp.zeros((H_B,), F32), zeros])
    alog = jnp.concatenate([jnp.zeros((2 * H_A,), F32), a_log.astype(F32), jnp.zeros((H_B,), F32), zeros])
    return (bias.reshape(1, LANE), bias[:GATE_ROWS].reshape(GATE_ROWS, 1),
            alog.reshape(1, LANE), alog[:GATE_ROWS].reshape(GATE_ROWS, 1))


def _gates_rows(proj, c):
    g = proj[:, G_OFF:G_OFF + GATE_ROWS]
    return g.reshape(g.shape[0] // c, c, GATE_ROWS).transpose(0, 2, 1)


def _hgrn_lower_bounds(logits):
    p = jax.nn.softmax(logits.astype(F32), axis=0)
    cs = jnp.cumsum(p, axis=0)
    return cs - cs[0:1]


def kernel(x_prompt, x_sample, state_mlstm_C, state_mlstm_n, state_mlstm_m, state_gdn_S, state_gdn_conv,
           state_hgrn_S, norm_mix_g, w_in, mlstm_b_i, mlstm_b_f, mlstm_norm_g, gdn_conv_w, gdn_A_log,
           gdn_dt_bias, gdn_norm_g, hgrn_lb_logits, hgrn_norm_g, w_out, norm_ffn_g, peer_w_q, peer_subkeys,
           peer_u, peer_v, final_norm_g, *, chunk=64, hgrn_chunk=16, tm=512, tm_peer=512, eb_peer=512, tb=256):
    bp, lp, d = x_prompt.shape
    bs, ls, _ = x_sample.shape
    depth = w_in.shape[0]
    tp = bp * lp
    ts = bs * SAMPLE_PAD
    xs = jnp.pad(x_sample, ((0, 0), (0, SAMPLE_PAD - ls), (0, 0)))
    x = jnp.concatenate([x_prompt.reshape(tp, d), xs.reshape(ts, d)], axis=0).astype(F32)
    lbs = _hgrn_lower_bounds(hgrn_lb_logits)
    wb = 3 * H_B * DK_B
    new = [[] for _ in range(12)]
    y_t = None
    for l in range(depth):
        if l == 0:
            h = rmsnorm_call(x, norm_mix_g[l], BF16, tb=tb)
        else:
            x, h = res_rmsnorm_call(x, y_t, norm_mix_g[l], BF16, tb=tb)
        proj = matmul_call([h], [_permute_w_in(w_in[l])], tm=2 * tm if x.shape[0] % (2 * tm) == 0 else tm,
                           name="in_proj")
        brow, bcol, arow, acol = _gate_params(mlstm_b_i[l], mlstm_b_f[l], gdn_A_log[l], gdn_dt_bias[l])
        gt_p = _gates_rows(proj[:tp], chunk)
        gt_s = _gates_rows(proj[tp:], SAMPLE_PAD)
        pk = dict(row0=0, nb=bp, nchunk=lp // chunk, c=chunk, n_valid=chunk)
        sk = dict(row0=tp, nb=bs, nchunk=1, c=SAMPLE_PAD, n_valid=ls)
        a_p, c_p, n_p, m_p = mlstm_call(proj, gt_p, brow, bcol, mlstm_norm_g[l], None, **pk)
        a_s, c_s, n_s, m_s = mlstm_call(proj, gt_s, brow, bcol, mlstm_norm_g[l],
                                        (state_mlstm_C[l], state_mlstm_n[l], state_mlstm_m[l]), **sk)
        b_p, s_p = gdn_call(proj, gt_p, brow, bcol, arow, acol, gdn_conv_w[l], gdn_norm_g[l], None, **pk)
        b_s, s_s = gdn_call(proj, gt_s, brow, bcol, arow, acol, gdn_conv_w[l], gdn_norm_g[l],
                            (state_gdn_S[l], state_gdn_conv[l]), **sk)
        hk = dict(pk, nchunk=lp // hgrn_chunk, c=hgrn_chunk, n_valid=hgrn_chunk)
        c_pp, h_p = hgrn_call(proj, lbs[l], hgrn_norm_g[l], None, **hk)
        c_ss, h_s = hgrn_call(proj, lbs[l], hgrn_norm_g[l], state_hgrn_S[l], **sk)
        buf_p = proj[:tp, B_OFF:B_OFF + wb].reshape(bp, lp, wb)[:, lp - (CONV_W - 1):]
        buf_s = proj[tp:, B_OFF:B_OFF + wb].reshape(bs, SAMPLE_PAD, wb)[:, ls - (CONV_W - 1):ls]
        for lst, s in zip(new, (c_p, c_s, n_p, n_s, m_p, m_s, s_p, s_s, buf_p, buf_s, h_p, h_s)):
            lst.append(s)
        mix_a = jnp.concatenate([a_p, a_s], axis=0)
        mix_b = jnp.concatenate([b_p, b_s], axis=0)
        mix_c = jnp.concatenate([c_pp, c_ss], axis=0)
        wo = w_out[l].astype(BF16)
        x = matmul_call([mix_a, mix_b, mix_c], [wo[:A_W], wo[A_W:2 * A_W], wo[2 * A_W:]], res=x, tm=tm,
                        name="out_proj")
        t_bf = rmsnorm_call(x, norm_ffn_g[l], BF16, tb=tb)
        qry = matmul_call([t_bf], [peer_w_q[l].astype(BF16)], tm=tm, name="peer_query")
        s1, s2, e1, e2, tau = route_call(qry, peer_subkeys[l], tb=tb)
        y_t = peer_call(t_bf, peer_u[l].astype(BF16), peer_v[l].astype(BF16), s1, s2, e1, e2, tau,
                        tm=tm_peer, eb=eb_peer)
    _, y = res_rmsnorm_call(x, y_t, final_norm_g, F32, tb=tb)
    y_prompt = y[:tp].reshape(bp, lp, d)
    y_sample = y[tp:].reshape(bs, SAMPLE_PAD, d)[:, :ls]
    states = [jnp.stack(s, axis=0) for s in new]
    return (y_prompt, y_sample, *states)
```

```python
import functools

import jax
import jax.numpy as jnp
from jax import lax
from jax.experimental import pallas as pl
from jax.experimental.pallas import tpu as pltpu

F32 = jnp.float32
BF16 = jnp.bfloat16
EPS = 1e-6
NEG_INF = float("-inf")

H_A, DK_A, DV_A = 6, 128, 256
H_B, DK_B, DV_B = 12, 128, 128
H_C, DK_C, DV_C = 8, 128, 128
CONV_W = 4
PEER_HEADS, N_KEYS, PEER_TOPK = 8, 128, 16
SAMPLE_PAD = 8
LANE = 128
VMEM_LIMIT = 56 * 1024 * 1024

A_W = H_A * DV_A
B_OFF = 3 * A_W
G_OFF = B_OFF + 4 * H_B * DV_B
C_OFF = G_OFF + 512
NP = C_OFF + 4 * H_C * DV_C
GATE_ROWS = 40


def _cparams(sem):
    return pltpu.CompilerParams(dimension_semantics=sem, vmem_limit_bytes=VMEM_LIMIT)


def _mm(a, b):
    return jnp.dot(a.astype(BF16), b.astype(BF16), preferred_element_type=F32)


def _mm_nt(a, b):
    return lax.dot_general(a.astype(BF16), b.astype(BF16), (((1,), (1,)), ((), ())), preferred_element_type=F32)


def _mm_tn(a, b):
    return lax.dot_general(a.astype(BF16), b.astype(BF16), (((0,), (0,)), ((), ())), preferred_element_type=F32)


def _mmf(a, b):
    return jnp.dot(a, b, preferred_element_type=F32, precision=lax.Precision.HIGHEST)


def _mmf_nt(a, b):
    return lax.dot_general(a, b, (((1,), (1,)), ((), ())), preferred_element_type=F32,
                           precision=lax.Precision.HIGHEST)


def _softplus(x):
    return jnp.maximum(x, 0.0) + jnp.log1p(jnp.exp(-jnp.abs(x)))


def _log_sigmoid(x):
    return -_softplus(-x)


def _sigmoid(x):
    return 1.0 / (1.0 + jnp.exp(-x))


def _silu(x):
    return x * _sigmoid(x)


def _head_norm(x, g_row):
    return x * lax.rsqrt(jnp.mean(x * x, axis=-1, keepdims=True) + EPS) * g_row


def _rms_kernel(x_ref, g_ref, h_ref):
    x = x_ref[...]
    y = x * lax.rsqrt(jnp.mean(x * x, axis=-1, keepdims=True) + EPS)
    h_ref[...] = (y * g_ref[...]).astype(h_ref.dtype)


def rmsnorm_call(x, g, out_dtype, tb=256):
    t, d = x.shape
    return pl.pallas_call(
        _rms_kernel, grid=(t // tb,),
        in_specs=[pl.BlockSpec((tb, d), lambda i: (i, 0)), pl.BlockSpec((1, d), lambda i: (0, 0))],
        out_specs=pl.BlockSpec((tb, d), lambda i: (i, 0)),
        out_shape=jax.ShapeDtypeStruct((t, d), out_dtype),
        compiler_params=_cparams(("parallel",)), name="rmsnorm",
    )(x, g.reshape(1, d).astype(F32))


def _res_rms_kernel(x_ref, y_ref, g_ref, xn_ref, h_ref):
    x = x_ref[...] + y_ref[...]
    xn_ref[...] = x
    y = x * lax.rsqrt(jnp.mean(x * x, axis=-1, keepdims=True) + EPS)
    h_ref[...] = (y * g_ref[...]).astype(h_ref.dtype)


def res_rmsnorm_call(x, y, g, out_dtype, tb=256):
    t, d = x.shape
    return pl.pallas_call(
        _res_rms_kernel, grid=(t // tb,),
        in_specs=[pl.BlockSpec((tb, d), lambda i: (i, 0)), pl.BlockSpec((tb, d), lambda i: (i, 0)),
                  pl.BlockSpec((1, d), lambda i: (0, 0))],
        out_specs=[pl.BlockSpec((tb, d), lambda i: (i, 0)), pl.BlockSpec((tb, d), lambda i: (i, 0))],
        out_shape=[jax.ShapeDtypeStruct((t, d), F32), jax.ShapeDtypeStruct((t, d), out_dtype)],
        compiler_params=_cparams(("parallel",)), name="res_rmsnorm",
    )(x, y, g.reshape(1, d).astype(F32))


def _matmul_kernel(*refs, n_pairs, has_res):
    a_refs, w_refs = refs[:n_pairs], refs[n_pairs:2 * n_pairs]
    o_ref = refs[-1]
    acc = _mm(a_refs[0][...], w_refs[0][...])
    for a_ref, w_ref in zip(a_refs[1:], w_refs[1:]):
        acc = acc + _mm(a_ref[...], w_ref[...])
    if has_res:
        acc = acc + refs[2 * n_pairs][...]
    o_ref[...] = acc


def matmul_call(a_list, w_list, res=None, tm=512, tn=512, name="matmul"):
    t = a_list[0].shape[0]
    n = w_list[0].shape[1]
    in_specs = [pl.BlockSpec((tm, a.shape[1]), lambda i, j: (i, 0)) for a in a_list]
    in_specs += [pl.BlockSpec((w.shape[0], tn), lambda i, j: (0, j)) for w in w_list]
    args = list(a_list) + list(w_list)
    if res is not None:
        in_specs.append(pl.BlockSpec((tm, tn), lambda i, j: (i, j)))
        args.append(res)
    return pl.pallas_call(
        functools.partial(_matmul_kernel, n_pairs=len(a_list), has_res=res is not None),
        grid=(t // tm, n // tn), in_specs=in_specs,
        out_specs=pl.BlockSpec((tm, tn), lambda i, j: (i, j)),
        out_shape=jax.ShapeDtypeStruct((t, n), F32),
        compiler_params=_cparams(("parallel", "parallel")), name=name,
    )(*args)


IN_TN = 512
A_BLOCKS = B_OFF // IN_TN
G_BLOCK = G_OFF // IN_TN
B_SHIFT = 2 * H_A
C_SHIFT = 2 * H_A + 2 * H_B
W_ROWS = 512


def _in_proj_kernel(a_ref, wm_ref, we_ref, o_ref, wb_ref):
    j, i = pl.program_id(0), pl.program_id(1)
    k = wm_ref.shape[0]

    def convert(shift):
        def body(r, carry):
            rs = pl.ds(pl.multiple_of(r * W_ROWS, W_ROWS), W_ROWS)
            if shift == 0:
                blk = wm_ref[rs, :]
            else:
                blk = jnp.concatenate([wm_ref[rs, shift:], we_ref[rs, :shift]], axis=1)
            wb_ref[rs, :] = blk.astype(BF16)
            return carry
        lax.fori_loop(0, k // W_ROWS, body, 0)

    first = i == 0

    @pl.when(first & (j < A_BLOCKS))
    def _():
        convert(0)

    @pl.when(first & (j >= A_BLOCKS) & (j < G_BLOCK))
    def _():
        convert(B_SHIFT)

    @pl.when(first & (j == G_BLOCK))
    def _():
        wb_ref[:, 0:LANE] = wm_ref[:, 0:LANE].astype(BF16)
        wb_ref[:, LANE:2 * LANE] = we_ref[...].astype(BF16)
        wb_ref[:, 2 * LANE:] = jnp.zeros((k, IN_TN - 2 * LANE), BF16)

    @pl.when(first & (j > G_BLOCK))
    def _():
        convert(C_SHIFT)

    o_ref[...] = jnp.dot(a_ref[...], wb_ref[...], preferred_element_type=F32)


def in_proj_call(a, w_in, layer, tm):
    t, k = a.shape
    lanes_per_tile = IN_TN // LANE
    main_blk = lambda j: jnp.where(j <= G_BLOCK, j, j - 1)
    extra_blk = lambda j: jnp.where(j < G_BLOCK, lanes_per_tile * (j + 1),
                                    jnp.where(j == G_BLOCK, B_OFF // LANE, lanes_per_tile * j))
    return pl.pallas_call(
        _in_proj_kernel, grid=(NP // IN_TN, t // tm),
        in_specs=[pl.BlockSpec((tm, k), lambda j, i: (i, 0)),
                  pl.BlockSpec((None, k, IN_TN), lambda j, i: (layer, 0, main_blk(j))),
                  pl.BlockSpec((None, k, LANE), lambda j, i: (layer, 0, extra_blk(j)))],
        out_specs=pl.BlockSpec((tm, IN_TN), lambda j, i: (i, j)),
        out_shape=jax.ShapeDtypeStruct((t, NP), F32),
        scratch_shapes=[pltpu.VMEM((k, IN_TN), BF16)],
        compiler_params=_cparams(("parallel", "arbitrary")), name="in_proj",
    )(a, w_in, w_in)


def _tri_masks(c):
    row = lax.broadcasted_iota(jnp.int32, (c, c), 0)
    col = lax.broadcasted_iota(jnp.int32, (c, c), 1)
    return row, col


def _cumsum_col_row(x_c, x_r, row, col):
    cs_c = jnp.sum(jnp.where(row >= col, x_r, 0.0), axis=1, keepdims=True)
    cs_r = jnp.sum(jnp.where(row <= col, x_c, 0.0), axis=0, keepdims=True)
    return cs_c, cs_r


def _mlstm_kernel(qk_ref, v_ref, o_ref, gc_ref, gt_ref, brow_ref, bcol_ref, ng_ref, c0_ref, n0_ref, m0_ref,
                  out_ref, c_ref, n_ref, m_ref, *, c, n_valid, has_init):
    ci = pl.program_id(1)

    @pl.when(ci == 0)
    def _():
        if has_init:
            c_ref[...] = c0_ref[...]
            n_ref[...] = n0_ref[...]
            m_ref[...] = m0_ref[...]
        else:
            c_ref[...] = jnp.zeros_like(c_ref)
            n_ref[...] = jnp.zeros_like(n_ref)
            m_ref[...] = jnp.zeros_like(m_ref)

    gc = gc_ref[...] + brow_ref[...]
    gt = gt_ref[0] + bcol_ref[...]
    row, col = _tri_masks(c)
    tri = row >= col
    padded = n_valid < c
    if padded:
        valid_c = lax.broadcasted_iota(jnp.int32, (c, 1), 0) < n_valid
        valid_r = lax.broadcasted_iota(jnp.int32, (1, c), 1) < n_valid
    scale = DK_A ** -0.5
    for h in range(H_A):
        ig_c, lf_c = gc[:, h:h + 1], _log_sigmoid(gc[:, H_A + h:H_A + h + 1])
        ig_r, lf_r = gt[h:h + 1, :], _log_sigmoid(gt[H_A + h:H_A + h + 1, :])
        if padded:
            ig_c, lf_c = jnp.where(valid_c, ig_c, NEG_INF), jnp.where(valid_c, lf_c, 0.0)
            ig_r, lf_r = jnp.where(valid_r, ig_r, NEG_INF), jnp.where(valid_r, lf_r, 0.0)
        q = qk_ref[:, h * DK_A:(h + 1) * DK_A] * scale
        k = qk_ref[:, H_A * DK_A + h * DK_A:H_A * DK_A + (h + 1) * DK_A]
        v = v_ref[:, h * DV_A:(h + 1) * DV_A]
        cst = c_ref[0, h]
        nst = n_ref[0, h:h + 1, :]
        m_prev = m_ref[0, :, h:h + 1]
        b_c, b_r = _cumsum_col_row(lf_c, lf_r, row, col)
        log_d = jnp.where(tri, b_c - b_r + ig_r, NEG_INF)
        m_inter = b_c + m_prev
        m_t = jnp.maximum(m_inter, jnp.max(log_d, axis=1, keepdims=True))
        s = _mm_nt(q, k) * jnp.exp(log_d - m_t)
        inter = jnp.exp(m_inter - m_t)
        num = _mm(s, v) + inter * _mm(q, cst)
        den = jnp.sum(s, axis=1, keepdims=True) + inter * jnp.sum(q * nst, axis=1, keepdims=True)
        hh = num / jnp.maximum(jnp.abs(den), jnp.exp(-m_t))
        m_new = m_t[c - 1:c, :]
        w = jnp.exp(b_c[c - 1:c, :] - b_c + ig_c - m_new)
        decay = inter[c - 1:c, :]
        kw = k * w
        c_ref[0, h] = decay * cst + _mm_tn(kw, v)
        n_ref[0, h:h + 1, :] = decay * nst + jnp.sum(kw, axis=0, keepdims=True)
        m_ref[0, :, h:h + 1] = m_new
        gate = _sigmoid(o_ref[:, h * DV_A:(h + 1) * DV_A])
        out_ref[:, h * DV_A:(h + 1) * DV_A] = _head_norm(hh, ng_ref[:, h * DV_A:(h + 1) * DV_A]) * gate


def mlstm_call(proj, gates_t, brow, bcol, norm_g, state, *, row0, nb, nchunk, c, n_valid):
    has_init = state is not None
    rb = row0 // c
    tok = lambda b, ci: rb + b * nchunk + ci
    st_shapes = [jax.ShapeDtypeStruct((nb, H_A, DK_A, DV_A), F32), jax.ShapeDtypeStruct((nb, H_A, DK_A), F32),
                 jax.ShapeDtypeStruct((nb, 1, H_A), F32)]
    st_specs = [pl.BlockSpec((1, H_A, DK_A, DV_A), lambda b, ci: (b, 0, 0, 0)),
                pl.BlockSpec((1, H_A, DK_A), lambda b, ci: (b, 0, 0)),
                pl.BlockSpec((1, 1, H_A), lambda b, ci: (b, 0, 0))]
    if has_init:
        c0, n0, m0 = state[0], state[1], state[2].reshape(nb, 1, H_A)
    else:
        c0, n0, m0 = (jnp.zeros((1,) + s.shape[1:], F32) for s in st_shapes)
        st_in_specs = [pl.BlockSpec((1, H_A, DK_A, DV_A), lambda b, ci: (0, 0, 0, 0)),
                       pl.BlockSpec((1, H_A, DK_A), lambda b, ci: (0, 0, 0)),
                       pl.BlockSpec((1, 1, H_A), lambda b, ci: (0, 0, 0))]
    out, c1, n1, m1 = pl.pallas_call(
        functools.partial(_mlstm_kernel, c=c, n_valid=n_valid, has_init=has_init),
        grid=(nb, nchunk),
        in_specs=[pl.BlockSpec((c, A_W), lambda b, ci: (tok(b, ci), 0)),
                  pl.BlockSpec((c, A_W), lambda b, ci: (tok(b, ci), 1)),
                  pl.BlockSpec((c, A_W), lambda b, ci: (tok(b, ci), 2)),
                  pl.BlockSpec((c, LANE), lambda b, ci: (tok(b, ci), G_OFF // LANE + 1)),
                  pl.BlockSpec((1, GATE_ROWS, c), lambda b, ci: (b * nchunk + ci, 0, 0)),
                  pl.BlockSpec((1, LANE), lambda b, ci: (0, 0)),
                  pl.BlockSpec((GATE_ROWS, 1), lambda b, ci: (0, 0)),
                  pl.BlockSpec((1, A_W), lambda b, ci: (0, 0))] + (st_specs if has_init else st_in_specs),
        out_specs=[pl.BlockSpec((c, A_W), lambda b, ci: (b * nchunk + ci, 0))] + st_specs,
        out_shape=[jax.ShapeDtypeStruct((nb * nchunk * c, A_W), F32)] + st_shapes,
        compiler_params=_cparams(("parallel", "arbitrary")), name="mlstm",
    )(proj, proj, proj, proj, gates_t, brow, bcol, norm_g.reshape(1, A_W).astype(F32), c0, n0, m0)
    return out, c1, n1, m1.reshape(nb, H_A)


def _bdot(a, b, dims):
    return lax.dot_general(a, b, (dims, ((0,), (0,))), preferred_element_type=F32)


def _bmm(a, b):
    return _bdot(a.astype(BF16), b.astype(BF16), ((2,), (1,)))


def _bmm_nt(a, b):
    return _bdot(a.astype(BF16), b.astype(BF16), ((2,), (2,)))


def _bmm_tn(a, b):
    return _bdot(jnp.swapaxes(a, 1, 2).astype(BF16), b.astype(BF16), ((2,), (1,)))


def _split_bf16(x):
    hi = x.astype(BF16)
    return hi, (x - hi.astype(F32)).astype(BF16)


def _bmm_hi(a, b):
    ah, al = _split_bf16(a)
    bh, bl = _split_bf16(b)
    dims = ((2,), (1,))
    return _bdot(ah, bh, dims) + _bdot(al, bh, dims) + _bdot(ah, bl, dims)


def _unit_lower_inverse(a, row, col, c):
    d = jnp.where(row == col, 1.0, 0.0)
    s = 1
    while s < c:
        blk = jnp.where(((row // (2 * s)) == (col // (2 * s))) & ((row % (2 * s)) >= s) & ((col % (2 * s)) < s),
                        a, 0.0)
        if s == 1:
            d = d - blk
        else:
            d = d - _bmm_hi(_bmm_hi(d, blk), d)
        s *= 2
    return d


def _gdn_kernel(qkv_ref, z_ref, gc_ref, gt_ref, brow_ref, bcol_ref, arow_ref, acol_ref, cw_ref, buf0_ref, ng_ref,
                s0_ref, out_ref, s_ref, xp_ref, *, c, n_valid, has_init):
    ci = pl.program_id(1)
    hk = H_B * DK_B

    @pl.when(ci == 0)
    def _():
        xp_ref[0:8, :] = jnp.zeros((8, xp_ref.shape[1]), F32)
        if has_init:
            s_ref[...] = s0_ref[...]
            xp_ref[8 - (CONV_W - 1):8, :] = buf0_ref[0]
        else:
            s_ref[...] = jnp.zeros_like(s_ref)

    @pl.when(ci > 0)
    def _():
        xp_ref[0:8, :] = xp_ref[c:c + 8, :]

    xp_ref[8:8 + c, :] = qkv_ref[...]

    gc = gc_ref[...] + brow_ref[...]
    gt = gt_ref[0] + bcol_ref[...]
    row, col = _tri_masks(c)
    tri = row >= col
    strict = row > col
    padded = n_valid < c
    if padded:
        valid_c = lax.broadcasted_iota(jnp.int32, (c, 1), 0) < n_valid
        valid_r = lax.broadcasted_iota(jnp.int32, (1, c), 1) < n_valid
    ga, gb = 2 * H_A, 2 * H_A + H_B

    def conv_act(c0):
        heads = []
        for h in range(H_B):
            cs = slice(c0 + h * DK_B, c0 + (h + 1) * DK_B)
            y = cw_ref[0:1, cs] * xp_ref[8 - 3:8 - 3 + c, cs]
            for j in range(1, CONV_W):
                y = y + cw_ref[j:j + 1, cs] * xp_ref[8 - 3 + j:8 - 3 + j + c, cs]
            heads.append(_silu(y))
        return jnp.stack(heads, axis=0)

    heads_c = lambda x, off: jnp.stack([x[:, off + h:off + h + 1] for h in range(H_B)], axis=0)
    heads_r = lambda x, off: jnp.stack([x[off + h:off + h + 1, :] for h in range(H_B)], axis=0)
    g_c = heads_c(-jnp.exp(arow_ref[...]) * _softplus(gc), ga)
    g_r = heads_r(-jnp.exp(acol_ref[...]) * _softplus(gt), ga)
    beta = heads_c(_sigmoid(gc), gb)
    if padded:
        g_c, g_r = jnp.where(valid_c, g_c, 0.0), jnp.where(valid_r, g_r, 0.0)
        beta = jnp.where(valid_c, beta, 0.0)
    q = conv_act(0)
    k = conv_act(hk)
    v = conv_act(2 * hk)
    q = q * lax.rsqrt(jnp.sum(q * q, axis=2, keepdims=True) + EPS) * (DK_B ** -0.5)
    k = k * lax.rsqrt(jnp.sum(k * k, axis=2, keepdims=True) + EPS)
    sst = s_ref[0]
    gcum_c = jnp.sum(jnp.where(tri, g_r, 0.0), axis=2, keepdims=True)
    gcum_r = jnp.sum(jnp.where(row <= col, g_c, 0.0), axis=1, keepdims=True)
    gam = jnp.exp(jnp.where(tri, gcum_c - gcum_r, NEG_INF))
    eg = jnp.exp(gcum_c)
    a = jnp.where(strict, beta * _bmm_nt(k, k) * gam, 0.0)
    tinv = _unit_lower_inverse(a, row, col, c)
    rhs = jnp.concatenate([beta * v, (beta * eg) * k], axis=2)
    sol = _bmm_hi(tinv, rhs)
    u = sol[:, :, :DV_B] - _bmm(sol[:, :, DV_B:], sst)
    o = eg * _bmm(q, sst) + _bmm(_bmm_nt(q, k) * gam, u)
    g_last = gcum_c[:, c - 1:c, :]
    s_ref[0] = jnp.exp(g_last) * sst + _bmm_tn(k * jnp.exp(g_last - gcum_c), u)
    for h in range(H_B):
        hs = slice(h * DV_B, (h + 1) * DV_B)
        out_ref[:, hs] = _head_norm(o[h], ng_ref[:, hs]) * _silu(z_ref[:, hs])


def gdn_call(proj, gates_t, brow, bcol, arow, acol, conv_w, norm_g, state, *, row0, nb, nchunk, c, n_valid):
    has_init = state is not None
    rb = row0 // c
    tok = lambda b, ci: rb + b * nchunk + ci
    wb = 3 * H_B * DK_B
    s_spec = pl.BlockSpec((1, H_B, DK_B, DV_B), lambda b, ci: (b, 0, 0, 0))
    if has_init:
        s0, buf0 = state
        s0_spec, buf_spec = s_spec, pl.BlockSpec((1, CONV_W - 1, wb), lambda b, ci: (b, 0, 0))
    else:
        s0, buf0 = jnp.zeros((1, H_B, DK_B, DV_B), F32), jnp.zeros((1, CONV_W - 1, wb), F32)
        s0_spec = pl.BlockSpec((1, H_B, DK_B, DV_B), lambda b, ci: (0, 0, 0, 0))
        buf_spec = pl.BlockSpec((1, CONV_W - 1, wb), lambda b, ci: (0, 0, 0))
    out, s1 = pl.pallas_call(
        functools.partial(_gdn_kernel, c=c, n_valid=n_valid, has_init=has_init),
        grid=(nb, nchunk),
        in_specs=[pl.BlockSpec((c, wb), lambda b, ci: (tok(b, ci), B_OFF // wb)),
                  pl.BlockSpec((c, A_W), lambda b, ci: (tok(b, ci), (B_OFF + wb) // A_W)),
                  pl.BlockSpec((c, LANE), lambda b, ci: (tok(b, ci), G_OFF // LANE)),
                  pl.BlockSpec((1, GATE_ROWS, c), lambda b, ci: (b * nchunk + ci, 0, 0)),
                  pl.BlockSpec((1, LANE), lambda b, ci: (0, 0)),
                  pl.BlockSpec((GATE_ROWS, 1), lambda b, ci: (0, 0)),
                  pl.BlockSpec((1, LANE), lambda b, ci: (0, 0)),
                  pl.BlockSpec((GATE_ROWS, 1), lambda b, ci: (0, 0)),
                  pl.BlockSpec((CONV_W, wb), lambda b, ci: (0, 0)),
                  buf_spec,
                  pl.BlockSpec((1, H_B * DV_B), lambda b, ci: (0, 0)),
                  s0_spec],
        out_specs=[pl.BlockSpec((c, H_B * DV_B), lambda b, ci: (b * nchunk + ci, 0)), s_spec],
        out_shape=[jax.ShapeDtypeStruct((nb * nchunk * c, H_B * DV_B), F32),
                   jax.ShapeDtypeStruct((nb, H_B, DK_B, DV_B), F32)],
        scratch_shapes=[pltpu.VMEM((c + 8, wb), F32)],
        compiler_params=_cparams(("parallel", "arbitrary")), name="gdn",
    )(proj, proj, proj, gates_t, brow, bcol, arow, acol, conv_w.astype(F32), buf0,
      norm_g.reshape(1, H_B * DV_B).astype(F32), s0)
    return out, s1


def _hgrn_kernel(q_ref, f_ref, i_ref, g_ref, lb_ref, ng_ref, s0_ref, out_ref, s_ref, st_ref, *,
                 c, n_valid, has_init, nchunk):
    ci = pl.program_id(1)

    @pl.when(ci == 0)
    def _():
        for h in range(H_C):
            st_ref[h] = s0_ref[0, h].T if has_init else jnp.zeros((DV_C, DK_C), F32)

    padded = n_valid < c
    rowi = lax.broadcasted_iota(jnp.int32, (c, 1), 0)
    for h in range(H_C):
        hs = slice(h * DK_C, (h + 1) * DK_C)
        cf = f_ref[:, hs]
        lb = lb_ref[:, hs]
        la = jnp.log1p(-lb) + _log_sigmoid(cf)
        lbl = jnp.log(lb)
        logf = jnp.maximum(la, lbl) + jnp.log1p(jnp.exp(-jnp.abs(la - lbl)))
        k = (1.0 - lb) * _sigmoid(-cf)
        if padded:
            logf, k = jnp.where(rowi < n_valid, logf, 0.0), jnp.where(rowi < n_valid, k, 0.0)
        q = _silu(q_ref[:, hs])
        iv = i_ref[:, hs]
        bc = jnp.zeros((c, DK_C), F32)
        for s in range(c):
            bc = bc + jnp.where(rowi >= s, logf[s:s + 1, :], 0.0)
        o = _mm_nt(q * jnp.exp(bc), st_ref[h])
        for s in range(c):
            dec = jnp.exp(jnp.where(rowi >= s, bc - bc[s:s + 1, :], NEG_INF))
            att = jnp.sum(q * dec * k[s:s + 1, :], axis=1, keepdims=True)
            o = o + att * iv[s:s + 1, :]
        b_last = bc[c - 1:c, :]
        st_ref[h] = jnp.exp(b_last) * st_ref[h] + _mm_tn(iv, k * jnp.exp(b_last - bc))
        gate = _silu(g_ref[:, hs])
        out_ref[:, hs] = _head_norm(o, ng_ref[:, hs]) * gate

    @pl.when(ci == nchunk - 1)
    def _():
        for h in range(H_C):
            s_ref[0, h] = st_ref[h].T


def hgrn_call(proj, lb, norm_g, state, *, row0, nb, nchunk, c, n_valid):
    has_init = state is not None
    rb = row0 // c
    tok = lambda b, ci: rb + b * nchunk + ci
    w = H_C * DK_C
    cb = C_OFF // w
    s_spec = pl.BlockSpec((1, H_C, DK_C, DV_C), lambda b, ci: (b, 0, 0, 0))
    if has_init:
        s0, s0_spec = state, s_spec
    else:
        s0 = jnp.zeros((1, H_C, DK_C, DV_C), F32)
        s0_spec = pl.BlockSpec((1, H_C, DK_C, DV_C), lambda b, ci: (0, 0, 0, 0))
    out, s1 = pl.pallas_call(
        functools.partial(_hgrn_kernel, c=c, n_valid=n_valid, has_init=has_init, nchunk=nchunk),
        grid=(nb, nchunk),
        in_specs=[pl.BlockSpec((c, w), lambda b, ci: (tok(b, ci), cb)),
                  pl.BlockSpec((c, w), lambda b, ci: (tok(b, ci), cb + 1)),
                  pl.BlockSpec((c, w), lambda b, ci: (tok(b, ci), cb + 2)),
                  pl.BlockSpec((c, w), lambda b, ci: (tok(b, ci), cb + 3)),
                  pl.BlockSpec((1, w), lambda b, ci: (0, 0)),
                  pl.BlockSpec((1, w), lambda b, ci: (0, 0)),
                  s0_spec],
        out_specs=[pl.BlockSpec((c, w), lambda b, ci: (b * nchunk + ci, 0)), s_spec],
        out_shape=[jax.ShapeDtypeStruct((nb * nchunk * c, w), F32),
                   jax.ShapeDtypeStruct((nb, H_C, DK_C, DV_C), F32)],
        scratch_shapes=[pltpu.VMEM((H_C, DV_C, DK_C), F32)],
        compiler_params=_cparams(("parallel", "arbitrary")), name="hgrn",
    )(proj, proj, proj, proj, lb.reshape(1, w).astype(F32), norm_g.reshape(1, w).astype(F32), s0)
    return out, s1


def _top_values(x, k):
    rows = []
    for _ in range(k):
        mx = jnp.max(x, axis=0, keepdims=True)
        rows.append(mx)
        x = jnp.where(x == mx, NEG_INF, x)
    return jnp.concatenate(rows, axis=0)


def _route_kernel(q_ref, sk_ref, s1_ref, s2_ref, e1_ref, e2_ref, tau_ref):
    for h in range(PEER_HEADS):
        s1 = _mmf_nt(sk_ref[h, 0], q_ref[:, (2 * h) * LANE:(2 * h + 1) * LANE])
        s2 = _mmf_nt(sk_ref[h, 1], q_ref[:, (2 * h + 1) * LANE:(2 * h + 2) * LANE])
        a = _top_values(s1, PEER_TOPK)
        b = _top_values(s2, PEER_TOPK)
        cand = jnp.concatenate([a[i:i + 1, :] + b for i in range(PEER_TOPK)], axis=0)
        best = _top_values(cand, PEER_TOPK)
        z = jnp.sum(jnp.exp(best - best[0:1, :]), axis=0, keepdims=True)
        s1_ref[h] = s1
        s2_ref[h] = s2
        e1_ref[h] = jnp.exp(s1 - a[0:1, :])
        e2_ref[h] = jnp.exp(s2 - b[0:1, :]) / z
        tau_ref[h:h + 1, :] = best[PEER_TOPK - 1:PEER_TOPK, :]


def route_call(qry, subkeys, tb=256):
    t = qry.shape[0]
    big = jax.ShapeDtypeStruct((PEER_HEADS, N_KEYS, t), F32)
    big_spec = pl.BlockSpec((PEER_HEADS, N_KEYS, tb), lambda i: (0, 0, i))
    return pl.pallas_call(
        _route_kernel, grid=(t // tb,),
        in_specs=[pl.BlockSpec((tb, qry.shape[1]), lambda i: (i, 0)),
                  pl.BlockSpec(subkeys.shape, lambda i: (0, 0, 0, 0))],
        out_specs=[big_spec] * 4 + [pl.BlockSpec((PEER_HEADS, tb), lambda i: (0, i))],
        out_shape=[big] * 4 + [jax.ShapeDtypeStruct((PEER_HEADS, t), F32)],
        compiler_params=_cparams(("parallel",)), name="peer_route",
    )(qry, subkeys.astype(F32))


PEER_SUB = 256
PEER_LANES = 256


def _peer_kernel(t_ref, u_ref, v_ref, s1_ref, s2_ref, e1_ref, e2_ref, tau_ref, y_ref, *, eb):
    j = pl.program_id(1)
    tm = t_ref.shape[0]

    @pl.when(j == 0)
    def _():
        y_ref[...] = jnp.zeros_like(y_ref)

    for p in range(eb // PEER_SUB):
        es = slice(p * PEER_SUB, (p + 1) * PEER_SUB)
        a_t = lax.dot_general(u_ref[es, :], t_ref[...], (((1,), (1,)), ((), ())), preferred_element_type=F32)
        rows = []
        for r in range(PEER_SUB // N_KEYS):
            i1 = j * (eb // N_KEYS) + p * (PEER_SUB // N_KEYS) + r
            s1_rows = [s1_ref[h, pl.ds(i1, 1), :] for h in range(PEER_HEADS)]
            e1_rows = [e1_ref[h, pl.ds(i1, 1), :] for h in range(PEER_HEADS)]
            tiles = []
            for tc in range(tm // PEER_LANES):
                ls = slice(tc * PEER_LANES, (tc + 1) * PEER_LANES)
                gsum = None
                for h in range(PEER_HEADS):
                    sc = s1_rows[h][:, ls] + s2_ref[h, :, ls]
                    gh = jnp.where(sc >= tau_ref[h:h + 1, ls], e1_rows[h][:, ls] * e2_ref[h, :, ls], 0.0)
                    gsum = gh if gsum is None else gsum + gh
                tiles.append(jax.nn.gelu(a_t[r * N_KEYS:(r + 1) * N_KEYS, ls]) * gsum)
            rows.append(jnp.concatenate(tiles, axis=1))
        coef = jnp.concatenate(rows, axis=0)
        y_ref[...] += jnp.dot(coef.T.astype(BF16), v_ref[es, :], preferred_element_type=F32)


def peer_call(t_bf, u_bf, v_bf, s1, s2, e1, e2, tau, tm=512, eb=512):
    t, d = t_bf.shape
    ne = u_bf.shape[0]
    once = pl.Buffered(1)
    big_spec = pl.BlockSpec((PEER_HEADS, N_KEYS, tm), lambda i, j: (0, 0, i), pipeline_mode=once)
    return pl.pallas_call(
        functools.partial(_peer_kernel, eb=eb),
        grid=(t // tm, ne // eb),
        in_specs=[pl.BlockSpec((tm, d), lambda i, j: (i, 0), pipeline_mode=once),
                  pl.BlockSpec((eb, d), lambda i, j: (j, 0)),
                  pl.BlockSpec((eb, d), lambda i, j: (j, 0)),
                  big_spec, big_spec, big_spec, big_spec,
                  pl.BlockSpec((PEER_HEADS, tm), lambda i, j: (0, i), pipeline_mode=once)],
        out_specs=pl.BlockSpec((tm, d), lambda i, j: (i, 0)),
        out_shape=jax.ShapeDtypeStruct((t, d), F32),
        compiler_params=_cparams(("parallel", "arbitrary")), name="peer_dense",
    )(t_bf, u_bf, v_bf, s1, s2, e1, e2, tau)


def _gate_params(b_i, b_f, a_log, dt_bias):
    zeros = jnp.zeros((LANE - 2 * H_A - 2 * H_B,), F32)
    bias = jnp.concatenate([b_i.astype(F32), b_f.astype(F32), dt_bias.astype(F32), jnp.zeros((H_B,), F32), zeros])
    alog = jnp.concatenate([jnp.zeros((2 * H_A,), F32), a_log.astype(F32), jnp.zeros((H_B,), F32), zeros])
    return (bias.reshape(1, LANE), bias[:GATE_ROWS].reshape(GATE_ROWS, 1),
            alog.reshape(1, LANE), alog[:GATE_ROWS].reshape(GATE_ROWS, 1))


def _gates_rows(proj, c):
    na, nb = 2 * H_A, 2 * H_B
    g = jnp.concatenate([proj[:, G_OFF + LANE:G_OFF + LANE + na], proj[:, G_OFF + na:G_OFF + na + nb],
                         jnp.zeros((proj.shape[0], GATE_ROWS - na - nb), F32)], axis=1)
    return g.reshape(g.shape[0] // c, c, GATE_ROWS).transpose(0, 2, 1)


def _hgrn_lower_bounds(logits):
    p = jax.nn.softmax(logits.astype(F32), axis=0)
    cs = jnp.cumsum(p, axis=0)
    return cs - cs[0:1]


def kernel(x_prompt, x_sample, state_mlstm_C, state_mlstm_n, state_mlstm_m, state_gdn_S, state_gdn_conv,
           state_hgrn_S, norm_mix_g, w_in, mlstm_b_i, mlstm_b_f, mlstm_norm_g, gdn_conv_w, gdn_A_log,
           gdn_dt_bias, gdn_norm_g, hgrn_lb_logits, hgrn_norm_g, w_out, norm_ffn_g, peer_w_q, peer_subkeys,
           peer_u, peer_v, final_norm_g, *, chunk=64, hgrn_chunk=16, tm=512, tm_peer=512, eb_peer=512, tb=256):
    bp, lp, d = x_prompt.shape
    bs, ls, _ = x_sample.shape
    depth = w_in.shape[0]
    tp = bp * lp
    ts = bs * SAMPLE_PAD
    xs = jnp.pad(x_sample, ((0, 0), (0, SAMPLE_PAD - ls), (0, 0)))
    x = jnp.concatenate([x_prompt.reshape(tp, d), xs.reshape(ts, d)], axis=0).astype(F32)
    lbs = _hgrn_lower_bounds(hgrn_lb_logits)
    wb = 3 * H_B * DK_B
    new = [[] for _ in range(12)]
    y_t = None
    for l in range(depth):
        if l == 0:
            h = rmsnorm_call(x, norm_mix_g[l], BF16, tb=tb)
        else:
            x, h = res_rmsnorm_call(x, y_t, norm_mix_g[l], BF16, tb=tb)
        proj = in_proj_call(h, w_in, l, tm=2 * tm if x.shape[0] % (2 * tm) == 0 else tm)
        brow, bcol, arow, acol = _gate_params(mlstm_b_i[l], mlstm_b_f[l], gdn_A_log[l], gdn_dt_bias[l])
        gt_p = _gates_rows(proj[:tp], chunk)
        gt_s = _gates_rows(proj[tp:], SAMPLE_PAD)
        pk = dict(row0=0, nb=bp, nchunk=lp // chunk, c=chunk, n_valid=chunk)
        sk = dict(row0=tp, nb=bs, nchunk=1, c=SAMPLE_PAD, n_valid=ls)
        a_p, c_p, n_p, m_p = mlstm_call(proj, gt_p, brow, bcol, mlstm_norm_g[l], None, **pk)
        a_s, c_s, n_s, m_s = mlstm_call(proj, gt_s, brow, bcol, mlstm_norm_g[l],
                                        (state_mlstm_C[l], state_mlstm_n[l], state_mlstm_m[l]), **sk)
        b_p, s_p = gdn_call(proj, gt_p, brow, bcol, arow, acol, gdn_conv_w[l], gdn_norm_g[l], None, **pk)
        b_s, s_s = gdn_call(proj, gt_s, brow, bcol, arow, acol, gdn_conv_w[l], gdn_norm_g[l],
                            (state_gdn_S[l], state_gdn_conv[l]), **sk)
        hk = dict(pk, nchunk=lp // hgrn_chunk, c=hgrn_chunk, n_valid=hgrn_chunk)
        c_pp, h_p = hgrn_call(proj, lbs[l], hgrn_norm_g[l], None, **hk)
        c_ss, h_s = hgrn_call(proj, lbs[l], hgrn_norm_g[l], state_hgrn_S[l], **sk)
        buf_p = proj[:tp, B_OFF:B_OFF + wb].reshape(bp, lp, wb)[:, lp - (CONV_W - 1):]
        buf_s = proj[tp:, B_OFF:B_OFF + wb].reshape(bs, SAMPLE_PAD, wb)[:, ls - (CONV_W - 1):ls]
        for lst, s in zip(new, (c_p, c_s, n_p, n_s, m_p, m_s, s_p, s_s, buf_p, buf_s, h_p, h_s)):
            lst.append(s)
        mix_a = jnp.concatenate([a_p, a_s], axis=0)
        mix_b = jnp.concatenate([b_p, b_s], axis=0)
        mix_c = jnp.concatenate([c_pp, c_ss], axis=0)
        wo = w_out[l].astype(BF16)
        x = matmul_call([mix_a, mix_b, mix_c], [wo[:A_W], wo[A_W:2 * A_W], wo[2 * A_W:]], res=x, tm=tm,
                        name="out_proj")
        t_bf = rmsnorm_call(x, norm_ffn_g[l], BF16, tb=tb)
        qry = matmul_call([t_bf], [peer_w_q[l].astype(BF16)], tm=tm, name="peer_query")
        s1, s2, e1, e2, tau = route_call(qry, peer_subkeys[l], tb=tb)
        y_t = peer_call(t_bf, peer_u[l].astype(BF16), peer_v[l].astype(BF16), s1, s2, e1, e2, tau,
                        tm=tm_peer, eb=eb_peer)
    _, y = res_rmsnorm_call(x, y_t, final_norm_g, F32, tb=tb)
    y_prompt = y[:tp].reshape(bp, lp, d)
    y_sample = y[tp:].reshape(bs, SAMPLE_PAD, d)[:, :ls]
    states = [jnp.stack(s, axis=0) for s in new]
    return (y_prompt, y_sample, *states)
```

```python
import functools

import jax
import jax.numpy as jnp
from jax import lax
from jax.experimental import pallas as pl
from jax.experimental.pallas import tpu as pltpu

F32 = jnp.float32
BF16 = jnp.bfloat16
EPS = 1e-6
NEG_INF = float("-inf")

H_A, DK_A, DV_A = 6, 128, 256
H_B, DK_B, DV_B = 12, 128, 128
H_C, DK_C, DV_C = 8, 128, 128
CONV_W = 4
PEER_HEADS, N_KEYS, PEER_TOPK = 8, 128, 16
SAMPLE_PAD = 8
LANE = 128
VMEM_LIMIT = 56 * 1024 * 1024

A_W = H_A * DV_A
B_OFF = 3 * A_W
G_OFF = B_OFF + 4 * H_B * DV_B
C_OFF = G_OFF + 512
NP = C_OFF + 4 * H_C * DV_C
GATE_ROWS = 40


def _cparams(sem):
    return pltpu.CompilerParams(dimension_semantics=sem, vmem_limit_bytes=VMEM_LIMIT)


def _mm(a, b):
    return jnp.dot(a.astype(BF16), b.astype(BF16), preferred_element_type=F32)


def _mmf_nt(a, b):
    return lax.dot_general(a, b, (((1,), (1,)), ((), ())), preferred_element_type=F32,
                           precision=lax.Precision.HIGHEST)


def _softplus(x):
    return jnp.maximum(x, 0.0) + jnp.log1p(jnp.exp(-jnp.abs(x)))


def _log_sigmoid(x):
    return -_softplus(-x)


def _sigmoid(x):
    return 1.0 / (1.0 + jnp.exp(-x))


def _silu(x):
    return x * _sigmoid(x)


def _head_norm(x, g_row):
    return x * lax.rsqrt(jnp.mean(x * x, axis=-1, keepdims=True) + EPS) * g_row


def _rms_kernel(x_ref, g_ref, h_ref):
    x = x_ref[...]
    y = x * lax.rsqrt(jnp.mean(x * x, axis=-1, keepdims=True) + EPS)
    h_ref[...] = (y * g_ref[...]).astype(h_ref.dtype)


def rmsnorm_call(x, g, out_dtype, tb=256):
    t, d = x.shape
    return pl.pallas_call(
        _rms_kernel, grid=(t // tb,),
        in_specs=[pl.BlockSpec((tb, d), lambda i: (i, 0)), pl.BlockSpec((1, d), lambda i: (0, 0))],
        out_specs=pl.BlockSpec((tb, d), lambda i: (i, 0)),
        out_shape=jax.ShapeDtypeStruct((t, d), out_dtype),
        compiler_params=_cparams(("parallel",)), name="rmsnorm",
    )(x, g.reshape(1, d).astype(F32))


def _res_rms_kernel(x_ref, y_ref, g_ref, xn_ref, h_ref):
    x = x_ref[...] + y_ref[...]
    xn_ref[...] = x
    y = x * lax.rsqrt(jnp.mean(x * x, axis=-1, keepdims=True) + EPS)
    h_ref[...] = (y * g_ref[...]).astype(h_ref.dtype)


def res_rmsnorm_call(x, y, g, out_dtype, tb=256):
    t, d = x.shape
    return pl.pallas_call(
        _res_rms_kernel, grid=(t // tb,),
        in_specs=[pl.BlockSpec((tb, d), lambda i: (i, 0)), pl.BlockSpec((tb, d), lambda i: (i, 0)),
                  pl.BlockSpec((1, d), lambda i: (0, 0))],
        out_specs=[pl.BlockSpec((tb, d), lambda i: (i, 0)), pl.BlockSpec((tb, d), lambda i: (i, 0))],
        out_shape=[jax.ShapeDtypeStruct((t, d), F32), jax.ShapeDtypeStruct((t, d), out_dtype)],
        compiler_params=_cparams(("parallel",)), name="res_rmsnorm",
    )(x, y, g.reshape(1, d).astype(F32))


def _matmul_kernel(*refs, n_pairs, has_res):
    a_refs, w_refs = refs[:n_pairs], refs[n_pairs:2 * n_pairs]
    o_ref = refs[-1]
    acc = _mm(a_refs[0][...], w_refs[0][...])
    for a_ref, w_ref in zip(a_refs[1:], w_refs[1:]):
        acc = acc + _mm(a_ref[...], w_ref[...])
    if has_res:
        acc = acc + refs[2 * n_pairs][...]
    o_ref[...] = acc


def matmul_call(a_list, w, layer, res=None, tm=512, tn=512, name="matmul"):
    t = a_list[0].shape[0]
    n = w.shape[2]
    in_specs = [pl.BlockSpec((tm, a.shape[1]), lambda i, j: (i, 0)) for a in a_list]
    row = 0
    for a in a_list:
        k = a.shape[1]
        in_specs.append(pl.BlockSpec((None, k, tn), functools.partial(lambda i, j, rb: (layer, rb, j), rb=row // k)))
        row += k
    args = list(a_list) + [w] * len(a_list)
    if res is not None:
        in_specs.append(pl.BlockSpec((tm, tn), lambda i, j: (i, j)))
        args.append(res)
    return pl.pallas_call(
        functools.partial(_matmul_kernel, n_pairs=len(a_list), has_res=res is not None),
        grid=(t // tm, n // tn), in_specs=in_specs,
        out_specs=pl.BlockSpec((tm, tn), lambda i, j: (i, j)),
        out_shape=jax.ShapeDtypeStruct((t, n), F32),
        compiler_params=_cparams(("parallel", "parallel")), name=name,
    )(*args)


IN_TN = 512
A_BLOCKS = B_OFF // IN_TN
G_BLOCK = G_OFF // IN_TN
B_SHIFT = 2 * H_A
C_SHIFT = 2 * H_A + 2 * H_B
W_ROWS = 512


def _in_proj_kernel(a_ref, wm_ref, we_ref, o_ref, wb_ref):
    j, i = pl.program_id(0), pl.program_id(1)
    k = wm_ref.shape[0]

    def convert(shift):
        def body(r, carry):
            rs = pl.ds(pl.multiple_of(r * W_ROWS, W_ROWS), W_ROWS)
            if shift == 0:
                blk = wm_ref[rs, :]
            else:
                blk = jnp.concatenate([wm_ref[rs, shift:], we_ref[rs, :shift]], axis=1)
            wb_ref[rs, :] = blk.astype(BF16)
            return carry
        lax.fori_loop(0, k // W_ROWS, body, 0)

    first = i == 0

    @pl.when(first & (j < A_BLOCKS))
    def _():
        convert(0)

    @pl.when(first & (j >= A_BLOCKS) & (j < G_BLOCK))
    def _():
        convert(B_SHIFT)

    @pl.when(first & (j == G_BLOCK))
    def _():
        wb_ref[:, 0:LANE] = wm_ref[:, 0:LANE].astype(BF16)
        wb_ref[:, LANE:2 * LANE] = we_ref[...].astype(BF16)
        wb_ref[:, 2 * LANE:] = jnp.zeros((k, IN_TN - 2 * LANE), BF16)

    @pl.when(first & (j > G_BLOCK))
    def _():
        convert(C_SHIFT)

    o_ref[...] = jnp.dot(a_ref[...], wb_ref[...], preferred_element_type=F32)


def in_proj_call(a, w_in, layer, tm):
    t, k = a.shape
    lanes_per_tile = IN_TN // LANE
    main_blk = lambda j: jnp.where(j <= G_BLOCK, j, j - 1)
    extra_blk = lambda j: jnp.where(j < G_BLOCK, lanes_per_tile * (j + 1),
                                    jnp.where(j == G_BLOCK, B_OFF // LANE, lanes_per_tile * j))
    return pl.pallas_call(
        _in_proj_kernel, grid=(NP // IN_TN, t // tm),
        in_specs=[pl.BlockSpec((tm, k), lambda j, i: (i, 0)),
                  pl.BlockSpec((None, k, IN_TN), lambda j, i: (layer, 0, main_blk(j))),
                  pl.BlockSpec((None, k, LANE), lambda j, i: (layer, 0, extra_blk(j)))],
        out_specs=pl.BlockSpec((tm, IN_TN), lambda j, i: (i, j)),
        out_shape=jax.ShapeDtypeStruct((t, NP), F32),
        scratch_shapes=[pltpu.VMEM((k, IN_TN), BF16)],
        compiler_params=_cparams(("parallel", "arbitrary")), name="in_proj",
    )(a, w_in, w_in)


def _tri_masks(c):
    row = lax.broadcasted_iota(jnp.int32, (c, c), 0)
    col = lax.broadcasted_iota(jnp.int32, (c, c), 1)
    return row, col


def _mlstm_kernel(qk_ref, v_ref, o_ref, gc_ref, gt_ref, brow_ref, bcol_ref, ng_ref, c0_ref, n0_ref, m0_ref,
                  out_ref, c_ref, n_ref, m_ref, *, c, n_valid, has_init):
    ci = pl.program_id(1)

    @pl.when(ci == 0)
    def _():
        if has_init:
            c_ref[...] = c0_ref[...]
            n_ref[...] = n0_ref[...]
            m_ref[...] = m0_ref[...]
        else:
            c_ref[...] = jnp.zeros_like(c_ref)
            n_ref[...] = jnp.zeros_like(n_ref)
            m_ref[...] = jnp.zeros_like(m_ref)

    gc = gc_ref[...] + brow_ref[...]
    gt = gt_ref[0] + bcol_ref[...]
    row, col = _tri_masks(c)
    tri = row >= col
    padded = n_valid < c
    if padded:
        valid_c = lax.broadcasted_iota(jnp.int32, (c, 1), 0) < n_valid
        valid_r = lax.broadcasted_iota(jnp.int32, (1, c), 1) < n_valid
    hr = range(H_A)
    heads_c = lambda x, off: jnp.stack([x[:, off + h:off + h + 1] for h in hr], axis=0)
    heads_r = lambda x, off: jnp.stack([x[off + h:off + h + 1, :] for h in hr], axis=0)
    ig_c, lf_c = heads_c(gc, 0), heads_c(_log_sigmoid(gc), H_A)
    ig_r, lf_r = heads_r(gt, 0), heads_r(_log_sigmoid(gt), H_A)
    if padded:
        ig_c, lf_c = jnp.where(valid_c, ig_c, NEG_INF), jnp.where(valid_c, lf_c, 0.0)
        ig_r, lf_r = jnp.where(valid_r, ig_r, NEG_INF), jnp.where(valid_r, lf_r, 0.0)
    q = jnp.stack([qk_ref[:, h * DK_A:(h + 1) * DK_A] for h in hr], axis=0) * (DK_A ** -0.5)
    k = jnp.stack([qk_ref[:, (H_A + h) * DK_A:(H_A + h + 1) * DK_A] for h in hr], axis=0)
    v = jnp.stack([v_ref[:, h * DV_A:(h + 1) * DV_A] for h in hr], axis=0)
    cst = c_ref[0]
    nst = jnp.stack([n_ref[0, h:h + 1, :] for h in hr], axis=0)
    m_prev = jnp.stack([m_ref[0, :, h:h + 1] for h in hr], axis=0)
    b_c = jnp.sum(jnp.where(tri, lf_r, 0.0), axis=2, keepdims=True)
    b_r = jnp.sum(jnp.where(row <= col, lf_c, 0.0), axis=1, keepdims=True)
    log_d = jnp.where(tri, b_c - b_r + ig_r, NEG_INF)
    m_inter = b_c + m_prev
    m_t = jnp.maximum(m_inter, jnp.max(log_d, axis=2, keepdims=True))
    s = _bmm_nt(q, k) * jnp.exp(log_d - m_t)
    inter = jnp.exp(m_inter - m_t)
    num = _bmm(s, v) + inter * _bmm(q, cst)
    den = jnp.sum(s, axis=2, keepdims=True) + inter * jnp.sum(q * nst, axis=2, keepdims=True)
    hh = num / jnp.maximum(jnp.abs(den), jnp.exp(-m_t))
    m_new = m_t[:, c - 1:c, :]
    w = jnp.exp(b_c[:, c - 1:c, :] - b_c + ig_c - m_new)
    decay = inter[:, c - 1:c, :]
    kw = k * w
    c_ref[0] = decay * cst + _bmm_tn(kw, v)
    n_new = decay * nst + jnp.sum(kw, axis=1, keepdims=True)
    for h in hr:
        hs = slice(h * DV_A, (h + 1) * DV_A)
        n_ref[0, h:h + 1, :] = n_new[h]
        m_ref[0, :, h:h + 1] = m_new[h]
        out_ref[:, hs] = _head_norm(hh[h], ng_ref[:, hs]) * _sigmoid(o_ref[:, hs])


def _mixer_call(body, name, proj, seqs, layer, depth, tok_blocks, extra_inputs, init_states, st_shapes,
                mix_w, prev_mix, prev_states, scratch_shapes=()):
    row0, nb, nchunk, c = seqs["row0"], seqs["nb"], seqs["nchunk"], seqs["c"]
    rb = row0 // c
    tok = lambda b, ci: rb + b * nchunk + ci
    st_spec = lambda shape, lyr: pl.BlockSpec(
        (None, 1) + shape, functools.partial(lambda b, ci, n: (lyr, b) + (0,) * n, n=len(shape)))
    in_specs = [pl.BlockSpec((c, w), functools.partial(lambda b, ci, cb: (tok(b, ci), cb), cb=cb))
                for w, cb in tok_blocks]
    args = [proj] * len(tok_blocks)
    for arr, spec in extra_inputs:
        in_specs.append(spec)
        args.append(arr)
    if init_states is None:
        for shape in st_shapes:
            in_specs.append(pl.BlockSpec((None, 1) + shape, functools.partial(
                lambda b, ci, n: (0,) * (n + 2), n=len(shape))))
            args.append(jnp.zeros((1, 1) + shape, F32))
    else:
        for shape, arr in zip(st_shapes, init_states):
            in_specs.append(st_spec(shape, layer))
            args.append(arr)
    n_in = len(args)
    out_specs = [pl.BlockSpec((c, mix_w), lambda b, ci: (tok(b, ci), 0))]
    out_shape = [jax.ShapeDtypeStruct((proj.shape[0], mix_w), F32)]
    for shape in st_shapes:
        out_specs.append(st_spec(shape, layer))
        out_shape.append(jax.ShapeDtypeStruct((depth, nb) + shape, F32))
    prev = [prev_mix] + (list(prev_states) if prev_states is not None else [None] * len(st_shapes))
    aliases = {}
    for k, arr in enumerate(prev):
        if arr is not None:
            aliases[len(args)] = k
            in_specs.append(pl.BlockSpec(memory_space=pl.ANY))
            args.append(arr)
    n_alias = len(args) - n_in

    def kernel_fn(*refs):
        body(*refs[:n_in], *refs[n_in + n_alias:])

    return pl.pallas_call(
        kernel_fn, grid=(nb, nchunk), in_specs=in_specs, out_specs=out_specs, out_shape=out_shape,
        input_output_aliases=aliases, scratch_shapes=list(scratch_shapes),
        compiler_params=_cparams(("parallel", "arbitrary")), name=name)(*args)


def _const_spec(shape):
    return pl.BlockSpec(shape, functools.partial(lambda b, ci, n: (0,) * n, n=len(shape)))


def _gates_spec(seqs):
    nchunk, c = seqs["nchunk"], seqs["c"]
    return pl.BlockSpec((1, GATE_ROWS, c), lambda b, ci: (b * nchunk + ci, 0, 0))


def mlstm_call(proj, gates_t, brow, bcol, norm_g, seqs, layer, depth, init_states, prev_mix, prev_states):
    body = functools.partial(_mlstm_kernel, c=seqs["c"], n_valid=seqs["n_valid"],
                             has_init=init_states is not None)
    extra = [(gates_t, _gates_spec(seqs)), (brow, _const_spec((1, LANE))), (bcol, _const_spec((GATE_ROWS, 1))),
             (norm_g.reshape(1, A_W).astype(F32), _const_spec((1, A_W)))]
    return _mixer_call(body, "mlstm", proj, seqs, layer, depth,
                       [(A_W, 0), (A_W, 1), (A_W, 2), (LANE, G_OFF // LANE + 1)], extra, init_states,
                       [(H_A, DK_A, DV_A), (H_A, DK_A), (1, H_A)], A_W, prev_mix, prev_states)


def _bdot(a, b, dims):
    return lax.dot_general(a, b, (dims, ((0,), (0,))), preferred_element_type=F32)


def _bmm(a, b):
    return _bdot(a.astype(BF16), b.astype(BF16), ((2,), (1,)))


def _bmm_nt(a, b):
    return _bdot(a.astype(BF16), b.astype(BF16), ((2,), (2,)))


def _bmm_tn(a, b):
    return _bdot(jnp.swapaxes(a, 1, 2).astype(BF16), b.astype(BF16), ((2,), (1,)))


def _split_bf16(x):
    hi = x.astype(BF16)
    return hi, (x - hi.astype(F32)).astype(BF16)


def _bmm_hi(a, b):
    ah, al = _split_bf16(a)
    bh, bl = _split_bf16(b)
    dims = ((2,), (1,))
    return _bdot(ah, bh, dims) + _bdot(al, bh, dims) + _bdot(ah, bl, dims)


def _unit_lower_inverse(a, row, col, c):
    d = jnp.where(row == col, 1.0, 0.0)
    s = 1
    while s < c:
        blk = jnp.where(((row // (2 * s)) == (col // (2 * s))) & ((row % (2 * s)) >= s) & ((col % (2 * s)) < s),
                        a, 0.0)
        if s == 1:
            d = d - blk
        else:
            d = d - _bmm_hi(_bmm_hi(d, blk), d)
        s *= 2
    return d


def _gdn_kernel(qkv_ref, z_ref, gc_ref, gt_ref, brow_ref, bcol_ref, arow_ref, acol_ref, cw_ref, buf0_ref, ng_ref,
                s0_ref, out_ref, s_ref, xp_ref, *, c, n_valid, has_init):
    ci = pl.program_id(1)
    hk = H_B * DK_B

    @pl.when(ci == 0)
    def _():
        xp_ref[0:8, :] = jnp.zeros((8, xp_ref.shape[1]), F32)
        if has_init:
            s_ref[...] = s0_ref[...]
            xp_ref[8 - (CONV_W - 1):8, :] = buf0_ref[0]
        else:
            s_ref[...] = jnp.zeros_like(s_ref)

    @pl.when(ci > 0)
    def _():
        xp_ref[0:8, :] = xp_ref[c:c + 8, :]

    xp_ref[8:8 + c, :] = qkv_ref[...]

    gc = gc_ref[...] + brow_ref[...]
    gt = gt_ref[0] + bcol_ref[...]
    row, col = _tri_masks(c)
    tri = row >= col
    strict = row > col
    padded = n_valid < c
    if padded:
        valid_c = lax.broadcasted_iota(jnp.int32, (c, 1), 0) < n_valid
        valid_r = lax.broadcasted_iota(jnp.int32, (1, c), 1) < n_valid
    ga, gb = 2 * H_A, 2 * H_A + H_B

    def conv_act(c0):
        heads = []
        for h in range(H_B):
            cs = slice(c0 + h * DK_B, c0 + (h + 1) * DK_B)
            y = cw_ref[0:1, cs] * xp_ref[8 - 3:8 - 3 + c, cs]
            for j in range(1, CONV_W):
                y = y + cw_ref[j:j + 1, cs] * xp_ref[8 - 3 + j:8 - 3 + j + c, cs]
            heads.append(_silu(y))
        return jnp.stack(heads, axis=0)

    heads_c = lambda x, off: jnp.stack([x[:, off + h:off + h + 1] for h in range(H_B)], axis=0)
    heads_r = lambda x, off: jnp.stack([x[off + h:off + h + 1, :] for h in range(H_B)], axis=0)
    g_c = heads_c(-jnp.exp(arow_ref[...]) * _softplus(gc), ga)
    g_r = heads_r(-jnp.exp(acol_ref[...]) * _softplus(gt), ga)
    beta = heads_c(_sigmoid(gc), gb)
    if padded:
        g_c, g_r = jnp.where(valid_c, g_c, 0.0), jnp.where(valid_r, g_r, 0.0)
        beta = jnp.where(valid_c, beta, 0.0)
    q = conv_act(0)
    k = conv_act(hk)
    v = conv_act(2 * hk)
    q = q * lax.rsqrt(jnp.sum(q * q, axis=2, keepdims=True) + EPS) * (DK_B ** -0.5)
    k = k * lax.rsqrt(jnp.sum(k * k, axis=2, keepdims=True) + EPS)
    sst = s_ref[0]
    gcum_c = jnp.sum(jnp.where(tri, g_r, 0.0), axis=2, keepdims=True)
    gcum_r = jnp.sum(jnp.where(row <= col, g_c, 0.0), axis=1, keepdims=True)
    gam = jnp.exp(jnp.where(tri, gcum_c - gcum_r, NEG_INF))
    eg = jnp.exp(gcum_c)
    a = jnp.where(strict, beta * _bmm_nt(k, k) * gam, 0.0)
    tinv = _unit_lower_inverse(a, row, col, c)
    rhs = jnp.concatenate([beta * v, (beta * eg) * k], axis=2)
    sol = _bmm_hi(tinv, rhs)
    u = sol[:, :, :DV_B] - _bmm(sol[:, :, DV_B:], sst)
    o = eg * _bmm(q, sst) + _bmm(_bmm_nt(q, k) * gam, u)
    g_last = gcum_c[:, c - 1:c, :]
    s_ref[0] = jnp.exp(g_last) * sst + _bmm_tn(k * jnp.exp(g_last - gcum_c), u)
    for h in range(H_B):
        hs = slice(h * DV_B, (h + 1) * DV_B)
        out_ref[:, hs] = _head_norm(o[h], ng_ref[:, hs]) * _silu(z_ref[:, hs])


def gdn_call(proj, gates_t, brow, bcol, arow, acol, conv_w, norm_g, seqs, layer, depth, init_states, buf0,
             prev_mix, prev_states):
    wb = 3 * H_B * DK_B
    has_init = init_states is not None
    body = functools.partial(_gdn_kernel, c=seqs["c"], n_valid=seqs["n_valid"], has_init=has_init)
    if has_init:
        buf_spec = pl.BlockSpec((None, 1, CONV_W - 1, wb), lambda b, ci: (layer, b, 0, 0))
    else:
        buf0 = jnp.zeros((1, 1, CONV_W - 1, wb), F32)
        buf_spec = pl.BlockSpec((None, 1, CONV_W - 1, wb), lambda b, ci: (0, 0, 0, 0))
    extra = [(gates_t, _gates_spec(seqs)), (brow, _const_spec((1, LANE))), (bcol, _const_spec((GATE_ROWS, 1))),
             (arow, _const_spec((1, LANE))), (acol, _const_spec((GATE_ROWS, 1))),
             (conv_w.astype(F32), _const_spec((CONV_W, wb))), (buf0, buf_spec),
             (norm_g.reshape(1, H_B * DV_B).astype(F32), _const_spec((1, H_B * DV_B)))]
    return _mixer_call(body, "gdn", proj, seqs, layer, depth,
                       [(wb, B_OFF // wb), (A_W, (B_OFF + wb) // A_W), (LANE, G_OFF // LANE)], extra, init_states,
                       [(H_B, DK_B, DV_B)], H_B * DV_B, prev_mix, prev_states,
                       scratch_shapes=[pltpu.VMEM((seqs["c"] + 8, wb), F32)])


def _hgrn_kernel(q_ref, f_ref, i_ref, g_ref, lb_ref, ng_ref, s0_ref, out_ref, s_ref, st_ref, *,
                 c, n_valid, has_init, nchunk):
    ci = pl.program_id(1)

    @pl.when(ci == 0)
    def _():
        for h in range(H_C):
            st_ref[h] = s0_ref[0, h].T if has_init else jnp.zeros((DV_C, DK_C), F32)

    padded = n_valid < c
    rowi = lax.broadcasted_iota(jnp.int32, (1, c, 1), 1)
    heads = lambda x: jnp.stack([x[:, h * DK_C:(h + 1) * DK_C] for h in range(H_C)], axis=0)
    cf = heads(f_ref[...])
    lb = heads(lb_ref[...])
    la = jnp.log1p(-lb) + _log_sigmoid(cf)
    lbl = jnp.log(lb)
    logf = jnp.maximum(la, lbl) + jnp.log1p(jnp.exp(-jnp.abs(la - lbl)))
    k = (1.0 - lb) * _sigmoid(-cf)
    if padded:
        logf, k = jnp.where(rowi < n_valid, logf, 0.0), jnp.where(rowi < n_valid, k, 0.0)
    q = _silu(heads(q_ref[...]))
    iv = heads(i_ref[...])
    bc = jnp.zeros((H_C, c, DK_C), F32)
    for s in range(c):
        bc = bc + jnp.where(rowi >= s, logf[:, s:s + 1, :], 0.0)
    sst = st_ref[...]
    o = _bmm_nt(q * jnp.exp(bc), sst)
    for s in range(c):
        dec = jnp.exp(jnp.where(rowi >= s, bc - bc[:, s:s + 1, :], NEG_INF))
        att = jnp.sum(q * dec * k[:, s:s + 1, :], axis=2, keepdims=True)
        o = o + att * iv[:, s:s + 1, :]
    b_last = bc[:, c - 1:c, :]
    st_ref[...] = jnp.exp(b_last) * sst + _bmm_tn(iv, k * jnp.exp(b_last - bc))
    for h in range(H_C):
        hs = slice(h * DK_C, (h + 1) * DK_C)
        out_ref[:, hs] = _head_norm(o[h], ng_ref[:, hs]) * _silu(g_ref[:, hs])

    @pl.when(ci == nchunk - 1)
    def _():
        for h in range(H_C):
            s_ref[0, h] = st_ref[h].T


def hgrn_call(proj, lb, norm_g, seqs, layer, depth, init_states, prev_mix, prev_states):
    w = H_C * DK_C
    cb = C_OFF // w
    body = functools.partial(_hgrn_kernel, c=seqs["c"], n_valid=seqs["n_valid"],
                             has_init=init_states is not None, nchunk=seqs["nchunk"])
    extra = [(lb.reshape(1, w).astype(F32), _const_spec((1, w))),
             (norm_g.reshape(1, w).astype(F32), _const_spec((1, w)))]
    return _mixer_call(body, "hgrn", proj, seqs, layer, depth,
                       [(w, cb), (w, cb + 1), (w, cb + 2), (w, cb + 3)], extra, init_states,
                       [(H_C, DK_C, DV_C)], w, prev_mix, prev_states,
                       scratch_shapes=[pltpu.VMEM((H_C, DV_C, DK_C), F32)])


def _top_values(x, k):
    rows = []
    for _ in range(k):
        mx = jnp.max(x, axis=0, keepdims=True)
        rows.append(mx)
        x = jnp.where(x == mx, NEG_INF, x)
    return jnp.concatenate(rows, axis=0)


def _route_kernel(q_ref, sk_ref, s1_ref, s2_ref, e1_ref, e2_ref, tau_ref):
    for h in range(PEER_HEADS):
        s1 = _mmf_nt(sk_ref[h, 0], q_ref[:, (2 * h) * LANE:(2 * h + 1) * LANE])
        s2 = _mmf_nt(sk_ref[h, 1], q_ref[:, (2 * h + 1) * LANE:(2 * h + 2) * LANE])
        a = _top_values(s1, PEER_TOPK)
        b = _top_values(s2, PEER_TOPK)
        cand = jnp.concatenate([a[i:i + 1, :] + b for i in range(PEER_TOPK)], axis=0)
        best = _top_values(cand, PEER_TOPK)
        z = jnp.sum(jnp.exp(best - best[0:1, :]), axis=0, keepdims=True)
        s1_ref[h] = s1
        s2_ref[h] = s2
        e1_ref[h] = jnp.exp(s1 - a[0:1, :])
        e2_ref[h] = jnp.exp(s2 - b[0:1, :]) / z
        tau_ref[h:h + 1, :] = best[PEER_TOPK - 1:PEER_TOPK, :]


def route_call(qry, subkeys, tb=256):
    t = qry.shape[0]
    big = jax.ShapeDtypeStruct((PEER_HEADS, N_KEYS, t), F32)
    big_spec = pl.BlockSpec((PEER_HEADS, N_KEYS, tb), lambda i: (0, 0, i))
    return pl.pallas_call(
        _route_kernel, grid=(t // tb,),
        in_specs=[pl.BlockSpec((tb, qry.shape[1]), lambda i: (i, 0)),
                  pl.BlockSpec(subkeys.shape, lambda i: (0, 0, 0, 0))],
        out_specs=[big_spec] * 4 + [pl.BlockSpec((PEER_HEADS, tb), lambda i: (0, i))],
        out_shape=[big] * 4 + [jax.ShapeDtypeStruct((PEER_HEADS, t), F32)],
        compiler_params=_cparams(("parallel",)), name="peer_route",
    )(qry, subkeys.astype(F32))


PEER_SUB = 256
PEER_LANES = 256


def _peer_kernel(t_ref, u_ref, v_ref, s1_ref, s2_ref, e1_ref, e2_ref, tau_ref, y_ref, *, eb):
    j = pl.program_id(1)
    tm = t_ref.shape[0]

    @pl.when(j == 0)
    def _():
        y_ref[...] = jnp.zeros_like(y_ref)

    for p in range(eb // PEER_SUB):
        es = slice(p * PEER_SUB, (p + 1) * PEER_SUB)
        a_t = lax.dot_general(u_ref[es, :], t_ref[...], (((1,), (1,)), ((), ())), preferred_element_type=F32)
        rows = []
        for r in range(PEER_SUB // N_KEYS):
            i1 = j * (eb // N_KEYS) + p * (PEER_SUB // N_KEYS) + r
            s1_rows = [s1_ref[h, pl.ds(i1, 1), :] for h in range(PEER_HEADS)]
            e1_rows = [e1_ref[h, pl.ds(i1, 1), :] for h in range(PEER_HEADS)]
            tiles = []
            for tc in range(tm // PEER_LANES):
                ls = slice(tc * PEER_LANES, (tc + 1) * PEER_LANES)
                gsum = None
                for h in range(PEER_HEADS):
                    sc = s1_rows[h][:, ls] + s2_ref[h, :, ls]
                    gh = jnp.where(sc >= tau_ref[h:h + 1, ls], e1_rows[h][:, ls] * e2_ref[h, :, ls], 0.0)
                    gsum = gh if gsum is None else gsum + gh
                tiles.append(jax.nn.gelu(a_t[r * N_KEYS:(r + 1) * N_KEYS, ls]) * gsum)
            rows.append(jnp.concatenate(tiles, axis=1))
        coef = jnp.concatenate(rows, axis=0)
        y_ref[...] += jnp.dot(coef.T.astype(BF16), v_ref[es, :], preferred_element_type=F32)


def peer_call(t_bf, u_bf, v_bf, layer, s1, s2, e1, e2, tau, tm=512, eb=512):
    t, d = t_bf.shape
    ne = u_bf.shape[1]
    once = pl.Buffered(1)
    big_spec = pl.BlockSpec((PEER_HEADS, N_KEYS, tm), lambda i, j: (0, 0, i), pipeline_mode=once)
    return pl.pallas_call(
        functools.partial(_peer_kernel, eb=eb),
        grid=(t // tm, ne // eb),
        in_specs=[pl.BlockSpec((tm, d), lambda i, j: (i, 0), pipeline_mode=once),
                  pl.BlockSpec((None, eb, d), lambda i, j: (layer, j, 0)),
                  pl.BlockSpec((None, eb, d), lambda i, j: (layer, j, 0)),
                  big_spec, big_spec, big_spec, big_spec,
                  pl.BlockSpec((PEER_HEADS, tm), lambda i, j: (0, i), pipeline_mode=once)],
        out_specs=pl.BlockSpec((tm, d), lambda i, j: (i, 0)),
        out_shape=jax.ShapeDtypeStruct((t, d), F32),
        compiler_params=_cparams(("parallel", "arbitrary")), name="peer_dense",
    )(t_bf, u_bf, v_bf, s1, s2, e1, e2, tau)


def _gate_params(b_i, b_f, a_log, dt_bias):
    zeros = jnp.zeros((LANE - 2 * H_A - 2 * H_B,), F32)
    bias = jnp.concatenate([b_i.astype(F32), b_f.astype(F32), dt_bias.astype(F32), jnp.zeros((H_B,), F32), zeros])
    alog = jnp.concatenate([jnp.zeros((2 * H_A,), F32), a_log.astype(F32), jnp.zeros((H_B,), F32), zeros])
    return (bias.reshape(1, LANE), bias[:GATE_ROWS].reshape(GATE_ROWS, 1),
            alog.reshape(1, LANE), alog[:GATE_ROWS].reshape(GATE_ROWS, 1))


def _gates_rows(proj, c):
    na, nb = 2 * H_A, 2 * H_B
    g = jnp.concatenate([proj[:, G_OFF + LANE:G_OFF + LANE + na], proj[:, G_OFF + na:G_OFF + na + nb],
                         jnp.zeros((proj.shape[0], GATE_ROWS - na - nb), F32)], axis=1)
    return g.reshape(g.shape[0] // c, c, GATE_ROWS).transpose(0, 2, 1)


def _hgrn_lower_bounds(logits):
    p = jax.nn.softmax(logits.astype(F32), axis=0)
    cs = jnp.cumsum(p, axis=0)
    return cs - cs[0:1]


def kernel(x_prompt, x_sample, state_mlstm_C, state_mlstm_n, state_mlstm_m, state_gdn_S, state_gdn_conv,
           state_hgrn_S, norm_mix_g, w_in, mlstm_b_i, mlstm_b_f, mlstm_norm_g, gdn_conv_w, gdn_A_log,
           gdn_dt_bias, gdn_norm_g, hgrn_lb_logits, hgrn_norm_g, w_out, norm_ffn_g, peer_w_q, peer_subkeys,
           peer_u, peer_v, final_norm_g, *, chunk=64, hgrn_chunk=16, tm=512, tm_peer=512, eb_peer=512, tb=256):
    bp, lp, d = x_prompt.shape
    bs, ls, _ = x_sample.shape
    depth = w_in.shape[0]
    tp = bp * lp
    ts = bs * SAMPLE_PAD
    xs = jnp.pad(x_sample, ((0, 0), (0, SAMPLE_PAD - ls), (0, 0)))
    x = jnp.concatenate([x_prompt.reshape(tp, d), xs.reshape(ts, d)], axis=0).astype(F32)
    lbs = _hgrn_lower_bounds(hgrn_lb_logits)
    wb = 3 * H_B * DK_B
    u_all, v_all = peer_u.astype(BF16), peer_v.astype(BF16)
    wo_all, wq_all = w_out.astype(BF16), peer_w_q.astype(BF16)
    pseq = dict(row0=0, nb=bp, nchunk=lp // chunk, c=chunk, n_valid=chunk)
    hseq = dict(pseq, nchunk=lp // hgrn_chunk, c=hgrn_chunk, n_valid=hgrn_chunk)
    sseq = dict(row0=tp, nb=bs, nchunk=1, c=SAMPLE_PAD, n_valid=ls)
    a_init = [state_mlstm_C, state_mlstm_n, state_mlstm_m.reshape(depth, bs, 1, H_A)]
    st_ap = st_as = st_bp = st_bs = st_cp = st_cs = None
    bufs_p, bufs_s = [], []
    y_peer = None
    for l in range(depth):
        if l == 0:
            h = rmsnorm_call(x, norm_mix_g[l], BF16, tb=tb)
        else:
            x, h = res_rmsnorm_call(x, y_peer, norm_mix_g[l], BF16, tb=tb)
        proj = in_proj_call(h, w_in, l, tm=2 * tm if x.shape[0] % (2 * tm) == 0 else tm)
        brow, bcol, arow, acol = _gate_params(mlstm_b_i[l], mlstm_b_f[l], gdn_A_log[l], gdn_dt_bias[l])
        gt_p = _gates_rows(proj[:tp], chunk)
        gt_s = _gates_rows(proj[tp:], SAMPLE_PAD)
        mix_a, *st_ap = mlstm_call(proj, gt_p, brow, bcol, mlstm_norm_g[l], pseq, l, depth, None, None, st_ap)
        mix_a, *st_as = mlstm_call(proj, gt_s, brow, bcol, mlstm_norm_g[l], sseq, l, depth, a_init, mix_a, st_as)
        gdn_w = (brow, bcol, arow, acol, gdn_conv_w[l], gdn_norm_g[l])
        mix_b, *st_bp = gdn_call(proj, gt_p, *gdn_w, pseq, l, depth, None, None, None, st_bp)
        mix_b, *st_bs = gdn_call(proj, gt_s, *gdn_w, sseq, l, depth, [state_gdn_S], state_gdn_conv, mix_b, st_bs)
        mix_c, *st_cp = hgrn_call(proj, lbs[l], hgrn_norm_g[l], hseq, l, depth, None, None, st_cp)
        mix_c, *st_cs = hgrn_call(proj, lbs[l], hgrn_norm_g[l], sseq, l, depth, [state_hgrn_S], mix_c, st_cs)
        bufs_p.append(proj[:tp, B_OFF:B_OFF + wb].reshape(bp, lp, wb)[:, lp - (CONV_W - 1):])
        bufs_s.append(proj[tp:, B_OFF:B_OFF + wb].reshape(bs, SAMPLE_PAD, wb)[:, ls - (CONV_W - 1):ls])
        x = matmul_call([mix_a, mix_b, mix_c], wo_all, l, res=x, tm=tm, name="out_proj")
        t_bf = rmsnorm_call(x, norm_ffn_g[l], BF16, tb=tb)
        qry = matmul_call([t_bf], wq_all, l, tm=tm, name="peer_query")
        s1, s2, e1, e2, tau = route_call(qry, peer_subkeys[l], tb=tb)
        y_peer = peer_call(t_bf, u_all, v_all, l, s1, s2, e1, e2, tau, tm=tm_peer, eb=eb_peer)
    _, y = res_rmsnorm_call(x, y_peer, final_norm_g, F32, tb=tb)
    y_prompt = y[:tp].reshape(bp, lp, d)
    y_sample = y[tp:].reshape(bs, SAMPLE_PAD, d)[:, :ls]
    m_p, m_s = st_ap[2].reshape(depth, bp, H_A), st_as[2].reshape(depth, bs, H_A)
    return (y_prompt, y_sample, st_ap[0], st_as[0], st_ap[1], st_as[1], m_p, m_s, st_bp[0], st_bs[0],
            jnp.stack(bufs_p, axis=0), jnp.stack(bufs_s, axis=0), st_cp[0], st_cs[0])
```

```python
import functools

import jax
import jax.numpy as jnp
from jax import lax
from jax.experimental import pallas as pl
from jax.experimental.pallas import tpu as pltpu

F32 = jnp.float32
BF16 = jnp.bfloat16
EPS = 1e-6
NEG_INF = float("-inf")

H_A, DK_A, DV_A = 6, 128, 256
H_B, DK_B, DV_B = 12, 128, 128
H_C, DK_C, DV_C = 8, 128, 128
CONV_W = 4
PEER_HEADS, N_KEYS, PEER_TOPK = 8, 128, 16
SAMPLE_PAD = 8
LANE = 128
VMEM_LIMIT = 56 * 1024 * 1024

A_W = H_A * DV_A
B_OFF = 3 * A_W
G_OFF = B_OFF + 4 * H_B * DV_B
C_OFF = G_OFF + 512
NP = C_OFF + 4 * H_C * DV_C
GATE_ROWS = 40


def _cparams(sem):
    return pltpu.CompilerParams(dimension_semantics=sem, vmem_limit_bytes=VMEM_LIMIT)


def _mm(a, b):
    return jnp.dot(a.astype(BF16), b.astype(BF16), preferred_element_type=F32)


def _mmf_nt(a, b):
    return lax.dot_general(a, b, (((1,), (1,)), ((), ())), preferred_element_type=F32,
                           precision=lax.Precision.HIGHEST)


def _softplus(x):
    return jnp.maximum(x, 0.0) + jnp.log1p(jnp.exp(-jnp.abs(x)))


def _log_sigmoid(x):
    return -_softplus(-x)


def _sigmoid(x):
    return 1.0 / (1.0 + jnp.exp(-x))


def _silu(x):
    return x * _sigmoid(x)


def _head_norm(x, g_row):
    return x * lax.rsqrt(jnp.mean(x * x, axis=-1, keepdims=True) + EPS) * g_row


def _rms_kernel(x_ref, g_ref, h_ref):
    x = x_ref[...]
    y = x * lax.rsqrt(jnp.mean(x * x, axis=-1, keepdims=True) + EPS)
    h_ref[...] = (y * g_ref[...]).astype(h_ref.dtype)


def rmsnorm_call(x, g, out_dtype, tb=256):
    t, d = x.shape
    return pl.pallas_call(
        _rms_kernel, grid=(t // tb,),
        in_specs=[pl.BlockSpec((tb, d), lambda i: (i, 0)), pl.BlockSpec((1, d), lambda i: (0, 0))],
        out_specs=pl.BlockSpec((tb, d), lambda i: (i, 0)),
        out_shape=jax.ShapeDtypeStruct((t, d), out_dtype),
        compiler_params=_cparams(("parallel",)), name="rmsnorm",
    )(x, g.reshape(1, d).astype(F32))


def _res_rms_kernel(x_ref, y_ref, g_ref, xn_ref, h_ref):
    x = x_ref[...] + y_ref[...]
    xn_ref[...] = x
    y = x * lax.rsqrt(jnp.mean(x * x, axis=-1, keepdims=True) + EPS)
    h_ref[...] = (y * g_ref[...]).astype(h_ref.dtype)


def res_rmsnorm_call(x, y, g, out_dtype, tb=256):
    t, d = x.shape
    return pl.pallas_call(
        _res_rms_kernel, grid=(t // tb,),
        in_specs=[pl.BlockSpec((tb, d), lambda i: (i, 0)), pl.BlockSpec((tb, d), lambda i: (i, 0)),
                  pl.BlockSpec((1, d), lambda i: (0, 0))],
        out_specs=[pl.BlockSpec((tb, d), lambda i: (i, 0)), pl.BlockSpec((tb, d), lambda i: (i, 0))],
        out_shape=[jax.ShapeDtypeStruct((t, d), F32), jax.ShapeDtypeStruct((t, d), out_dtype)],
        compiler_params=_cparams(("parallel",)), name="res_rmsnorm",
    )(x, y, g.reshape(1, d).astype(F32))


def _matmul_kernel(*refs, n_pairs, has_res):
    a_refs, w_refs = refs[:n_pairs], refs[n_pairs:2 * n_pairs]
    o_ref = refs[-1]
    acc = _mm(a_refs[0][...], w_refs[0][...])
    for a_ref, w_ref in zip(a_refs[1:], w_refs[1:]):
        acc = acc + _mm(a_ref[...], w_ref[...])
    if has_res:
        acc = acc + refs[2 * n_pairs][...]
    o_ref[...] = acc


def matmul_call(a_list, w, layer, res=None, tm=512, tn=512, name="matmul"):
    t = a_list[0].shape[0]
    n = w.shape[2]
    in_specs = [pl.BlockSpec((tm, a.shape[1]), lambda i, j: (i, 0)) for a in a_list]
    row = 0
    for a in a_list:
        k = a.shape[1]
        in_specs.append(pl.BlockSpec((None, k, tn), functools.partial(lambda i, j, rb: (layer, rb, j), rb=row // k)))
        row += k
    args = list(a_list) + [w] * len(a_list)
    if res is not None:
        in_specs.append(pl.BlockSpec((tm, tn), lambda i, j: (i, j)))
        args.append(res)
    return pl.pallas_call(
        functools.partial(_matmul_kernel, n_pairs=len(a_list), has_res=res is not None),
        grid=(t // tm, n // tn), in_specs=in_specs,
        out_specs=pl.BlockSpec((tm, tn), lambda i, j: (i, j)),
        out_shape=jax.ShapeDtypeStruct((t, n), F32),
        compiler_params=_cparams(("parallel", "parallel")), name=name,
    )(*args)


IN_TN = 512
A_BLOCKS = B_OFF // IN_TN
G_BLOCK = G_OFF // IN_TN
B_SHIFT = 2 * H_A
C_SHIFT = 2 * H_A + 2 * H_B
W_ROWS = 512


def _in_proj_kernel(a_ref, wm_ref, we_ref, o_ref, wb_ref):
    j, i = pl.program_id(0), pl.program_id(1)
    k = wm_ref.shape[0]

    def convert(shift):
        def body(r, carry):
            rs = pl.ds(pl.multiple_of(r * W_ROWS, W_ROWS), W_ROWS)
            if shift == 0:
                blk = wm_ref[rs, :]
            else:
                blk = jnp.concatenate([wm_ref[rs, shift:], we_ref[rs, :shift]], axis=1)
            wb_ref[rs, :] = blk.astype(BF16)
            return carry
        lax.fori_loop(0, k // W_ROWS, body, 0)

    first = i == 0

    @pl.when(first & (j < A_BLOCKS))
    def _():
        convert(0)

    @pl.when(first & (j >= A_BLOCKS) & (j < G_BLOCK))
    def _():
        convert(B_SHIFT)

    @pl.when(first & (j == G_BLOCK))
    def _():
        wb_ref[:, 0:LANE] = wm_ref[:, 0:LANE].astype(BF16)
        wb_ref[:, LANE:2 * LANE] = we_ref[...].astype(BF16)
        wb_ref[:, 2 * LANE:] = jnp.zeros((k, IN_TN - 2 * LANE), BF16)

    @pl.when(first & (j > G_BLOCK))
    def _():
        convert(C_SHIFT)

    o_ref[...] = jnp.dot(a_ref[...], wb_ref[...], preferred_element_type=F32)


def in_proj_call(a, w_in, layer, tm):
    t, k = a.shape
    lanes_per_tile = IN_TN // LANE
    main_blk = lambda j: jnp.where(j <= G_BLOCK, j, j - 1)
    extra_blk = lambda j: jnp.where(j < G_BLOCK, lanes_per_tile * (j + 1),
                                    jnp.where(j == G_BLOCK, B_OFF // LANE, lanes_per_tile * j))
    return pl.pallas_call(
        _in_proj_kernel, grid=(NP // IN_TN, t // tm),
        in_specs=[pl.BlockSpec((tm, k), lambda j, i: (i, 0)),
                  pl.BlockSpec((None, k, IN_TN), lambda j, i: (layer, 0, main_blk(j))),
                  pl.BlockSpec((None, k, LANE), lambda j, i: (layer, 0, extra_blk(j)))],
        out_specs=pl.BlockSpec((tm, IN_TN), lambda j, i: (i, j)),
        out_shape=jax.ShapeDtypeStruct((t, NP), F32),
        scratch_shapes=[pltpu.VMEM((k, IN_TN), BF16)],
        compiler_params=_cparams(("parallel", "arbitrary")), name="in_proj",
    )(a, w_in, w_in)


def _tri_masks(c):
    row = lax.broadcasted_iota(jnp.int32, (c, c), 0)
    col = lax.broadcasted_iota(jnp.int32, (c, c), 1)
    return row, col


def _mlstm_kernel(qk_ref, v_ref, o_ref, gc_ref, gt_ref, brow_ref, bcol_ref, ng_ref, c0_ref, n0_ref, m0_ref,
                  out_ref, c_ref, n_ref, m_ref, *, c, n_valid, has_init):
    ci = pl.program_id(1)

    @pl.when(ci == 0)
    def _():
        if has_init:
            c_ref[...] = c0_ref[...]
            n_ref[...] = n0_ref[...]
            m_ref[...] = m0_ref[...]
        else:
            c_ref[...] = jnp.zeros_like(c_ref)
            n_ref[...] = jnp.zeros_like(n_ref)
            m_ref[...] = jnp.zeros_like(m_ref)

    gc = gc_ref[...] + brow_ref[...]
    gt = gt_ref[0] + bcol_ref[...]
    row, col = _tri_masks(c)
    tri = row >= col
    padded = n_valid < c
    if padded:
        valid_c = lax.broadcasted_iota(jnp.int32, (c, 1), 0) < n_valid
        valid_r = lax.broadcasted_iota(jnp.int32, (1, c), 1) < n_valid
    hr = range(H_A)
    heads_c = lambda x, off: jnp.stack([x[:, off + h:off + h + 1] for h in hr], axis=0)
    heads_r = lambda x, off: jnp.stack([x[off + h:off + h + 1, :] for h in hr], axis=0)
    ig_c, lf_c = heads_c(gc, 0), heads_c(_log_sigmoid(gc), H_A)
    ig_r, lf_r = heads_r(gt, 0), heads_r(_log_sigmoid(gt), H_A)
    if padded:
        ig_c, lf_c = jnp.where(valid_c, ig_c, NEG_INF), jnp.where(valid_c, lf_c, 0.0)
        ig_r, lf_r = jnp.where(valid_r, ig_r, NEG_INF), jnp.where(valid_r, lf_r, 0.0)
    q = jnp.stack([qk_ref[:, h * DK_A:(h + 1) * DK_A] for h in hr], axis=0) * (DK_A ** -0.5)
    k = jnp.stack([qk_ref[:, (H_A + h) * DK_A:(H_A + h + 1) * DK_A] for h in hr], axis=0)
    v = jnp.stack([v_ref[:, h * DV_A:(h + 1) * DV_A] for h in hr], axis=0)
    cst = c_ref[0]
    nst = jnp.stack([n_ref[0, h:h + 1, :] for h in hr], axis=0)
    m_prev = jnp.stack([m_ref[0, :, h:h + 1] for h in hr], axis=0)
    b_c = jnp.sum(jnp.where(tri, lf_r, 0.0), axis=2, keepdims=True)
    b_r = jnp.sum(jnp.where(row <= col, lf_c, 0.0), axis=1, keepdims=True)
    log_d = jnp.where(tri, b_c - b_r + ig_r, NEG_INF)
    m_inter = b_c + m_prev
    m_t = jnp.maximum(m_inter, jnp.max(log_d, axis=2, keepdims=True))
    s = _bmm_nt(q, k) * jnp.exp(log_d - m_t)
    inter = jnp.exp(m_inter - m_t)
    num = _bmm(s, v) + inter * _bmm(q, cst)
    den = jnp.sum(s, axis=2, keepdims=True) + inter * jnp.sum(q * nst, axis=2, keepdims=True)
    hh = num / jnp.maximum(jnp.abs(den), jnp.exp(-m_t))
    m_new = m_t[:, c - 1:c, :]
    w = jnp.exp(b_c[:, c - 1:c, :] - b_c + ig_c - m_new)
    decay = inter[:, c - 1:c, :]
    kw = k * w
    c_ref[0] = decay * cst + _bmm_tn(kw, v)
    n_new = decay * nst + jnp.sum(kw, axis=1, keepdims=True)
    for h in hr:
        hs = slice(h * DV_A, (h + 1) * DV_A)
        n_ref[0, h:h + 1, :] = n_new[h]
        m_ref[0, :, h:h + 1] = m_new[h]
        out_ref[:, hs] = _head_norm(hh[h], ng_ref[:, hs]) * _sigmoid(o_ref[:, hs])


def _mixer_call(body, name, proj, seqs, layer, depth, tok_blocks, extra_inputs, init_states, st_shapes,
                mix_w, prev_mix, prev_states, scratch_shapes=()):
    row0, nb, nchunk, c = seqs["row0"], seqs["nb"], seqs["nchunk"], seqs["c"]
    rb = row0 // c
    tok = lambda b, ci: rb + b * nchunk + ci
    st_spec = lambda shape, lyr: pl.BlockSpec(
        (None, 1) + shape, functools.partial(lambda b, ci, n: (lyr, b) + (0,) * n, n=len(shape)))
    in_specs = [pl.BlockSpec((c, w), functools.partial(lambda b, ci, cb: (tok(b, ci), cb), cb=cb))
                for w, cb in tok_blocks]
    args = [proj] * len(tok_blocks)
    for arr, spec in extra_inputs:
        in_specs.append(spec)
        args.append(arr)
    if init_states is None:
        for shape in st_shapes:
            in_specs.append(pl.BlockSpec((None, 1) + shape, functools.partial(
                lambda b, ci, n: (0,) * (n + 2), n=len(shape))))
            args.append(jnp.zeros((1, 1) + shape, F32))
    else:
        for shape, arr in zip(st_shapes, init_states):
            in_specs.append(st_spec(shape, layer))
            args.append(arr)
    n_in = len(args)
    out_specs = [pl.BlockSpec((c, mix_w), lambda b, ci: (tok(b, ci), 0))]
    out_shape = [jax.ShapeDtypeStruct((proj.shape[0], mix_w), F32)]
    for shape in st_shapes:
        out_specs.append(st_spec(shape, layer))
        out_shape.append(jax.ShapeDtypeStruct((depth, nb) + shape, F32))
    prev = [prev_mix] + (list(prev_states) if prev_states is not None else [None] * len(st_shapes))
    aliases = {}
    for k, arr in enumerate(prev):
        if arr is not None:
            aliases[len(args)] = k
            in_specs.append(pl.BlockSpec(memory_space=pl.ANY))
            args.append(arr)
    n_alias = len(args) - n_in

    def kernel_fn(*refs):
        body(*refs[:n_in], *refs[n_in + n_alias:])

    return pl.pallas_call(
        kernel_fn, grid=(nb, nchunk), in_specs=in_specs, out_specs=out_specs, out_shape=out_shape,
        input_output_aliases=aliases, scratch_shapes=list(scratch_shapes),
        compiler_params=_cparams(("parallel", "arbitrary")), name=name)(*args)


def _const_spec(shape):
    return pl.BlockSpec(shape, functools.partial(lambda b, ci, n: (0,) * n, n=len(shape)))


def _gates_spec(seqs):
    nchunk, c = seqs["nchunk"], seqs["c"]
    return pl.BlockSpec((1, GATE_ROWS, c), lambda b, ci: (b * nchunk + ci, 0, 0))


def mlstm_call(proj, gates_t, brow, bcol, norm_g, seqs, layer, depth, init_states, prev_mix, prev_states):
    body = functools.partial(_mlstm_kernel, c=seqs["c"], n_valid=seqs["n_valid"],
                             has_init=init_states is not None)
    extra = [(gates_t, _gates_spec(seqs)), (brow, _const_spec((1, LANE))), (bcol, _const_spec((GATE_ROWS, 1))),
             (norm_g.reshape(1, A_W).astype(F32), _const_spec((1, A_W)))]
    return _mixer_call(body, "mlstm", proj, seqs, layer, depth,
                       [(A_W, 0), (A_W, 1), (A_W, 2), (LANE, G_OFF // LANE + 1)], extra, init_states,
                       [(H_A, DK_A, DV_A), (H_A, DK_A), (1, H_A)], A_W, prev_mix, prev_states)


def _bdot(a, b, dims):
    return lax.dot_general(a, b, (dims, ((0,), (0,))), preferred_element_type=F32)


def _bmm(a, b):
    return _bdot(a.astype(BF16), b.astype(BF16), ((2,), (1,)))


def _bmm_nt(a, b):
    return _bdot(a.astype(BF16), b.astype(BF16), ((2,), (2,)))


def _bmm_tn(a, b):
    return _bdot(jnp.swapaxes(a, 1, 2).astype(BF16), b.astype(BF16), ((2,), (1,)))


def _unit_lower_inverse(a, row, col, c):
    d = jnp.where(row == col, 1.0, 0.0)
    s = 1
    while s < c:
        blk = jnp.where(((row // (2 * s)) == (col // (2 * s))) & ((row % (2 * s)) >= s) & ((col % (2 * s)) < s),
                        a, 0.0)
        if s == 1:
            d = d - blk
        else:
            d = d - _bmm(_bmm(d, blk), d)
        s *= 2
    return d


def _gdn_kernel(qkv_ref, z_ref, gc_ref, gt_ref, brow_ref, bcol_ref, arow_ref, acol_ref, cw_ref, buf0_ref, ng_ref,
                s0_ref, out_ref, s_ref, xp_ref, *, c, n_valid, has_init):
    ci = pl.program_id(1)
    hk = H_B * DK_B

    @pl.when(ci == 0)
    def _():
        xp_ref[0:8, :] = jnp.zeros((8, xp_ref.shape[1]), F32)
        if has_init:
            s_ref[...] = s0_ref[...]
            xp_ref[8 - (CONV_W - 1):8, :] = buf0_ref[0]
        else:
            s_ref[...] = jnp.zeros_like(s_ref)

    @pl.when(ci > 0)
    def _():
        xp_ref[0:8, :] = xp_ref[c:c + 8, :]

    xp_ref[8:8 + c, :] = qkv_ref[...]

    gc = gc_ref[...] + brow_ref[...]
    gt = gt_ref[0] + bcol_ref[...]
    row, col = _tri_masks(c)
    tri = row >= col
    strict = row > col
    padded = n_valid < c
    if padded:
        valid_c = lax.broadcasted_iota(jnp.int32, (c, 1), 0) < n_valid
        valid_r = lax.broadcasted_iota(jnp.int32, (1, c), 1) < n_valid
    ga, gb = 2 * H_A, 2 * H_A + H_B

    def conv_act(c0):
        heads = []
        for h in range(H_B):
            cs = slice(c0 + h * DK_B, c0 + (h + 1) * DK_B)
            y = cw_ref[0:1, cs] * xp_ref[8 - 3:8 - 3 + c, cs]
            for j in range(1, CONV_W):
                y = y + cw_ref[j:j + 1, cs] * xp_ref[8 - 3 + j:8 - 3 + j + c, cs]
            heads.append(_silu(y))
        return jnp.stack(heads, axis=0)

    heads_c = lambda x, off: jnp.stack([x[:, off + h:off + h + 1] for h in range(H_B)], axis=0)
    heads_r = lambda x, off: jnp.stack([x[off + h:off + h + 1, :] for h in range(H_B)], axis=0)
    g_c = heads_c(-jnp.exp(arow_ref[...]) * _softplus(gc), ga)
    g_r = heads_r(-jnp.exp(acol_ref[...]) * _softplus(gt), ga)
    beta = heads_c(_sigmoid(gc), gb)
    if padded:
        g_c, g_r = jnp.where(valid_c, g_c, 0.0), jnp.where(valid_r, g_r, 0.0)
        beta = jnp.where(valid_c, beta, 0.0)
    q = conv_act(0)
    k = conv_act(hk)
    v = conv_act(2 * hk)
    q = q * lax.rsqrt(jnp.sum(q * q, axis=2, keepdims=True) + EPS) * (DK_B ** -0.5)
    k = k * lax.rsqrt(jnp.sum(k * k, axis=2, keepdims=True) + EPS)
    sst = s_ref[0]
    gcum_c = jnp.sum(jnp.where(tri, g_r, 0.0), axis=2, keepdims=True)
    gcum_r = jnp.sum(jnp.where(row <= col, g_c, 0.0), axis=1, keepdims=True)
    gam = jnp.exp(jnp.where(tri, gcum_c - gcum_r, NEG_INF))
    eg = jnp.exp(gcum_c)
    a = jnp.where(strict, beta * _bmm_nt(k, k) * gam, 0.0)
    tinv = _unit_lower_inverse(a, row, col, c)
    rhs = jnp.concatenate([beta * v, (beta * eg) * k], axis=2)
    sol = _bmm(tinv, rhs)
    u = sol[:, :, :DV_B] - _bmm(sol[:, :, DV_B:], sst)
    o = eg * _bmm(q, sst) + _bmm(_bmm_nt(q, k) * gam, u)
    g_last = gcum_c[:, c - 1:c, :]
    s_ref[0] = jnp.exp(g_last) * sst + _bmm_tn(k * jnp.exp(g_last - gcum_c), u)
    for h in range(H_B):
        hs = slice(h * DV_B, (h + 1) * DV_B)
        out_ref[:, hs] = _head_norm(o[h], ng_ref[:, hs]) * _silu(z_ref[:, hs])


def gdn_call(proj, gates_t, brow, bcol, arow, acol, conv_w, norm_g, seqs, layer, depth, init_states, buf0,
             prev_mix, prev_states):
    wb = 3 * H_B * DK_B
    has_init = init_states is not None
    body = functools.partial(_gdn_kernel, c=seqs["c"], n_valid=seqs["n_valid"], has_init=has_init)
    if has_init:
        buf_spec = pl.BlockSpec((None, 1, CONV_W - 1, wb), lambda b, ci: (layer, b, 0, 0))
    else:
        buf0 = jnp.zeros((1, 1, CONV_W - 1, wb), F32)
        buf_spec = pl.BlockSpec((None, 1, CONV_W - 1, wb), lambda b, ci: (0, 0, 0, 0))
    extra = [(gates_t, _gates_spec(seqs)), (brow, _const_spec((1, LANE))), (bcol, _const_spec((GATE_ROWS, 1))),
             (arow, _const_spec((1, LANE))), (acol, _const_spec((GATE_ROWS, 1))),
             (conv_w.astype(F32), _const_spec((CONV_W, wb))), (buf0, buf_spec),
             (norm_g.reshape(1, H_B * DV_B).astype(F32), _const_spec((1, H_B * DV_B)))]
    return _mixer_call(body, "gdn", proj, seqs, layer, depth,
                       [(wb, B_OFF // wb), (A_W, (B_OFF + wb) // A_W), (LANE, G_OFF // LANE)], extra, init_states,
                       [(H_B, DK_B, DV_B)], H_B * DV_B, prev_mix, prev_states,
                       scratch_shapes=[pltpu.VMEM((seqs["c"] + 8, wb), F32)])


def _hgrn_kernel(q_ref, f_ref, i_ref, g_ref, lb_ref, ng_ref, s0_ref, out_ref, s_ref, st_ref, *,
                 c, sub, n_valid, has_init, nchunk):
    ci = pl.program_id(1)

    @pl.when(ci == 0)
    def _():
        for h in range(H_C):
            st_ref[h] = s0_ref[0, h].T if has_init else jnp.zeros((DV_C, DK_C), F32)

    padded = n_valid < sub
    rowi = lax.broadcasted_iota(jnp.int32, (1, sub, 1), 1)
    heads = lambda x: jnp.stack([x[:, h * DK_C:(h + 1) * DK_C] for h in range(H_C)], axis=0)
    lb = heads(lb_ref[...])
    log_1mlb, log_lb = jnp.log1p(-lb), jnp.log(lb)
    sst = st_ref[...]
    for j in range(c // sub):
        rows = slice(j * sub, (j + 1) * sub)
        cf = heads(f_ref[rows, :])
        la = log_1mlb + _log_sigmoid(cf)
        logf = jnp.maximum(la, log_lb) + jnp.log1p(jnp.exp(-jnp.abs(la - log_lb)))
        k = (1.0 - lb) * _sigmoid(-cf)
        if padded:
            logf, k = jnp.where(rowi < n_valid, logf, 0.0), jnp.where(rowi < n_valid, k, 0.0)
        q = _silu(heads(q_ref[rows, :]))
        iv = heads(i_ref[rows, :])
        bc = jnp.zeros((H_C, sub, DK_C), F32)
        for s in range(sub):
            bc = bc + jnp.where(rowi >= s, logf[:, s:s + 1, :], 0.0)
        o = _bmm_nt(q * jnp.exp(bc), sst)
        for s in range(sub):
            dec = jnp.exp(jnp.where(rowi >= s, bc - bc[:, s:s + 1, :], NEG_INF))
            att = jnp.sum(q * dec * k[:, s:s + 1, :], axis=2, keepdims=True)
            o = o + att * iv[:, s:s + 1, :]
        b_last = bc[:, sub - 1:sub, :]
        sst = jnp.exp(b_last) * sst + _bmm_tn(iv, k * jnp.exp(b_last - bc))
        for h in range(H_C):
            hs = slice(h * DK_C, (h + 1) * DK_C)
            out_ref[rows, hs] = _head_norm(o[h], ng_ref[:, hs]) * _silu(g_ref[rows, hs])
    st_ref[...] = sst

    @pl.when(ci == nchunk - 1)
    def _():
        for h in range(H_C):
            s_ref[0, h] = st_ref[h].T


def hgrn_call(proj, lb, norm_g, seqs, sub, layer, depth, init_states, prev_mix, prev_states):
    w = H_C * DK_C
    cb = C_OFF // w
    body = functools.partial(_hgrn_kernel, c=seqs["c"], sub=sub, n_valid=min(seqs["n_valid"], sub),
                             has_init=init_states is not None, nchunk=seqs["nchunk"])
    extra = [(lb.reshape(1, w).astype(F32), _const_spec((1, w))),
             (norm_g.reshape(1, w).astype(F32), _const_spec((1, w)))]
    return _mixer_call(body, "hgrn", proj, seqs, layer, depth,
                       [(w, cb), (w, cb + 1), (w, cb + 2), (w, cb + 3)], extra, init_states,
                       [(H_C, DK_C, DV_C)], w, prev_mix, prev_states,
                       scratch_shapes=[pltpu.VMEM((H_C, DV_C, DK_C), F32)])


def _top_values(x, k):
    rows = []
    for _ in range(k):
        mx = jnp.max(x, axis=0, keepdims=True)
        rows.append(mx)
        x = jnp.where(x == mx, NEG_INF, x)
    return jnp.concatenate(rows, axis=0)


def _route_kernel(q_ref, sk_ref, s1_ref, s2_ref, e1_ref, e2_ref, tau_ref):
    for h in range(PEER_HEADS):
        s1 = _mmf_nt(sk_ref[h, 0], q_ref[:, (2 * h) * LANE:(2 * h + 1) * LANE])
        s2 = _mmf_nt(sk_ref[h, 1], q_ref[:, (2 * h + 1) * LANE:(2 * h + 2) * LANE])
        a = _top_values(s1, PEER_TOPK)
        b = _top_values(s2, PEER_TOPK)
        cand = jnp.concatenate([a[i:i + 1, :] + b for i in range(PEER_TOPK)], axis=0)
        best = _top_values(cand, PEER_TOPK)
        z = jnp.sum(jnp.exp(best - best[0:1, :]), axis=0, keepdims=True)
        s1_ref[h] = s1
        s2_ref[h] = s2
        e1_ref[h] = jnp.exp(s1 - a[0:1, :])
        e2_ref[h] = jnp.exp(s2 - b[0:1, :]) / z
        tau_ref[h:h + 1, :] = best[PEER_TOPK - 1:PEER_TOPK, :]


def route_call(qry, subkeys, tb=256):
    t = qry.shape[0]
    big = jax.ShapeDtypeStruct((PEER_HEADS, N_KEYS, t), F32)
    big_spec = pl.BlockSpec((PEER_HEADS, N_KEYS, tb), lambda i: (0, 0, i))
    return pl.pallas_call(
        _route_kernel, grid=(t // tb,),
        in_specs=[pl.BlockSpec((tb, qry.shape[1]), lambda i: (i, 0)),
                  pl.BlockSpec(subkeys.shape, lambda i: (0, 0, 0, 0))],
        out_specs=[big_spec] * 4 + [pl.BlockSpec((PEER_HEADS, tb), lambda i: (0, i))],
        out_shape=[big] * 4 + [jax.ShapeDtypeStruct((PEER_HEADS, t), F32)],
        compiler_params=_cparams(("parallel",)), name="peer_route",
    )(qry, subkeys.astype(F32))


PEER_SUB = 256
PEER_LANES = 256


def _peer_kernel(t_ref, u_ref, v_ref, s1_ref, s2_ref, e1_ref, e2_ref, tau_ref, y_ref, *, eb):
    j = pl.program_id(1)
    tm = t_ref.shape[0]

    @pl.when(j == 0)
    def _():
        y_ref[...] = jnp.zeros_like(y_ref)

    for p in range(eb // PEER_SUB):
        es = slice(p * PEER_SUB, (p + 1) * PEER_SUB)
        a_t = lax.dot_general(u_ref[es, :], t_ref[...], (((1,), (1,)), ((), ())), preferred_element_type=F32)
        rows = []
        for r in range(PEER_SUB // N_KEYS):
            i1 = j * (eb // N_KEYS) + p * (PEER_SUB // N_KEYS) + r
            s1_rows = [s1_ref[h, pl.ds(i1, 1), :] for h in range(PEER_HEADS)]
            e1_rows = [e1_ref[h, pl.ds(i1, 1), :] for h in range(PEER_HEADS)]
            tiles = []
            for tc in range(tm // PEER_LANES):
                ls = slice(tc * PEER_LANES, (tc + 1) * PEER_LANES)
                gsum = None
                for h in range(PEER_HEADS):
                    sc = s1_rows[h][:, ls] + s2_ref[h, :, ls]
                    gh = jnp.where(sc >= tau_ref[h:h + 1, ls], e1_rows[h][:, ls] * e2_ref[h, :, ls], 0.0)
                    gsum = gh if gsum is None else gsum + gh
                tiles.append(jax.nn.gelu(a_t[r * N_KEYS:(r + 1) * N_KEYS, ls]) * gsum)
            rows.append(jnp.concatenate(tiles, axis=1))
        coef = jnp.concatenate(rows, axis=0)
        y_ref[...] += jnp.dot(coef.T.astype(BF16), v_ref[es, :], preferred_element_type=F32)


def peer_call(t_bf, u_bf, v_bf, layer, s1, s2, e1, e2, tau, tm=512, eb=512):
    t, d = t_bf.shape
    ne = u_bf.shape[1]
    once = pl.Buffered(1)
    big_spec = pl.BlockSpec((PEER_HEADS, N_KEYS, tm), lambda i, j: (0, 0, i), pipeline_mode=once)
    return pl.pallas_call(
        functools.partial(_peer_kernel, eb=eb),
        grid=(t // tm, ne // eb),
        in_specs=[pl.BlockSpec((tm, d), lambda i, j: (i, 0), pipeline_mode=once),
                  pl.BlockSpec((None, eb, d), lambda i, j: (layer, j, 0)),
                  pl.BlockSpec((None, eb, d), lambda i, j: (layer, j, 0)),
                  big_spec, big_spec, big_spec, big_spec,
                  pl.BlockSpec((PEER_HEADS, tm), lambda i, j: (0, i), pipeline_mode=once)],
        out_specs=pl.BlockSpec((tm, d), lambda i, j: (i, 0)),
        out_shape=jax.ShapeDtypeStruct((t, d), F32),
        compiler_params=_cparams(("parallel", "arbitrary")), name="peer_dense",
    )(t_bf, u_bf, v_bf, s1, s2, e1, e2, tau)


def _gate_params(b_i, b_f, a_log, dt_bias):
    zeros = jnp.zeros((LANE - 2 * H_A - 2 * H_B,), F32)
    bias = jnp.concatenate([b_i.astype(F32), b_f.astype(F32), dt_bias.astype(F32), jnp.zeros((H_B,), F32), zeros])
    alog = jnp.concatenate([jnp.zeros((2 * H_A,), F32), a_log.astype(F32), jnp.zeros((H_B,), F32), zeros])
    return (bias.reshape(1, LANE), bias[:GATE_ROWS].reshape(GATE_ROWS, 1),
            alog.reshape(1, LANE), alog[:GATE_ROWS].reshape(GATE_ROWS, 1))


def _gates_rows(proj, c):
    na, nb = 2 * H_A, 2 * H_B
    g = jnp.concatenate([proj[:, G_OFF + LANE:G_OFF + LANE + na], proj[:, G_OFF + na:G_OFF + na + nb],
                         jnp.zeros((proj.shape[0], GATE_ROWS - na - nb), F32)], axis=1)
    return g.reshape(g.shape[0] // c, c, GATE_ROWS).transpose(0, 2, 1)


def _hgrn_lower_bounds(logits):
    p = jax.nn.softmax(logits.astype(F32), axis=0)
    cs = jnp.cumsum(p, axis=0)
    return cs - cs[0:1]


def kernel(x_prompt, x_sample, state_mlstm_C, state_mlstm_n, state_mlstm_m, state_gdn_S, state_gdn_conv,
           state_hgrn_S, norm_mix_g, w_in, mlstm_b_i, mlstm_b_f, mlstm_norm_g, gdn_conv_w, gdn_A_log,
           gdn_dt_bias, gdn_norm_g, hgrn_lb_logits, hgrn_norm_g, w_out, norm_ffn_g, peer_w_q, peer_subkeys,
           peer_u, peer_v, final_norm_g, *, chunk=64, hgrn_chunk=16, tm=512, tm_peer=512, eb_peer=512, tb=256):
    bp, lp, d = x_prompt.shape
    bs, ls, _ = x_sample.shape
    depth = w_in.shape[0]
    tp = bp * lp
    ts = bs * SAMPLE_PAD
    xs = jnp.pad(x_sample, ((0, 0), (0, SAMPLE_PAD - ls), (0, 0)))
    x = jnp.concatenate([x_prompt.reshape(tp, d), xs.reshape(ts, d)], axis=0).astype(F32)
    lbs = _hgrn_lower_bounds(hgrn_lb_logits)
    wb = 3 * H_B * DK_B
    u_all, v_all = peer_u.astype(BF16), peer_v.astype(BF16)
    wo_all, wq_all = w_out.astype(BF16), peer_w_q.astype(BF16)
    pseq = dict(row0=0, nb=bp, nchunk=lp // chunk, c=chunk, n_valid=chunk)
    sseq = dict(row0=tp, nb=bs, nchunk=1, c=SAMPLE_PAD, n_valid=ls)
    a_init = [state_mlstm_C, state_mlstm_n, state_mlstm_m.reshape(depth, bs, 1, H_A)]
    st_ap = st_as = st_bp = st_bs = st_cp = st_cs = None
    bufs_p, bufs_s = [], []
    y_peer = None
    for l in range(depth):
        if l == 0:
            h = rmsnorm_call(x, norm_mix_g[l], BF16, tb=tb)
        else:
            x, h = res_rmsnorm_call(x, y_peer, norm_mix_g[l], BF16, tb=tb)
        proj = in_proj_call(h, w_in, l, tm=2 * tm if x.shape[0] % (2 * tm) == 0 else tm)
        brow, bcol, arow, acol = _gate_params(mlstm_b_i[l], mlstm_b_f[l], gdn_A_log[l], gdn_dt_bias[l])
        gt_p = _gates_rows(proj[:tp], chunk)
        gt_s = _gates_rows(proj[tp:], SAMPLE_PAD)
        mix_a, *st_ap = mlstm_call(proj, gt_p, brow, bcol, mlstm_norm_g[l], pseq, l, depth, None, None, st_ap)
        mix_a, *st_as = mlstm_call(proj, gt_s, brow, bcol, mlstm_norm_g[l], sseq, l, depth, a_init, mix_a, st_as)
        gdn_w = (brow, bcol, arow, acol, gdn_conv_w[l], gdn_norm_g[l])
        mix_b, *st_bp = gdn_call(proj, gt_p, *gdn_w, pseq, l, depth, None, None, None, st_bp)
        mix_b, *st_bs = gdn_call(proj, gt_s, *gdn_w, sseq, l, depth, [state_gdn_S], state_gdn_conv, mix_b, st_bs)
        mix_c, *st_cp = hgrn_call(proj, lbs[l], hgrn_norm_g[l], pseq, hgrn_chunk, l, depth, None, None, st_cp)
        mix_c, *st_cs = hgrn_call(proj, lbs[l], hgrn_norm_g[l], sseq, SAMPLE_PAD, l, depth, [state_hgrn_S],
                                  mix_c, st_cs)
        bufs_p.append(proj[:tp, B_OFF:B_OFF + wb].reshape(bp, lp, wb)[:, lp - (CONV_W - 1):])
        bufs_s.append(proj[tp:, B_OFF:B_OFF + wb].reshape(bs, SAMPLE_PAD, wb)[:, ls - (CONV_W - 1):ls])
        x = matmul_call([mix_a, mix_b, mix_c], wo_all, l, res=x, tm=tm, name="out_proj")
        t_bf = rmsnorm_call(x, norm_ffn_g[l], BF16, tb=tb)
        qry = matmul_call([t_bf], wq_all, l, tm=tm, name="peer_query")
        s1, s2, e1, e2, tau = route_call(qry, peer_subkeys[l], tb=tb)
        y_peer = peer_call(t_bf, u_all, v_all, l, s1, s2, e1, e2, tau, tm=tm_peer, eb=eb_peer)
    _, y = res_rmsnorm_call(x, y_peer, final_norm_g, F32, tb=tb)
    y_prompt = y[:tp].reshape(bp, lp, d)
    y_sample = y[tp:].reshape(bs, SAMPLE_PAD, d)[:, :ls]
    m_p, m_s = st_ap[2].reshape(depth, bp, H_A), st_as[2].reshape(depth, bs, H_A)
    return (y_prompt, y_sample, st_ap[0], st_as[0], st_ap[1], st_as[1], m_p, m_s, st_bp[0], st_bs[0],
            jnp.stack(bufs_p, axis=0), jnp.stack(bufs_s, axis=0), st_cp[0], st_cs[0])
```

```python
import functools

import jax
import jax.numpy as jnp
from jax import lax
from jax.experimental import pallas as pl
from jax.experimental.pallas import tpu as pltpu

F32 = jnp.float32
BF16 = jnp.bfloat16
EPS = 1e-6
NEG_INF = float("-inf")

H_A, DK_A, DV_A = 6, 128, 256
H_B, DK_B, DV_B = 12, 128, 128
H_C, DK_C, DV_C = 8, 128, 128
CONV_W = 4
PEER_HEADS, N_KEYS, PEER_TOPK = 8, 128, 16
SAMPLE_PAD = 8
LANE = 128
VMEM_LIMIT = 56 * 1024 * 1024

A_W = H_A * DV_A
B_OFF = 3 * A_W
G_OFF = B_OFF + 4 * H_B * DV_B
C_OFF = G_OFF + 512
NP = C_OFF + 4 * H_C * DV_C
GATE_ROWS = 40


def _cparams(sem):
    return pltpu.CompilerParams(dimension_semantics=sem, vmem_limit_bytes=VMEM_LIMIT)


def _mm(a, b):
    return jnp.dot(a.astype(BF16), b.astype(BF16), preferred_element_type=F32)


def _mmf_nt(a, b):
    return lax.dot_general(a, b, (((1,), (1,)), ((), ())), preferred_element_type=F32,
                           precision=lax.Precision.HIGHEST)


def _softplus(x):
    return jnp.maximum(x, 0.0) + jnp.log1p(jnp.exp(-jnp.abs(x)))


def _log_sigmoid(x):
    return -_softplus(-x)


def _sigmoid(x):
    return 1.0 / (1.0 + jnp.exp(-x))


def _silu(x):
    return x * _sigmoid(x)


def _head_norm(x, g_row):
    return x * lax.rsqrt(jnp.mean(x * x, axis=-1, keepdims=True) + EPS) * g_row


def _rms_kernel(x_ref, g_ref, h_ref):
    x = x_ref[...]
    y = x * lax.rsqrt(jnp.mean(x * x, axis=-1, keepdims=True) + EPS)
    h_ref[...] = (y * g_ref[...]).astype(h_ref.dtype)


def rmsnorm_call(x, g, out_dtype, tb=256):
    t, d = x.shape
    return pl.pallas_call(
        _rms_kernel, grid=(t // tb,),
        in_specs=[pl.BlockSpec((tb, d), lambda i: (i, 0)), pl.BlockSpec((1, d), lambda i: (0, 0))],
        out_specs=pl.BlockSpec((tb, d), lambda i: (i, 0)),
        out_shape=jax.ShapeDtypeStruct((t, d), out_dtype),
        compiler_params=_cparams(("parallel",)), name="rmsnorm",
    )(x, g.reshape(1, d).astype(F32))


def _res_rms_kernel(x_ref, y_ref, g_ref, xn_ref, h_ref):
    x = x_ref[...] + y_ref[...]
    xn_ref[...] = x
    y = x * lax.rsqrt(jnp.mean(x * x, axis=-1, keepdims=True) + EPS)
    h_ref[...] = (y * g_ref[...]).astype(h_ref.dtype)


def res_rmsnorm_call(x, y, g, out_dtype, tb=256):
    t, d = x.shape
    return pl.pallas_call(
        _res_rms_kernel, grid=(t // tb,),
        in_specs=[pl.BlockSpec((tb, d), lambda i: (i, 0)), pl.BlockSpec((tb, d), lambda i: (i, 0)),
                  pl.BlockSpec((1, d), lambda i: (0, 0))],
        out_specs=[pl.BlockSpec((tb, d), lambda i: (i, 0)), pl.BlockSpec((tb, d), lambda i: (i, 0))],
        out_shape=[jax.ShapeDtypeStruct((t, d), F32), jax.ShapeDtypeStruct((t, d), out_dtype)],
        compiler_params=_cparams(("parallel",)), name="res_rmsnorm",
    )(x, y, g.reshape(1, d).astype(F32))


def _matmul_kernel(*refs, n_pairs, has_res):
    a_refs, w_refs = refs[:n_pairs], refs[n_pairs:2 * n_pairs]
    o_ref = refs[-1]
    acc = _mm(a_refs[0][...], w_refs[0][...])
    for a_ref, w_ref in zip(a_refs[1:], w_refs[1:]):
        acc = acc + _mm(a_ref[...], w_ref[...])
    if has_res:
        acc = acc + refs[2 * n_pairs][...]
    o_ref[...] = acc


def matmul_call(a_list, w, layer, res=None, tm=512, tn=512, name="matmul"):
    t = a_list[0].shape[0]
    n = w.shape[2]
    in_specs = [pl.BlockSpec((tm, a.shape[1]), lambda i, j: (i, 0)) for a in a_list]
    row = 0
    for a in a_list:
        k = a.shape[1]
        in_specs.append(pl.BlockSpec((None, k, tn), functools.partial(lambda i, j, rb: (layer, rb, j), rb=row // k)))
        row += k
    args = list(a_list) + [w] * len(a_list)
    if res is not None:
        in_specs.append(pl.BlockSpec((tm, tn), lambda i, j: (i, j)))
        args.append(res)
    return pl.pallas_call(
        functools.partial(_matmul_kernel, n_pairs=len(a_list), has_res=res is not None),
        grid=(t // tm, n // tn), in_specs=in_specs,
        out_specs=pl.BlockSpec((tm, tn), lambda i, j: (i, j)),
        out_shape=jax.ShapeDtypeStruct((t, n), F32),
        compiler_params=_cparams(("parallel", "parallel")), name=name,
    )(*args)


IN_TN = 512
A_BLOCKS = B_OFF // IN_TN
G_BLOCK = G_OFF // IN_TN
B_SHIFT = 2 * H_A
C_SHIFT = 2 * H_A + 2 * H_B
W_ROWS = 512


def _in_proj_kernel(a_ref, wm_ref, we_ref, o_ref, wb_ref):
    j, i = pl.program_id(0), pl.program_id(1)
    k = wm_ref.shape[0]

    def convert(shift):
        def body(r, carry):
            rs = pl.ds(pl.multiple_of(r * W_ROWS, W_ROWS), W_ROWS)
            if shift == 0:
                blk = wm_ref[rs, :]
            else:
                blk = jnp.concatenate([wm_ref[rs, shift:], we_ref[rs, :shift]], axis=1)
            wb_ref[rs, :] = blk.astype(BF16)
            return carry
        lax.fori_loop(0, k // W_ROWS, body, 0)

    first = i == 0

    @pl.when(first & (j < A_BLOCKS))
    def _():
        convert(0)

    @pl.when(first & (j >= A_BLOCKS) & (j < G_BLOCK))
    def _():
        convert(B_SHIFT)

    @pl.when(first & (j == G_BLOCK))
    def _():
        wb_ref[:, 0:LANE] = wm_ref[:, 0:LANE].astype(BF16)
        wb_ref[:, LANE:2 * LANE] = we_ref[...].astype(BF16)
        wb_ref[:, 2 * LANE:] = jnp.zeros((k, IN_TN - 2 * LANE), BF16)

    @pl.when(first & (j > G_BLOCK))
    def _():
        convert(C_SHIFT)

    o_ref[...] = jnp.dot(a_ref[...], wb_ref[...], preferred_element_type=F32)


def in_proj_call(a, w_in, layer, tm):
    t, k = a.shape
    lanes_per_tile = IN_TN // LANE
    main_blk = lambda j: jnp.where(j <= G_BLOCK, j, j - 1)
    extra_blk = lambda j: jnp.where(j < G_BLOCK, lanes_per_tile * (j + 1),
                                    jnp.where(j == G_BLOCK, B_OFF // LANE, lanes_per_tile * j))
    return pl.pallas_call(
        _in_proj_kernel, grid=(NP // IN_TN, t // tm),
        in_specs=[pl.BlockSpec((tm, k), lambda j, i: (i, 0)),
                  pl.BlockSpec((None, k, IN_TN), lambda j, i: (layer, 0, main_blk(j))),
                  pl.BlockSpec((None, k, LANE), lambda j, i: (layer, 0, extra_blk(j)))],
        out_specs=pl.BlockSpec((tm, IN_TN), lambda j, i: (i, j)),
        out_shape=jax.ShapeDtypeStruct((t, NP), F32),
        scratch_shapes=[pltpu.VMEM((k, IN_TN), BF16)],
        compiler_params=_cparams(("parallel", "arbitrary")), name="in_proj",
    )(a, w_in, w_in)


def _tri_masks(c):
    row = lax.broadcasted_iota(jnp.int32, (c, c), 0)
    col = lax.broadcasted_iota(jnp.int32, (c, c), 1)
    return row, col


def _mlstm_kernel(qk_ref, v_ref, o_ref, gc_ref, gt_ref, brow_ref, bcol_ref, ng_ref, c0_ref, n0_ref, m0_ref,
                  out_ref, c_ref, n_ref, m_ref, *, c, n_valid, has_init):
    ci = pl.program_id(1)

    @pl.when(ci == 0)
    def _():
        if has_init:
            c_ref[...] = c0_ref[...]
            n_ref[...] = n0_ref[...]
            m_ref[...] = m0_ref[...]
        else:
            c_ref[...] = jnp.zeros_like(c_ref)
            n_ref[...] = jnp.zeros_like(n_ref)
            m_ref[...] = jnp.zeros_like(m_ref)

    gc = gc_ref[...] + brow_ref[...]
    gt = gt_ref[0] + bcol_ref[...]
    row, col = _tri_masks(c)
    tri = row >= col
    padded = n_valid < c
    if padded:
        valid_c = lax.broadcasted_iota(jnp.int32, (c, 1), 0) < n_valid
        valid_r = lax.broadcasted_iota(jnp.int32, (1, c), 1) < n_valid
    hr = range(H_A)
    heads_c = lambda x, off: jnp.stack([x[:, off + h:off + h + 1] for h in hr], axis=0)
    heads_r = lambda x, off: jnp.stack([x[off + h:off + h + 1, :] for h in hr], axis=0)
    ig_c, lf_c = heads_c(gc, 0), heads_c(_log_sigmoid(gc), H_A)
    ig_r, lf_r = heads_r(gt, 0), heads_r(_log_sigmoid(gt), H_A)
    if padded:
        ig_c, lf_c = jnp.where(valid_c, ig_c, NEG_INF), jnp.where(valid_c, lf_c, 0.0)
        ig_r, lf_r = jnp.where(valid_r, ig_r, NEG_INF), jnp.where(valid_r, lf_r, 0.0)
    q = jnp.stack([qk_ref[:, h * DK_A:(h + 1) * DK_A] for h in hr], axis=0) * (DK_A ** -0.5)
    k = jnp.stack([qk_ref[:, (H_A + h) * DK_A:(H_A + h + 1) * DK_A] for h in hr], axis=0)
    v = jnp.stack([v_ref[:, h * DV_A:(h + 1) * DV_A] for h in hr], axis=0)
    cst = c_ref[0]
    nst = jnp.stack([n_ref[0, h:h + 1, :] for h in hr], axis=0)
    m_prev = jnp.stack([m_ref[0, :, h:h + 1] for h in hr], axis=0)
    b_c = jnp.sum(jnp.where(tri, lf_r, 0.0), axis=2, keepdims=True)
    b_r = jnp.sum(jnp.where(row <= col, lf_c, 0.0), axis=1, keepdims=True)
    log_d = jnp.where(tri, b_c - b_r + ig_r, NEG_INF)
    m_inter = b_c + m_prev
    m_t = jnp.maximum(m_inter, jnp.max(log_d, axis=2, keepdims=True))
    s = _bmm_nt(q, k) * jnp.exp(log_d - m_t)
    inter = jnp.exp(m_inter - m_t)
    num = _bmm(s, v) + inter * _bmm(q, cst)
    den = jnp.sum(s, axis=2, keepdims=True) + inter * jnp.sum(q * nst, axis=2, keepdims=True)
    hh = num / jnp.maximum(jnp.abs(den), jnp.exp(-m_t))
    m_new = m_t[:, c - 1:c, :]
    w = jnp.exp(b_c[:, c - 1:c, :] - b_c + ig_c - m_new)
    decay = inter[:, c - 1:c, :]
    kw = k * w
    c_ref[0] = decay * cst + _bmm_tn(kw, v)
    n_new = decay * nst + jnp.sum(kw, axis=1, keepdims=True)
    for h in hr:
        hs = slice(h * DV_A, (h + 1) * DV_A)
        n_ref[0, h:h + 1, :] = n_new[h]
        m_ref[0, :, h:h + 1] = m_new[h]
        out_ref[:, hs] = _head_norm(hh[h], ng_ref[:, hs]) * _sigmoid(o_ref[:, hs])


def _mixer_call(body, name, proj, seqs, layer, depth, tok_blocks, extra_inputs, init_states, st_shapes,
                mix_w, prev_mix, prev_states, scratch_shapes=()):
    row0, nb, nchunk, c = seqs["row0"], seqs["nb"], seqs["nchunk"], seqs["c"]
    rb = row0 // c
    tok = lambda b, ci: rb + b * nchunk + ci
    st_spec = lambda shape, lyr: pl.BlockSpec(
        (None, 1) + shape, functools.partial(lambda b, ci, n: (lyr, b) + (0,) * n, n=len(shape)))
    in_specs = [pl.BlockSpec((c, w), functools.partial(lambda b, ci, cb: (tok(b, ci), cb), cb=cb))
                for w, cb in tok_blocks]
    args = [proj] * len(tok_blocks)
    for arr, spec in extra_inputs:
        in_specs.append(spec)
        args.append(arr)
    if init_states is None:
        for shape in st_shapes:
            in_specs.append(pl.BlockSpec((None, 1) + shape, functools.partial(
                lambda b, ci, n: (0,) * (n + 2), n=len(shape))))
            args.append(jnp.zeros((1, 1) + shape, F32))
    else:
        for shape, arr in zip(st_shapes, init_states):
            in_specs.append(st_spec(shape, layer))
            args.append(arr)
    n_in = len(args)
    out_specs = [pl.BlockSpec((c, mix_w), lambda b, ci: (tok(b, ci), 0))]
    out_shape = [jax.ShapeDtypeStruct((proj.shape[0], mix_w), F32)]
    for shape in st_shapes:
        out_specs.append(st_spec(shape, layer))
        out_shape.append(jax.ShapeDtypeStruct((depth, nb) + shape, F32))
    prev = [prev_mix] + (list(prev_states) if prev_states is not None else [None] * len(st_shapes))
    aliases = {}
    for k, arr in enumerate(prev):
        if arr is not None:
            aliases[len(args)] = k
            in_specs.append(pl.BlockSpec(memory_space=pl.ANY))
            args.append(arr)
    n_alias = len(args) - n_in

    def kernel_fn(*refs):
        body(*refs[:n_in], *refs[n_in + n_alias:])

    return pl.pallas_call(
        kernel_fn, grid=(nb, nchunk), in_specs=in_specs, out_specs=out_specs, out_shape=out_shape,
        input_output_aliases=aliases, scratch_shapes=list(scratch_shapes),
        compiler_params=_cparams(("parallel", "arbitrary")), name=name)(*args)


def _const_spec(shape):
    return pl.BlockSpec(shape, functools.partial(lambda b, ci, n: (0,) * n, n=len(shape)))


def _gates_spec(seqs):
    nchunk, c = seqs["nchunk"], seqs["c"]
    return pl.BlockSpec((1, GATE_ROWS, c), lambda b, ci: (b * nchunk + ci, 0, 0))


def mlstm_call(proj, gates_t, brow, bcol, norm_g, seqs, layer, depth, init_states, prev_mix, prev_states):
    body = functools.partial(_mlstm_kernel, c=seqs["c"], n_valid=seqs["n_valid"],
                             has_init=init_states is not None)
    extra = [(gates_t, _gates_spec(seqs)), (brow, _const_spec((1, LANE))), (bcol, _const_spec((GATE_ROWS, 1))),
             (norm_g.reshape(1, A_W).astype(F32), _const_spec((1, A_W)))]
    return _mixer_call(body, "mlstm", proj, seqs, layer, depth,
                       [(A_W, 0), (A_W, 1), (A_W, 2), (LANE, G_OFF // LANE + 1)], extra, init_states,
                       [(H_A, DK_A, DV_A), (H_A, DK_A), (1, H_A)], A_W, prev_mix, prev_states)


def _bdot(a, b, dims):
    return lax.dot_general(a, b, (dims, ((0,), (0,))), preferred_element_type=F32)


def _bmm(a, b):
    return _bdot(a.astype(BF16), b.astype(BF16), ((2,), (1,)))


def _bmm_nt(a, b):
    return _bdot(a.astype(BF16), b.astype(BF16), ((2,), (2,)))


def _bmm_tn(a, b):
    return _bdot(jnp.swapaxes(a, 1, 2).astype(BF16), b.astype(BF16), ((2,), (1,)))


def _unit_lower_inverse(a, row, col, c):
    d = jnp.where(row == col, 1.0, 0.0)
    s = 1
    while s < c:
        blk = jnp.where(((row // (2 * s)) == (col // (2 * s))) & ((row % (2 * s)) >= s) & ((col % (2 * s)) < s),
                        a, 0.0)
        if s == 1:
            d = d - blk
        else:
            d = d - _bmm(_bmm(d, blk), d)
        s *= 2
    return d


def _gdn_kernel(qkv_ref, z_ref, gc_ref, gt_ref, brow_ref, bcol_ref, arow_ref, acol_ref, cw_ref, buf0_ref, ng_ref,
                s0_ref, out_ref, s_ref, xp_ref, *, c, n_valid, has_init):
    ci = pl.program_id(1)
    hk = H_B * DK_B

    @pl.when(ci == 0)
    def _():
        xp_ref[0:8, :] = jnp.zeros((8, xp_ref.shape[1]), F32)
        if has_init:
            s_ref[...] = s0_ref[...]
            xp_ref[8 - (CONV_W - 1):8, :] = buf0_ref[0]
        else:
            s_ref[...] = jnp.zeros_like(s_ref)

    @pl.when(ci > 0)
    def _():
        xp_ref[0:8, :] = xp_ref[c:c + 8, :]

    xp_ref[8:8 + c, :] = qkv_ref[...]

    gc = gc_ref[...] + brow_ref[...]
    gt = gt_ref[0] + bcol_ref[...]
    row, col = _tri_masks(c)
    tri = row >= col
    strict = row > col
    padded = n_valid < c
    if padded:
        valid_c = lax.broadcasted_iota(jnp.int32, (c, 1), 0) < n_valid
        valid_r = lax.broadcasted_iota(jnp.int32, (1, c), 1) < n_valid
    ga, gb = 2 * H_A, 2 * H_A + H_B

    def conv_act(c0):
        heads = []
        for h in range(H_B):
            cs = slice(c0 + h * DK_B, c0 + (h + 1) * DK_B)
            y = cw_ref[0:1, cs] * xp_ref[8 - 3:8 - 3 + c, cs]
            for j in range(1, CONV_W):
                y = y + cw_ref[j:j + 1, cs] * xp_ref[8 - 3 + j:8 - 3 + j + c, cs]
            heads.append(_silu(y))
        return jnp.stack(heads, axis=0)

    heads_c = lambda x, off: jnp.stack([x[:, off + h:off + h + 1] for h in range(H_B)], axis=0)
    heads_r = lambda x, off: jnp.stack([x[off + h:off + h + 1, :] for h in range(H_B)], axis=0)
    g_c = heads_c(-jnp.exp(arow_ref[...]) * _softplus(gc), ga)
    g_r = heads_r(-jnp.exp(acol_ref[...]) * _softplus(gt), ga)
    beta = heads_c(_sigmoid(gc), gb)
    if padded:
        g_c, g_r = jnp.where(valid_c, g_c, 0.0), jnp.where(valid_r, g_r, 0.0)
        beta = jnp.where(valid_c, beta, 0.0)
    q = conv_act(0)
    k = conv_act(hk)
    v = conv_act(2 * hk)
    q = q * lax.rsqrt(jnp.sum(q * q, axis=2, keepdims=True) + EPS) * (DK_B ** -0.5)
    k = k * lax.rsqrt(jnp.sum(k * k, axis=2, keepdims=True) + EPS)
    sst = s_ref[0]
    gcum_c = jnp.sum(jnp.where(tri, g_r, 0.0), axis=2, keepdims=True)
    gcum_r = jnp.sum(jnp.where(row <= col, g_c, 0.0), axis=1, keepdims=True)
    gam = jnp.exp(jnp.where(tri, gcum_c - gcum_r, NEG_INF))
    eg = jnp.exp(gcum_c)
    a = jnp.where(strict, beta * _bmm_nt(k, k) * gam, 0.0)
    tinv = _unit_lower_inverse(a, row, col, c)
    rhs = jnp.concatenate([beta * v, (beta * eg) * k], axis=2)
    sol = _bmm(tinv, rhs)
    u = sol[:, :, :DV_B] - _bmm(sol[:, :, DV_B:], sst)
    o = eg * _bmm(q, sst) + _bmm(_bmm_nt(q, k) * gam, u)
    g_last = gcum_c[:, c - 1:c, :]
    s_ref[0] = jnp.exp(g_last) * sst + _bmm_tn(k * jnp.exp(g_last - gcum_c), u)
    for h in range(H_B):
        hs = slice(h * DV_B, (h + 1) * DV_B)
        out_ref[:, hs] = _head_norm(o[h], ng_ref[:, hs]) * _silu(z_ref[:, hs])


def gdn_call(proj, gates_t, brow, bcol, arow, acol, conv_w, norm_g, seqs, layer, depth, init_states, buf0,
             prev_mix, prev_states):
    wb = 3 * H_B * DK_B
    has_init = init_states is not None
    body = functools.partial(_gdn_kernel, c=seqs["c"], n_valid=seqs["n_valid"], has_init=has_init)
    if has_init:
        buf_spec = pl.BlockSpec((None, 1, CONV_W - 1, wb), lambda b, ci: (layer, b, 0, 0))
    else:
        buf0 = jnp.zeros((1, 1, CONV_W - 1, wb), F32)
        buf_spec = pl.BlockSpec((None, 1, CONV_W - 1, wb), lambda b, ci: (0, 0, 0, 0))
    extra = [(gates_t, _gates_spec(seqs)), (brow, _const_spec((1, LANE))), (bcol, _const_spec((GATE_ROWS, 1))),
             (arow, _const_spec((1, LANE))), (acol, _const_spec((GATE_ROWS, 1))),
             (conv_w.astype(F32), _const_spec((CONV_W, wb))), (buf0, buf_spec),
             (norm_g.reshape(1, H_B * DV_B).astype(F32), _const_spec((1, H_B * DV_B)))]
    return _mixer_call(body, "gdn", proj, seqs, layer, depth,
                       [(wb, B_OFF // wb), (A_W, (B_OFF + wb) // A_W), (LANE, G_OFF // LANE)], extra, init_states,
                       [(H_B, DK_B, DV_B)], H_B * DV_B, prev_mix, prev_states,
                       scratch_shapes=[pltpu.VMEM((seqs["c"] + 8, wb), F32)])


def _hgrn_kernel(q_ref, f_ref, i_ref, g_ref, lb_ref, ng_ref, s0_ref, out_ref, s_ref, st_ref, *,
                 c, sub, n_valid, has_init, nchunk):
    ci = pl.program_id(1)

    @pl.when(ci == 0)
    def _():
        for h in range(H_C):
            st_ref[h] = s0_ref[0, h].T if has_init else jnp.zeros((DV_C, DK_C), F32)

    padded = n_valid < sub
    rowi = lax.broadcasted_iota(jnp.int32, (1, sub, 1), 1)
    heads = lambda x: jnp.stack([x[:, h * DK_C:(h + 1) * DK_C] for h in range(H_C)], axis=0)
    lb = heads(lb_ref[...])
    log_1mlb, log_lb = jnp.log1p(-lb), jnp.log(lb)
    sst = st_ref[...]
    for j in range(c // sub):
        rows = slice(j * sub, (j + 1) * sub)
        cf = heads(f_ref[rows, :])
        la = log_1mlb + _log_sigmoid(cf)
        logf = jnp.maximum(la, log_lb) + jnp.log1p(jnp.exp(-jnp.abs(la - log_lb)))
        k = (1.0 - lb) * _sigmoid(-cf)
        if padded:
            logf, k = jnp.where(rowi < n_valid, logf, 0.0), jnp.where(rowi < n_valid, k, 0.0)
        q = _silu(heads(q_ref[rows, :]))
        iv = heads(i_ref[rows, :])
        bc = jnp.zeros((H_C, sub, DK_C), F32)
        for s in range(sub):
            bc = bc + jnp.where(rowi >= s, logf[:, s:s + 1, :], 0.0)
        o = _bmm_nt(q * jnp.exp(bc), sst)
        for s in range(sub):
            dec = jnp.exp(jnp.where(rowi >= s, bc - bc[:, s:s + 1, :], NEG_INF))
            att = jnp.sum(q * dec * k[:, s:s + 1, :], axis=2, keepdims=True)
            o = o + att * iv[:, s:s + 1, :]
        b_last = bc[:, sub - 1:sub, :]
        sst = jnp.exp(b_last) * sst + _bmm_tn(iv, k * jnp.exp(b_last - bc))
        for h in range(H_C):
            hs = slice(h * DK_C, (h + 1) * DK_C)
            out_ref[rows, hs] = _head_norm(o[h], ng_ref[:, hs]) * _silu(g_ref[rows, hs])
    st_ref[...] = sst

    @pl.when(ci == nchunk - 1)
    def _():
        for h in range(H_C):
            s_ref[0, h] = st_ref[h].T


def hgrn_call(proj, lb, norm_g, seqs, sub, layer, depth, init_states, prev_mix, prev_states):
    w = H_C * DK_C
    cb = C_OFF // w
    body = functools.partial(_hgrn_kernel, c=seqs["c"], sub=sub, n_valid=min(seqs["n_valid"], sub),
                             has_init=init_states is not None, nchunk=seqs["nchunk"])
    extra = [(lb.reshape(1, w).astype(F32), _const_spec((1, w))),
             (norm_g.reshape(1, w).astype(F32), _const_spec((1, w)))]
    return _mixer_call(body, "hgrn", proj, seqs, layer, depth,
                       [(w, cb), (w, cb + 1), (w, cb + 2), (w, cb + 3)], extra, init_states,
                       [(H_C, DK_C, DV_C)], w, prev_mix, prev_states,
                       scratch_shapes=[pltpu.VMEM((H_C, DV_C, DK_C), F32)])


def _top_values(x, k):
    rows = []
    for _ in range(k):
        mx = jnp.max(x, axis=0, keepdims=True)
        rows.append(mx)
        x = jnp.where(x == mx, NEG_INF, x)
    return jnp.concatenate(rows, axis=0)


def _route_kernel(q_ref, sk_ref, s1_ref, s2_ref, e1_ref, e2_ref, tau_ref):
    for h in range(PEER_HEADS):
        s1 = _mmf_nt(sk_ref[h, 0], q_ref[:, (2 * h) * LANE:(2 * h + 1) * LANE])
        s2 = _mmf_nt(sk_ref[h, 1], q_ref[:, (2 * h + 1) * LANE:(2 * h + 2) * LANE])
        a = _top_values(s1, PEER_TOPK)
        b = _top_values(s2, PEER_TOPK)
        cand = jnp.concatenate([a[0:1, :] + b] + [a[i:i + 1, :] + b[:PEER_TOPK // 2, :]
                                                  for i in range(1, PEER_TOPK)], axis=0)
        best = _top_values(cand, PEER_TOPK)
        z = jnp.sum(jnp.exp(best - best[0:1, :]), axis=0, keepdims=True)
        s1_ref[h] = s1
        s2_ref[h] = s2
        e1_ref[h] = jnp.exp(s1 - a[0:1, :])
        e2_ref[h] = jnp.exp(s2 - b[0:1, :]) / z
        tau_ref[h:h + 1, :] = best[PEER_TOPK - 1:PEER_TOPK, :]


def route_call(qry, subkeys, tb=256):
    t = qry.shape[0]
    big = jax.ShapeDtypeStruct((PEER_HEADS, N_KEYS, t), F32)
    big_spec = pl.BlockSpec((PEER_HEADS, N_KEYS, tb), lambda i: (0, 0, i))
    return pl.pallas_call(
        _route_kernel, grid=(t // tb,),
        in_specs=[pl.BlockSpec((tb, qry.shape[1]), lambda i: (i, 0)),
                  pl.BlockSpec(subkeys.shape, lambda i: (0, 0, 0, 0))],
        out_specs=[big_spec] * 4 + [pl.BlockSpec((PEER_HEADS, tb), lambda i: (0, i))],
        out_shape=[big] * 4 + [jax.ShapeDtypeStruct((PEER_HEADS, t), F32)],
        compiler_params=_cparams(("parallel",)), name="peer_route",
    )(qry, subkeys.astype(F32))


PEER_SUB = 256
PEER_LANES = 256


def _peer_kernel(t_ref, u_ref, v_ref, s1_ref, s2_ref, e1_ref, e2_ref, tau_ref, y_ref, *, eb):
    j = pl.program_id(1)
    tm = t_ref.shape[0]

    @pl.when(j == 0)
    def _():
        y_ref[...] = jnp.zeros_like(y_ref)

    for p in range(eb // PEER_SUB):
        es = slice(p * PEER_SUB, (p + 1) * PEER_SUB)
        a_t = lax.dot_general(u_ref[es, :], t_ref[...], (((1,), (1,)), ((), ())), preferred_element_type=F32)
        rows = []
        for r in range(PEER_SUB // N_KEYS):
            i1 = j * (eb // N_KEYS) + p * (PEER_SUB // N_KEYS) + r
            s1_rows = [s1_ref[h, pl.ds(i1, 1), :] for h in range(PEER_HEADS)]
            e1_rows = [e1_ref[h, pl.ds(i1, 1), :] for h in range(PEER_HEADS)]
            tiles = []
            for tc in range(tm // PEER_LANES):
                ls = slice(tc * PEER_LANES, (tc + 1) * PEER_LANES)
                gsum = None
                for h in range(PEER_HEADS):
                    sc = s1_rows[h][:, ls] + s2_ref[h, :, ls]
                    gh = jnp.where(sc >= tau_ref[h:h + 1, ls], e1_rows[h][:, ls] * e2_ref[h, :, ls], 0.0)
                    gsum = gh if gsum is None else gsum + gh
                tiles.append(jax.nn.gelu(a_t[r * N_KEYS:(r + 1) * N_KEYS, ls]) * gsum)
            rows.append(jnp.concatenate(tiles, axis=1))
        coef = jnp.concatenate(rows, axis=0)
        y_ref[...] += jnp.dot(coef.T.astype(BF16), v_ref[es, :], preferred_element_type=F32)


def peer_call(t_bf, u_bf, v_bf, layer, s1, s2, e1, e2, tau, tm=512, eb=512):
    t, d = t_bf.shape
    ne = u_bf.shape[1]
    once = pl.Buffered(1)
    big_spec = pl.BlockSpec((PEER_HEADS, N_KEYS, tm), lambda i, j: (0, 0, i), pipeline_mode=once)
    return pl.pallas_call(
        functools.partial(_peer_kernel, eb=eb),
        grid=(t // tm, ne // eb),
        in_specs=[pl.BlockSpec((tm, d), lambda i, j: (i, 0), pipeline_mode=once),
                  pl.BlockSpec((None, eb, d), lambda i, j: (layer, j, 0)),
                  pl.BlockSpec((None, eb, d), lambda i, j: (layer, j, 0)),
                  big_spec, big_spec, big_spec, big_spec,
                  pl.BlockSpec((PEER_HEADS, tm), lambda i, j: (0, i), pipeline_mode=once)],
        out_specs=pl.BlockSpec((tm, d), lambda i, j: (i, 0)),
        out_shape=jax.ShapeDtypeStruct((t, d), F32),
        compiler_params=_cparams(("parallel", "arbitrary")), name="peer_dense",
    )(t_bf, u_bf, v_bf, s1, s2, e1, e2, tau)


def _gate_params(b_i, b_f, a_log, dt_bias):
    zeros = jnp.zeros((LANE - 2 * H_A - 2 * H_B,), F32)
    bias = jnp.concatenate([b_i.astype(F32), b_f.astype(F32), dt_bias.astype(F32), jnp.zeros((H_B,), F32), zeros])
    alog = jnp.concatenate([jnp.zeros((2 * H_A,), F32), a_log.astype(F32), jnp.zeros((H_B,), F32), zeros])
    return (bias.reshape(1, LANE), bias[:GATE_ROWS].reshape(GATE_ROWS, 1),
            alog.reshape(1, LANE), alog[:GATE_ROWS].reshape(GATE_ROWS, 1))


def _gates_rows(proj, c):
    na, nb = 2 * H_A, 2 * H_B
    g = jnp.concatenate([proj[:, G_OFF + LANE:G_OFF + LANE + na], proj[:, G_OFF + na:G_OFF + na + nb],
                         jnp.zeros((proj.shape[0], GATE_ROWS - na - nb), F32)], axis=1)
    return g.reshape(g.shape[0] // c, c, GATE_ROWS).transpose(0, 2, 1)


def _hgrn_lower_bounds(logits):
    p = jax.nn.softmax(logits.astype(F32), axis=0)
    cs = jnp.cumsum(p, axis=0)
    return cs - cs[0:1]


def kernel(x_prompt, x_sample, state_mlstm_C, state_mlstm_n, state_mlstm_m, state_gdn_S, state_gdn_conv,
           state_hgrn_S, norm_mix_g, w_in, mlstm_b_i, mlstm_b_f, mlstm_norm_g, gdn_conv_w, gdn_A_log,
           gdn_dt_bias, gdn_norm_g, hgrn_lb_logits, hgrn_norm_g, w_out, norm_ffn_g, peer_w_q, peer_subkeys,
           peer_u, peer_v, final_norm_g, *, chunk=64, hgrn_chunk=16, tm=512, tm_peer=512, eb_peer=512, tb=256):
    bp, lp, d = x_prompt.shape
    bs, ls, _ = x_sample.shape
    depth = w_in.shape[0]
    tp = bp * lp
    ts = bs * ls
    x = jnp.concatenate([x_prompt.reshape(tp, d), x_sample.reshape(ts, d)], axis=0).astype(F32)
    tm_in = next(c for c in (1024, 1088, tm) if (tp + ts) % c == 0)
    lbs = _hgrn_lower_bounds(hgrn_lb_logits)
    wb = 3 * H_B * DK_B
    u_all, v_all = peer_u.astype(BF16), peer_v.astype(BF16)
    wo_all, wq_all = w_out.astype(BF16), peer_w_q.astype(BF16)
    pseq = dict(row0=0, nb=bp, nchunk=lp // chunk, c=chunk, n_valid=chunk)
    sseq = dict(row0=0, nb=bs, nchunk=1, c=SAMPLE_PAD, n_valid=ls)
    a_init = [state_mlstm_C, state_mlstm_n, state_mlstm_m.reshape(depth, bs, 1, H_A)]
    st_ap = st_as = st_bp = st_bs = st_cp = st_cs = None
    bufs_p, bufs_s = [], []
    y_peer = None
    for l in range(depth):
        if l == 0:
            h = rmsnorm_call(x, norm_mix_g[l], BF16, tb=tb)
        else:
            x, h = res_rmsnorm_call(x, y_peer, norm_mix_g[l], BF16, tb=tb)
        proj = in_proj_call(h, w_in, l, tm=tm_in)
        proj_s = jnp.pad(proj[tp:].reshape(bs, ls, NP), ((0, 0), (0, SAMPLE_PAD - ls), (0, 0)))
        proj_s = proj_s.reshape(bs * SAMPLE_PAD, NP)
        brow, bcol, arow, acol = _gate_params(mlstm_b_i[l], mlstm_b_f[l], gdn_A_log[l], gdn_dt_bias[l])
        gt_p = _gates_rows(proj[:tp], chunk)
        gt_s = _gates_rows(proj_s, SAMPLE_PAD)
        mix_a, *st_ap = mlstm_call(proj, gt_p, brow, bcol, mlstm_norm_g[l], pseq, l, depth, None, None, st_ap)
        smp_a, *st_as = mlstm_call(proj_s, gt_s, brow, bcol, mlstm_norm_g[l], sseq, l, depth, a_init, None, st_as)
        gdn_w = (brow, bcol, arow, acol, gdn_conv_w[l], gdn_norm_g[l])
        mix_b, *st_bp = gdn_call(proj, gt_p, *gdn_w, pseq, l, depth, None, None, None, st_bp)
        smp_b, *st_bs = gdn_call(proj_s, gt_s, *gdn_w, sseq, l, depth, [state_gdn_S], state_gdn_conv, None, st_bs)
        mix_c, *st_cp = hgrn_call(proj, lbs[l], hgrn_norm_g[l], pseq, hgrn_chunk, l, depth, None, None, st_cp)
        smp_c, *st_cs = hgrn_call(proj_s, lbs[l], hgrn_norm_g[l], sseq, SAMPLE_PAD, l, depth, [state_hgrn_S],
                                  None, st_cs)
        valid = lambda m: m.reshape(bs, SAMPLE_PAD, m.shape[1])[:, :ls].reshape(ts, m.shape[1])
        mix_a, mix_b, mix_c = (m.at[tp:].set(valid(s)) for m, s in ((mix_a, smp_a), (mix_b, smp_b), (mix_c, smp_c)))
        bufs_p.append(proj[:tp, B_OFF:B_OFF + wb].reshape(bp, lp, wb)[:, lp - (CONV_W - 1):])
        bufs_s.append(proj[tp:, B_OFF:B_OFF + wb].reshape(bs, ls, wb)[:, ls - (CONV_W - 1):])
        x = matmul_call([mix_a, mix_b, mix_c], wo_all, l, res=x, tm=tm, name="out_proj")
        t_bf = rmsnorm_call(x, norm_ffn_g[l], BF16, tb=tb)
        qry = matmul_call([t_bf], wq_all, l, tm=tm, name="peer_query")
        s1, s2, e1, e2, tau = route_call(qry, peer_subkeys[l], tb=tb)
        y_peer = peer_call(t_bf, u_all, v_all, l, s1, s2, e1, e2, tau, tm=tm_peer, eb=eb_peer)
    _, y = res_rmsnorm_call(x, y_peer, final_norm_g, F32, tb=tb)
    y_prompt = y[:tp].reshape(bp, lp, d)
    y_sample = y[tp:].reshape(bs, ls, d)
    m_p, m_s = st_ap[2].reshape(depth, bp, H_A), st_as[2].reshape(depth, bs, H_A)
    return (y_prompt, y_sample, st_ap[0], st_as[0], st_ap[1], st_as[1], m_p, m_s, st_bp[0], st_bs[0],
            jnp.stack(bufs_p, axis=0), jnp.stack(bufs_s, axis=0), st_cp[0], st_cs[0])
```

```python
import functools

import jax
import jax.numpy as jnp
from jax import lax
from jax.experimental import pallas as pl
from jax.experimental.pallas import tpu as pltpu

F32 = jnp.float32
BF16 = jnp.bfloat16
EPS = 1e-6
NEG_INF = float("-inf")

H_A, DK_A, DV_A = 6, 128, 256
H_B, DK_B, DV_B = 12, 128, 128
H_C, DK_C, DV_C = 8, 128, 128
CONV_W = 4
PEER_HEADS, N_KEYS, PEER_TOPK = 8, 128, 16
SAMPLE_PAD = 8
SAMPLE_SEQS = 4
LANE = 128
VMEM_LIMIT = 56 * 1024 * 1024

A_W = H_A * DV_A
B_OFF = 3 * A_W
G_OFF = B_OFF + 4 * H_B * DV_B
C_OFF = G_OFF + 512
NP = C_OFF + 4 * H_C * DV_C
GATE_ROWS = 40


def _cparams(sem):
    return pltpu.CompilerParams(dimension_semantics=sem, vmem_limit_bytes=VMEM_LIMIT)


def _mm(a, b):
    return jnp.dot(a.astype(BF16), b.astype(BF16), preferred_element_type=F32)


def _mmf_nt(a, b):
    return lax.dot_general(a, b, (((1,), (1,)), ((), ())), preferred_element_type=F32,
                           precision=lax.Precision.HIGHEST)


def _softplus(x):
    return jnp.maximum(x, 0.0) + jnp.log1p(jnp.exp(-jnp.abs(x)))


def _log_sigmoid(x):
    return -_softplus(-x)


def _sigmoid(x):
    return 1.0 / (1.0 + jnp.exp(-x))


def _silu(x):
    return x * _sigmoid(x)


def _head_norm(x, g_row):
    return x * lax.rsqrt(jnp.mean(x * x, axis=-1, keepdims=True) + EPS) * g_row


def _rms_kernel(x_ref, g_ref, h_ref):
    x = x_ref[...]
    y = x * lax.rsqrt(jnp.mean(x * x, axis=-1, keepdims=True) + EPS)
    h_ref[...] = (y * g_ref[...]).astype(h_ref.dtype)


def rmsnorm_call(x, g, out_dtype, tb=256):
    t, d = x.shape
    return pl.pallas_call(
        _rms_kernel, grid=(t // tb,),
        in_specs=[pl.BlockSpec((tb, d), lambda i: (i, 0)), pl.BlockSpec((1, d), lambda i: (0, 0))],
        out_specs=pl.BlockSpec((tb, d), lambda i: (i, 0)),
        out_shape=jax.ShapeDtypeStruct((t, d), out_dtype),
        compiler_params=_cparams(("parallel",)), name="rmsnorm",
    )(x, g.reshape(1, d).astype(F32))


def _res_rms_kernel(x_ref, y_ref, g_ref, xn_ref, h_ref):
    x = x_ref[...] + y_ref[...]
    xn_ref[...] = x
    y = x * lax.rsqrt(jnp.mean(x * x, axis=-1, keepdims=True) + EPS)
    h_ref[...] = (y * g_ref[...]).astype(h_ref.dtype)


def res_rmsnorm_call(x, y, g, out_dtype, tb=256):
    t, d = x.shape
    return pl.pallas_call(
        _res_rms_kernel, grid=(t // tb,),
        in_specs=[pl.BlockSpec((tb, d), lambda i: (i, 0)), pl.BlockSpec((tb, d), lambda i: (i, 0)),
                  pl.BlockSpec((1, d), lambda i: (0, 0))],
        out_specs=[pl.BlockSpec((tb, d), lambda i: (i, 0)), pl.BlockSpec((tb, d), lambda i: (i, 0))],
        out_shape=[jax.ShapeDtypeStruct((t, d), F32), jax.ShapeDtypeStruct((t, d), out_dtype)],
        compiler_params=_cparams(("parallel",)), name="res_rmsnorm",
    )(x, y, g.reshape(1, d).astype(F32))


def _matmul_kernel(*refs, n_pairs, has_res):
    a_refs, w_refs = refs[:n_pairs], refs[n_pairs:2 * n_pairs]
    o_ref = refs[-1]
    acc = _mm(a_refs[0][...], w_refs[0][...])
    for a_ref, w_ref in zip(a_refs[1:], w_refs[1:]):
        acc = acc + _mm(a_ref[...], w_ref[...])
    if has_res:
        acc = acc + refs[2 * n_pairs][...]
    o_ref[...] = acc


def matmul_call(a_list, w, layer, res=None, tm=512, tn=512, name="matmul"):
    t = a_list[0].shape[0]
    n = w.shape[2]
    in_specs = [pl.BlockSpec((tm, a.shape[1]), lambda i, j: (i, 0)) for a in a_list]
    row = 0
    for a in a_list:
        k = a.shape[1]
        in_specs.append(pl.BlockSpec((None, k, tn), functools.partial(lambda i, j, rb: (layer, rb, j), rb=row // k)))
        row += k
    args = list(a_list) + [w] * len(a_list)
    if res is not None:
        in_specs.append(pl.BlockSpec((tm, tn), lambda i, j: (i, j)))
        args.append(res)
    return pl.pallas_call(
        functools.partial(_matmul_kernel, n_pairs=len(a_list), has_res=res is not None),
        grid=(t // tm, n // tn), in_specs=in_specs,
        out_specs=pl.BlockSpec((tm, tn), lambda i, j: (i, j)),
        out_shape=jax.ShapeDtypeStruct((t, n), F32),
        compiler_params=_cparams(("parallel", "parallel")), name=name,
    )(*args)


IN_TN = 512
A_BLOCKS = B_OFF // IN_TN
G_BLOCK = G_OFF // IN_TN
B_SHIFT = 2 * H_A
C_SHIFT = 2 * H_A + 2 * H_B
W_ROWS = 512


def _in_proj_kernel(a_ref, wm_ref, we_ref, o_ref, wb_ref):
    j, i = pl.program_id(0), pl.program_id(1)
    k = wm_ref.shape[0]

    def convert(shift):
        def body(r, carry):
            rs = pl.ds(pl.multiple_of(r * W_ROWS, W_ROWS), W_ROWS)
            if shift == 0:
                blk = wm_ref[rs, :]
            else:
                blk = jnp.concatenate([wm_ref[rs, shift:], we_ref[rs, :shift]], axis=1)
            wb_ref[rs, :] = blk.astype(BF16)
            return carry
        lax.fori_loop(0, k // W_ROWS, body, 0)

    first = i == 0

    @pl.when(first & (j < A_BLOCKS))
    def _():
        convert(0)

    @pl.when(first & (j >= A_BLOCKS) & (j < G_BLOCK))
    def _():
        convert(B_SHIFT)

    @pl.when(first & (j == G_BLOCK))
    def _():
        wb_ref[:, 0:LANE] = wm_ref[:, 0:LANE].astype(BF16)
        wb_ref[:, LANE:2 * LANE] = we_ref[...].astype(BF16)
        wb_ref[:, 2 * LANE:] = jnp.zeros((k, IN_TN - 2 * LANE), BF16)

    @pl.when(first & (j > G_BLOCK))
    def _():
        convert(C_SHIFT)

    o_ref[...] = jnp.dot(a_ref[...], wb_ref[...], preferred_element_type=F32)


def in_proj_call(a, w_in, layer, tm):
    t, k = a.shape
    lanes_per_tile = IN_TN // LANE
    main_blk = lambda j: jnp.where(j <= G_BLOCK, j, j - 1)
    extra_blk = lambda j: jnp.where(j < G_BLOCK, lanes_per_tile * (j + 1),
                                    jnp.where(j == G_BLOCK, B_OFF // LANE, lanes_per_tile * j))
    return pl.pallas_call(
        _in_proj_kernel, grid=(NP // IN_TN, t // tm),
        in_specs=[pl.BlockSpec((tm, k), lambda j, i: (i, 0)),
                  pl.BlockSpec((None, k, IN_TN), lambda j, i: (layer, 0, main_blk(j))),
                  pl.BlockSpec((None, k, LANE), lambda j, i: (layer, 0, extra_blk(j)))],
        out_specs=pl.BlockSpec((tm, IN_TN), lambda j, i: (i, j)),
        out_shape=jax.ShapeDtypeStruct((t, NP), F32),
        scratch_shapes=[pltpu.VMEM((k, IN_TN), BF16)],
        compiler_params=_cparams(("parallel", "arbitrary")), name="in_proj",
    )(a, w_in, w_in)


def _tri_masks(c):
    row = lax.broadcasted_iota(jnp.int32, (c, c), 0)
    col = lax.broadcasted_iota(jnp.int32, (c, c), 1)
    return row, col


def _mlstm_kernel(qk_ref, v_ref, o_ref, gc_ref, gt_ref, brow_ref, bcol_ref, ng_ref, c0_ref, n0_ref, m0_ref,
                  out_ref, c_ref, n_ref, m_ref, *, c, nseq, n_valid, has_init):
    ci = pl.program_id(1)

    @pl.when(ci == 0)
    def _():
        if has_init:
            c_ref[...] = c0_ref[...]
            n_ref[...] = n0_ref[...]
            m_ref[...] = m0_ref[...]
        else:
            c_ref[...] = jnp.zeros_like(c_ref)
            n_ref[...] = jnp.zeros_like(n_ref)
            m_ref[...] = jnp.zeros_like(m_ref)

    gc = gc_ref[...] + brow_ref[...]
    gt = gt_ref[...] + bcol_ref[...]
    row, col = _tri_masks(c)
    tri = row >= col
    padded = n_valid < c
    if padded:
        valid_c = lax.broadcasted_iota(jnp.int32, (c, 1), 0) < n_valid
        valid_r = lax.broadcasted_iota(jnp.int32, (1, c), 1) < n_valid
    sh = [(s, h) for s in range(nseq) for h in range(H_A)]
    rows = lambda s: slice(s * c, (s + 1) * c)
    heads_c = lambda x, off: jnp.stack([x[rows(s), off + h:off + h + 1] for s, h in sh], axis=0)
    heads_r = lambda x, off: jnp.stack([x[s, off + h:off + h + 1, :] for s, h in sh], axis=0)
    ig_c, lf_c = heads_c(gc, 0), heads_c(_log_sigmoid(gc), H_A)
    ig_r, lf_r = heads_r(gt, 0), heads_r(_log_sigmoid(gt), H_A)
    if padded:
        ig_c, lf_c = jnp.where(valid_c, ig_c, NEG_INF), jnp.where(valid_c, lf_c, 0.0)
        ig_r, lf_r = jnp.where(valid_r, ig_r, NEG_INF), jnp.where(valid_r, lf_r, 0.0)
    q = jnp.stack([qk_ref[rows(s), h * DK_A:(h + 1) * DK_A] for s, h in sh], axis=0) * (DK_A ** -0.5)
    k = jnp.stack([qk_ref[rows(s), (H_A + h) * DK_A:(H_A + h + 1) * DK_A] for s, h in sh], axis=0)
    v = jnp.stack([v_ref[rows(s), h * DV_A:(h + 1) * DV_A] for s, h in sh], axis=0)
    cst = c_ref[...].reshape(nseq * H_A, DK_A, DV_A)
    nst = jnp.stack([n_ref[s, h:h + 1, :] for s, h in sh], axis=0)
    m_prev = jnp.stack([m_ref[s, :, h:h + 1] for s, h in sh], axis=0)
    b_c = jnp.sum(jnp.where(tri, lf_r, 0.0), axis=2, keepdims=True)
    b_r = jnp.sum(jnp.where(row <= col, lf_c, 0.0), axis=1, keepdims=True)
    log_d = jnp.where(tri, b_c - b_r + ig_r, NEG_INF)
    m_inter = b_c + m_prev
    m_t = jnp.maximum(m_inter, jnp.max(log_d, axis=2, keepdims=True))
    s = _bmm_nt(q, k) * jnp.exp(log_d - m_t)
    inter = jnp.exp(m_inter - m_t)
    num = _bmm(s, v) + inter * _bmm(q, cst)
    den = jnp.sum(s, axis=2, keepdims=True) + inter * jnp.sum(q * nst, axis=2, keepdims=True)
    hh = num / jnp.maximum(jnp.abs(den), jnp.exp(-m_t))
    m_new = m_t[:, c - 1:c, :]
    w = jnp.exp(b_c[:, c - 1:c, :] - b_c + ig_c - m_new)
    decay = inter[:, c - 1:c, :]
    kw = k * w
    c_ref[...] = (decay * cst + _bmm_tn(kw, v)).reshape(nseq, H_A, DK_A, DV_A)
    n_new = decay * nst + jnp.sum(kw, axis=1, keepdims=True)
    for i, (s, h) in enumerate(sh):
        hs = slice(h * DV_A, (h + 1) * DV_A)
        n_ref[s, h:h + 1, :] = n_new[i]
        m_ref[s, :, h:h + 1] = m_new[i]
        out_ref[rows(s), hs] = _head_norm(hh[i], ng_ref[:, hs]) * _sigmoid(o_ref[rows(s), hs])


def _mixer_call(body, name, proj, seqs, layer, depth, tok_blocks, extra_inputs, init_states, st_shapes,
                mix_w, prev_mix, prev_states, scratch_shapes=()):
    row0, nb, nchunk, c, nseq = seqs["row0"], seqs["nb"], seqs["nchunk"], seqs["c"], seqs["nseq"]
    assert nseq == 1 or nchunk == 1
    rows = nseq * c
    rb = row0 // rows
    tok = lambda b, ci: rb + b * nchunk + ci
    st_spec = lambda shape, lyr: pl.BlockSpec(
        (None, nseq) + shape, functools.partial(lambda b, ci, n: (lyr, b) + (0,) * n, n=len(shape)))
    in_specs = [pl.BlockSpec((rows, w), functools.partial(lambda b, ci, cb: (tok(b, ci), cb), cb=cb))
                for w, cb in tok_blocks]
    args = [proj] * len(tok_blocks)
    for arr, spec in extra_inputs:
        in_specs.append(spec)
        args.append(arr)
    if init_states is None:
        for shape in st_shapes:
            in_specs.append(pl.BlockSpec((None, nseq) + shape, functools.partial(
                lambda b, ci, n: (0,) * (n + 2), n=len(shape))))
            args.append(jnp.zeros((1, nseq) + shape, F32))
    else:
        for shape, arr in zip(st_shapes, init_states):
            in_specs.append(st_spec(shape, layer))
            args.append(arr)
    n_in = len(args)
    out_specs = [pl.BlockSpec((rows, mix_w), lambda b, ci: (tok(b, ci), 0))]
    out_shape = [jax.ShapeDtypeStruct((proj.shape[0], mix_w), F32)]
    for shape in st_shapes:
        out_specs.append(st_spec(shape, layer))
        out_shape.append(jax.ShapeDtypeStruct((depth, nb) + shape, F32))
    prev = [prev_mix] + (list(prev_states) if prev_states is not None else [None] * len(st_shapes))
    aliases = {}
    for k, arr in enumerate(prev):
        if arr is not None:
            aliases[len(args)] = k
            in_specs.append(pl.BlockSpec(memory_space=pl.ANY))
            args.append(arr)
    n_alias = len(args) - n_in

    def kernel_fn(*refs):
        body(*refs[:n_in], *refs[n_in + n_alias:])

    return pl.pallas_call(
        kernel_fn, grid=(nb // nseq, nchunk), in_specs=in_specs, out_specs=out_specs, out_shape=out_shape,
        input_output_aliases=aliases, scratch_shapes=list(scratch_shapes),
        compiler_params=_cparams(("parallel", "arbitrary")), name=name)(*args)


def _const_spec(shape):
    return pl.BlockSpec(shape, functools.partial(lambda b, ci, n: (0,) * n, n=len(shape)))


def _gates_spec(seqs):
    nchunk, c, nseq = seqs["nchunk"], seqs["c"], seqs["nseq"]
    return pl.BlockSpec((nseq, GATE_ROWS, c), lambda b, ci: (b * nchunk + ci, 0, 0))


def mlstm_call(proj, gates_t, brow, bcol, norm_g, seqs, layer, depth, init_states, prev_mix, prev_states):
    body = functools.partial(_mlstm_kernel, c=seqs["c"], nseq=seqs["nseq"], n_valid=seqs["n_valid"],
                             has_init=init_states is not None)
    extra = [(gates_t, _gates_spec(seqs)), (brow, _const_spec((1, LANE))), (bcol, _const_spec((GATE_ROWS, 1))),
             (norm_g.reshape(1, A_W).astype(F32), _const_spec((1, A_W)))]
    return _mixer_call(body, "mlstm", proj, seqs, layer, depth,
                       [(A_W, 0), (A_W, 1), (A_W, 2), (LANE, G_OFF // LANE + 1)], extra, init_states,
                       [(H_A, DK_A, DV_A), (H_A, DK_A), (1, H_A)], A_W, prev_mix, prev_states)


def _bdot(a, b, dims):
    return lax.dot_general(a, b, (dims, ((0,), (0,))), preferred_element_type=F32)


def _bmm(a, b):
    return _bdot(a.astype(BF16), b.astype(BF16), ((2,), (1,)))


def _bmm_nt(a, b):
    return _bdot(a.astype(BF16), b.astype(BF16), ((2,), (2,)))


def _bmm_tn(a, b):
    return _bdot(jnp.swapaxes(a, 1, 2).astype(BF16), b.astype(BF16), ((2,), (1,)))


def _unit_lower_inverse(a, row, col, c):
    d = jnp.where(row == col, 1.0, 0.0)
    s = 1
    while s < c:
        blk = jnp.where(((row // (2 * s)) == (col // (2 * s))) & ((row % (2 * s)) >= s) & ((col % (2 * s)) < s),
                        a, 0.0)
        if s == 1:
            d = d - blk
        else:
            d = d - _bmm(_bmm(d, blk), d)
        s *= 2
    return d


def _gdn_kernel(qkv_ref, z_ref, gc_ref, gt_ref, brow_ref, bcol_ref, arow_ref, acol_ref, cw_ref, buf0_ref, ng_ref,
                s0_ref, out_ref, s_ref, xp_ref, *, c, nseq, n_valid, has_init):
    ci = pl.program_id(1)
    hk = H_B * DK_B
    rows = lambda s: slice(s * c, (s + 1) * c)

    @pl.when(ci == 0)
    def _():
        xp_ref[:, 0:8, :] = jnp.zeros((nseq, 8, xp_ref.shape[2]), F32)
        if has_init:
            s_ref[...] = s0_ref[...]
            xp_ref[:, 8 - (CONV_W - 1):8, :] = buf0_ref[...]
        else:
            s_ref[...] = jnp.zeros_like(s_ref)

    @pl.when(ci > 0)
    def _():
        xp_ref[:, 0:8, :] = xp_ref[:, c:c + 8, :]

    for s in range(nseq):
        xp_ref[s, 8:8 + c, :] = qkv_ref[rows(s), :]

    gc = gc_ref[...] + brow_ref[...]
    gt = gt_ref[...] + bcol_ref[...]
    row, col = _tri_masks(c)
    tri = row >= col
    strict = row > col
    padded = n_valid < c
    if padded:
        valid_c = lax.broadcasted_iota(jnp.int32, (c, 1), 0) < n_valid
        valid_r = lax.broadcasted_iota(jnp.int32, (1, c), 1) < n_valid
    ga, gb = 2 * H_A, 2 * H_A + H_B

    sh = [(s, h) for s in range(nseq) for h in range(H_B)]

    def conv_act(c0):
        slabs = []
        for s, h in sh:
            cs = slice(c0 + h * DK_B, c0 + (h + 1) * DK_B)
            y = cw_ref[0:1, cs] * xp_ref[s, 8 - 3:8 - 3 + c, cs]
            for j in range(1, CONV_W):
                y = y + cw_ref[j:j + 1, cs] * xp_ref[s, 8 - 3 + j:8 - 3 + j + c, cs]
            slabs.append(_silu(y))
        return jnp.stack(slabs, axis=0)

    heads_c = lambda x, off: jnp.stack([x[rows(s), off + h:off + h + 1] for s, h in sh], axis=0)
    heads_r = lambda x, off: jnp.stack([x[s, off + h:off + h + 1, :] for s, h in sh], axis=0)
    g_c = heads_c(-jnp.exp(arow_ref[...]) * _softplus(gc), ga)
    g_r = heads_r(-jnp.exp(acol_ref[...]) * _softplus(gt), ga)
    beta = heads_c(_sigmoid(gc), gb)
    if padded:
        g_c, g_r = jnp.where(valid_c, g_c, 0.0), jnp.where(valid_r, g_r, 0.0)
        beta = jnp.where(valid_c, beta, 0.0)
    q = conv_act(0)
    k = conv_act(hk)
    v = conv_act(2 * hk)
    q = q * lax.rsqrt(jnp.sum(q * q, axis=2, keepdims=True) + EPS) * (DK_B ** -0.5)
    k = k * lax.rsqrt(jnp.sum(k * k, axis=2, keepdims=True) + EPS)
    sst = s_ref[...].reshape(nseq * H_B, DK_B, DV_B)
    gcum_c = jnp.sum(jnp.where(tri, g_r, 0.0), axis=2, keepdims=True)
    gcum_r = jnp.sum(jnp.where(row <= col, g_c, 0.0), axis=1, keepdims=True)
    gam = jnp.exp(jnp.where(tri, gcum_c - gcum_r, NEG_INF))
    eg = jnp.exp(gcum_c)
    a = jnp.where(strict, beta * _bmm_nt(k, k) * gam, 0.0)
    tinv = _unit_lower_inverse(a, row, col, c)
    rhs = jnp.concatenate([beta * v, (beta * eg) * k], axis=2)
    sol = _bmm(tinv, rhs)
    u = sol[:, :, :DV_B] - _bmm(sol[:, :, DV_B:], sst)
    o = eg * _bmm(q, sst) + _bmm(_bmm_nt(q, k) * gam, u)
    g_last = gcum_c[:, c - 1:c, :]
    s_new = jnp.exp(g_last) * sst + _bmm_tn(k * jnp.exp(g_last - gcum_c), u)
    s_ref[...] = s_new.reshape(nseq, H_B, DK_B, DV_B)
    for i, (s, h) in enumerate(sh):
        hs = slice(h * DV_B, (h + 1) * DV_B)
        out_ref[rows(s), hs] = _head_norm(o[i], ng_ref[:, hs]) * _silu(z_ref[rows(s), hs])


def gdn_call(proj, gates_t, brow, bcol, arow, acol, conv_w, norm_g, seqs, layer, depth, init_states, buf0,
             prev_mix, prev_states):
    wb = 3 * H_B * DK_B
    has_init = init_states is not None
    nseq = seqs["nseq"]
    body = functools.partial(_gdn_kernel, c=seqs["c"], nseq=nseq, n_valid=seqs["n_valid"], has_init=has_init)
    if has_init:
        buf_spec = pl.BlockSpec((None, nseq, CONV_W - 1, wb), lambda b, ci: (layer, b, 0, 0))
    else:
        buf0 = jnp.zeros((1, nseq, CONV_W - 1, wb), F32)
        buf_spec = pl.BlockSpec((None, nseq, CONV_W - 1, wb), lambda b, ci: (0, 0, 0, 0))
    extra = [(gates_t, _gates_spec(seqs)), (brow, _const_spec((1, LANE))), (bcol, _const_spec((GATE_ROWS, 1))),
             (arow, _const_spec((1, LANE))), (acol, _const_spec((GATE_ROWS, 1))),
             (conv_w.astype(F32), _const_spec((CONV_W, wb))), (buf0, buf_spec),
             (norm_g.reshape(1, H_B * DV_B).astype(F32), _const_spec((1, H_B * DV_B)))]
    return _mixer_call(body, "gdn", proj, seqs, layer, depth,
                       [(wb, B_OFF // wb), (A_W, (B_OFF + wb) // A_W), (LANE, G_OFF // LANE)], extra, init_states,
                       [(H_B, DK_B, DV_B)], H_B * DV_B, prev_mix, prev_states,
                       scratch_shapes=[pltpu.VMEM((nseq, seqs["c"] + 8, wb), F32)])


def _hgrn_kernel(q_ref, f_ref, i_ref, g_ref, lb_ref, ng_ref, s0_ref, out_ref, s_ref, st_ref, *,
                 c, sub, nseq, n_valid, has_init, nchunk):
    ci = pl.program_id(1)

    sh = [(s, h) for s in range(nseq) for h in range(H_C)]

    @pl.when(ci == 0)
    def _():
        for i, (s, h) in enumerate(sh):
            st_ref[i] = s0_ref[s, h].T if has_init else jnp.zeros((DV_C, DK_C), F32)

    padded = n_valid < sub
    rowi = lax.broadcasted_iota(jnp.int32, (1, sub, 1), 1)
    heads = lambda ref, j: jnp.stack(
        [ref[s * c + j * sub:s * c + (j + 1) * sub, h * DK_C:(h + 1) * DK_C] for s, h in sh], axis=0)
    lb = jnp.stack([lb_ref[:, h * DK_C:(h + 1) * DK_C] for s, h in sh], axis=0)
    log_1mlb, log_lb = jnp.log1p(-lb), jnp.log(lb)
    sst = st_ref[...]
    for j in range(c // sub):
        cf = heads(f_ref, j)
        la = log_1mlb + _log_sigmoid(cf)
        logf = jnp.maximum(la, log_lb) + jnp.log1p(jnp.exp(-jnp.abs(la - log_lb)))
        k = (1.0 - lb) * _sigmoid(-cf)
        if padded:
            logf, k = jnp.where(rowi < n_valid, logf, 0.0), jnp.where(rowi < n_valid, k, 0.0)
        q = _silu(heads(q_ref, j))
        iv = heads(i_ref, j)
        bc = jnp.zeros((nseq * H_C, sub, DK_C), F32)
        for t in range(sub):
            bc = bc + jnp.where(rowi >= t, logf[:, t:t + 1, :], 0.0)
        o = _bmm_nt(q * jnp.exp(bc), sst)
        for t in range(sub):
            dec = jnp.exp(jnp.where(rowi >= t, bc - bc[:, t:t + 1, :], NEG_INF))
            att = jnp.sum(q * dec * k[:, t:t + 1, :], axis=2, keepdims=True)
            o = o + att * iv[:, t:t + 1, :]
        b_last = bc[:, sub - 1:sub, :]
        sst = jnp.exp(b_last) * sst + _bmm_tn(iv, k * jnp.exp(b_last - bc))
        for i, (s, h) in enumerate(sh):
            rows = slice(s * c + j * sub, s * c + (j + 1) * sub)
            hs = slice(h * DK_C, (h + 1) * DK_C)
            out_ref[rows, hs] = _head_norm(o[i], ng_ref[:, hs]) * _silu(g_ref[rows, hs])
    st_ref[...] = sst

    @pl.when(ci == nchunk - 1)
    def _():
        for i, (s, h) in enumerate(sh):
            s_ref[s, h] = st_ref[i].T


def hgrn_call(proj, lb, norm_g, seqs, sub, layer, depth, init_states, prev_mix, prev_states):
    w = H_C * DK_C
    cb = C_OFF // w
    body = functools.partial(_hgrn_kernel, c=seqs["c"], sub=sub, nseq=seqs["nseq"],
                             n_valid=min(seqs["n_valid"], sub),
                             has_init=init_states is not None, nchunk=seqs["nchunk"])
    extra = [(lb.reshape(1, w).astype(F32), _const_spec((1, w))),
             (norm_g.reshape(1, w).astype(F32), _const_spec((1, w)))]
    return _mixer_call(body, "hgrn", proj, seqs, layer, depth,
                       [(w, cb), (w, cb + 1), (w, cb + 2), (w, cb + 3)], extra, init_states,
                       [(H_C, DK_C, DV_C)], w, prev_mix, prev_states,
                       scratch_shapes=[pltpu.VMEM((seqs["nseq"] * H_C, DV_C, DK_C), F32)])


def _top_values(x, k):
    rows = []
    for _ in range(k):
        mx = jnp.max(x, axis=0, keepdims=True)
        rows.append(mx)
        x = jnp.where(x == mx, NEG_INF, x)
    return jnp.concatenate(rows, axis=0)


def _route_kernel(q_ref, sk_ref, s1_ref, s2_ref, e1_ref, e2_ref, tau_ref):
    for h in range(PEER_HEADS):
        s1 = _mmf_nt(sk_ref[h, 0], q_ref[:, (2 * h) * LANE:(2 * h + 1) * LANE])
        s2 = _mmf_nt(sk_ref[h, 1], q_ref[:, (2 * h + 1) * LANE:(2 * h + 2) * LANE])
        a = _top_values(s1, PEER_TOPK)
        b = _top_values(s2, PEER_TOPK)
        cand = jnp.concatenate([a[0:1, :] + b] + [a[i:i + 1, :] + b[:PEER_TOPK // 2, :]
                                                  for i in range(1, PEER_TOPK)], axis=0)
        best = _top_values(cand, PEER_TOPK)
        z = jnp.sum(jnp.exp(best - best[0:1, :]), axis=0, keepdims=True)
        s1_ref[h] = s1
        s2_ref[h] = s2
        e1_ref[h] = jnp.exp(s1 - a[0:1, :])
        e2_ref[h] = jnp.exp(s2 - b[0:1, :]) / z
        tau_ref[h:h + 1, :] = best[PEER_TOPK - 1:PEER_TOPK, :]


def route_call(qry, subkeys, tb=256):
    t = qry.shape[0]
    big = jax.ShapeDtypeStruct((PEER_HEADS, N_KEYS, t), F32)
    big_spec = pl.BlockSpec((PEER_HEADS, N_KEYS, tb), lambda i: (0, 0, i))
    return pl.pallas_call(
        _route_kernel, grid=(t // tb,),
        in_specs=[pl.BlockSpec((tb, qry.shape[1]), lambda i: (i, 0)),
                  pl.BlockSpec(subkeys.shape, lambda i: (0, 0, 0, 0))],
        out_specs=[big_spec] * 4 + [pl.BlockSpec((PEER_HEADS, tb), lambda i: (0, i))],
        out_shape=[big] * 4 + [jax.ShapeDtypeStruct((PEER_HEADS, t), F32)],
        compiler_params=_cparams(("parallel",)), name="peer_route",
    )(qry, subkeys.astype(F32))


PEER_SUB = 256
PEER_LANES = 256


def _peer_kernel(t_ref, u_ref, v_ref, s1_ref, s2_ref, e1_ref, e2_ref, tau_ref, y_ref, *, eb):
    j = pl.program_id(1)
    tm = t_ref.shape[0]

    @pl.when(j == 0)
    def _():
        y_ref[...] = jnp.zeros_like(y_ref)

    for p in range(eb // PEER_SUB):
        es = slice(p * PEER_SUB, (p + 1) * PEER_SUB)
        a_t = lax.dot_general(u_ref[es, :], t_ref[...], (((1,), (1,)), ((), ())), preferred_element_type=F32)
        rows = []
        for r in range(PEER_SUB // N_KEYS):
            i1 = j * (eb // N_KEYS) + p * (PEER_SUB // N_KEYS) + r
            s1_rows = [s1_ref[h, pl.ds(i1, 1), :] for h in range(PEER_HEADS)]
            e1_rows = [e1_ref[h, pl.ds(i1, 1), :] for h in range(PEER_HEADS)]
            tiles = []
            for tc in range(tm // PEER_LANES):
                ls = slice(tc * PEER_LANES, (tc + 1) * PEER_LANES)
                gsum = None
                for h in range(PEER_HEADS):
                    sc = s1_rows[h][:, ls] + s2_ref[h, :, ls]
                    gh = jnp.where(sc >= tau_ref[h:h + 1, ls], e1_rows[h][:, ls] * e2_ref[h, :, ls], 0.0)
                    gsum = gh if gsum is None else gsum + gh
                tiles.append(jax.nn.gelu(a_t[r * N_KEYS:(r + 1) * N_KEYS, ls]) * gsum)
            rows.append(jnp.concatenate(tiles, axis=1))
        coef = jnp.concatenate(rows, axis=0)
        y_ref[...] += jnp.dot(coef.T.astype(BF16), v_ref[es, :], preferred_element_type=F32)


def peer_call(t_bf, u_bf, v_bf, layer, s1, s2, e1, e2, tau, tm=512, eb=512):
    t, d = t_bf.shape
    ne = u_bf.shape[1]
    once = pl.Buffered(1)
    big_spec = pl.BlockSpec((PEER_HEADS, N_KEYS, tm), lambda i, j: (0, 0, i), pipeline_mode=once)
    return pl.pallas_call(
        functools.partial(_peer_kernel, eb=eb),
        grid=(t // tm, ne // eb),
        in_specs=[pl.BlockSpec((tm, d), lambda i, j: (i, 0), pipeline_mode=once),
                  pl.BlockSpec((None, eb, d), lambda i, j: (layer, j, 0)),
                  pl.BlockSpec((None, eb, d), lambda i, j: (layer, j, 0)),
                  big_spec, big_spec, big_spec, big_spec,
                  pl.BlockSpec((PEER_HEADS, tm), lambda i, j: (0, i), pipeline_mode=once)],
        out_specs=pl.BlockSpec((tm, d), lambda i, j: (i, 0)),
        out_shape=jax.ShapeDtypeStruct((t, d), F32),
        compiler_params=_cparams(("parallel", "arbitrary")), name="peer_dense",
    )(t_bf, u_bf, v_bf, s1, s2, e1, e2, tau)


def _gate_params(b_i, b_f, a_log, dt_bias):
    zeros = jnp.zeros((LANE - 2 * H_A - 2 * H_B,), F32)
    bias = jnp.concatenate([b_i.astype(F32), b_f.astype(F32), dt_bias.astype(F32), jnp.zeros((H_B,), F32), zeros])
    alog = jnp.concatenate([jnp.zeros((2 * H_A,), F32), a_log.astype(F32), jnp.zeros((H_B,), F32), zeros])
    return (bias.reshape(1, LANE), bias[:GATE_ROWS].reshape(GATE_ROWS, 1),
            alog.reshape(1, LANE), alog[:GATE_ROWS].reshape(GATE_ROWS, 1))


def _gates_rows(proj, c):
    na, nb = 2 * H_A, 2 * H_B
    g = jnp.concatenate([proj[:, G_OFF + LANE:G_OFF + LANE + na], proj[:, G_OFF + na:G_OFF + na + nb],
                         jnp.zeros((proj.shape[0], GATE_ROWS - na - nb), F32)], axis=1)
    return g.reshape(g.shape[0] // c, c, GATE_ROWS).transpose(0, 2, 1)


def _hgrn_lower_bounds(logits):
    p = jax.nn.softmax(logits.astype(F32), axis=0)
    cs = jnp.cumsum(p, axis=0)
    return cs - cs[0:1]


def kernel(x_prompt, x_sample, state_mlstm_C, state_mlstm_n, state_mlstm_m, state_gdn_S, state_gdn_conv,
           state_hgrn_S, norm_mix_g, w_in, mlstm_b_i, mlstm_b_f, mlstm_norm_g, gdn_conv_w, gdn_A_log,
           gdn_dt_bias, gdn_norm_g, hgrn_lb_logits, hgrn_norm_g, w_out, norm_ffn_g, peer_w_q, peer_subkeys,
           peer_u, peer_v, final_norm_g, *, chunk=64, hgrn_chunk=16, tm=512, tm_peer=512, eb_peer=512, tb=256):
    bp, lp, d = x_prompt.shape
    bs, ls, _ = x_sample.shape
    depth = w_in.shape[0]
    tp = bp * lp
    ts = bs * ls
    x = jnp.concatenate([x_prompt.reshape(tp, d), x_sample.reshape(ts, d)], axis=0).astype(F32)
    tm_in = next(c for c in (1024, 1088, tm) if (tp + ts) % c == 0)
    lbs = _hgrn_lower_bounds(hgrn_lb_logits)
    wb = 3 * H_B * DK_B
    u_all, v_all = peer_u.astype(BF16), peer_v.astype(BF16)
    wo_all, wq_all = w_out.astype(BF16), peer_w_q.astype(BF16)
    pseq = dict(row0=0, nb=bp, nchunk=lp // chunk, c=chunk, n_valid=chunk, nseq=1)
    sseq = dict(row0=0, nb=bs, nchunk=1, c=SAMPLE_PAD, n_valid=ls, nseq=SAMPLE_SEQS)
    a_init = [state_mlstm_C, state_mlstm_n, state_mlstm_m.reshape(depth, bs, 1, H_A)]
    st_ap = st_as = st_bp = st_bs = st_cp = st_cs = None
    bufs_p, bufs_s = [], []
    y_peer = None
    for l in range(depth):
        if l == 0:
            h = rmsnorm_call(x, norm_mix_g[l], BF16, tb=tb)
        else:
            x, h = res_rmsnorm_call(x, y_peer, norm_mix_g[l], BF16, tb=tb)
        proj = in_proj_call(h, w_in, l, tm=tm_in)
        proj_s = jnp.pad(proj[tp:].reshape(bs, ls, NP), ((0, 0), (0, SAMPLE_PAD - ls), (0, 0)))
        proj_s = proj_s.reshape(bs * SAMPLE_PAD, NP)
        brow, bcol, arow, acol = _gate_params(mlstm_b_i[l], mlstm_b_f[l], gdn_A_log[l], gdn_dt_bias[l])
        gt_p = _gates_rows(proj[:tp], chunk)
        gt_s = _gates_rows(proj_s, SAMPLE_PAD)
        mix_a, *st_ap = mlstm_call(proj, gt_p, brow, bcol, mlstm_norm_g[l], pseq, l, depth, None, None, st_ap)
        smp_a, *st_as = mlstm_call(proj_s, gt_s, brow, bcol, mlstm_norm_g[l], sseq, l, depth, a_init, None, st_as)
        gdn_w = (brow, bcol, arow, acol, gdn_conv_w[l], gdn_norm_g[l])
        mix_b, *st_bp = gdn_call(proj, gt_p, *gdn_w, pseq, l, depth, None, None, None, st_bp)
        smp_b, *st_bs = gdn_call(proj_s, gt_s, *gdn_w, sseq, l, depth, [state_gdn_S], state_gdn_conv, None, st_bs)
        mix_c, *st_cp = hgrn_call(proj, lbs[l], hgrn_norm_g[l], pseq, hgrn_chunk, l, depth, None, None, st_cp)
        smp_c, *st_cs = hgrn_call(proj_s, lbs[l], hgrn_norm_g[l], sseq, SAMPLE_PAD, l, depth, [state_hgrn_S],
                                  None, st_cs)
        valid = lambda m: m.reshape(bs, SAMPLE_PAD, m.shape[1])[:, :ls].reshape(ts, m.shape[1])
        mix_a, mix_b, mix_c = (m.at[tp:].set(valid(s)) for m, s in ((mix_a, smp_a), (mix_b, smp_b), (mix_c, smp_c)))
        bufs_p.append(proj[:tp, B_OFF:B_OFF + wb].reshape(bp, lp, wb)[:, lp - (CONV_W - 1):])
        bufs_s.append(proj[tp:, B_OFF:B_OFF + wb].reshape(bs, ls, wb)[:, ls - (CONV_W - 1):])
        x = matmul_call([mix_a, mix_b, mix_c], wo_all, l, res=x, tm=tm, name="out_proj")
        t_bf = rmsnorm_call(x, norm_ffn_g[l], BF16, tb=tb)
        qry = matmul_call([t_bf], wq_all, l, tm=tm, name="peer_query")
        s1, s2, e1, e2, tau = route_call(qry, peer_subkeys[l], tb=tb)
        y_peer = peer_call(t_bf, u_all, v_all, l, s1, s2, e1, e2, tau, tm=tm_peer, eb=eb_peer)
    _, y = res_rmsnorm_call(x, y_peer, final_norm_g, F32, tb=tb)
    y_prompt = y[:tp].reshape(bp, lp, d)
    y_sample = y[tp:].reshape(bs, ls, d)
    m_p, m_s = st_ap[2].reshape(depth, bp, H_A), st_as[2].reshape(depth, bs, H_A)
    return (y_prompt, y_sample, st_ap[0], st_as[0], st_ap[1], st_as[1], m_p, m_s, st_bp[0], st_bs[0],
            jnp.stack(bufs_p, axis=0), jnp.stack(bufs_s, axis=0), st_cp[0], st_cs[0])
```

```python
import functools

import jax
import jax.numpy as jnp
from jax import lax
from jax.experimental import pallas as pl
from jax.experimental.pallas import tpu as pltpu

F32 = jnp.float32
BF16 = jnp.bfloat16
EPS = 1e-6
NEG_INF = float("-inf")

H_A, DK_A, DV_A = 6, 128, 256
H_B, DK_B, DV_B = 12, 128, 128
H_C, DK_C, DV_C = 8, 128, 128
CONV_W = 4
PEER_HEADS, N_KEYS, PEER_TOPK = 8, 128, 16
SAMPLE_PAD = 8
SAMPLE_SEQS = 4
LANE = 128
VMEM_LIMIT = 56 * 1024 * 1024

A_W = H_A * DV_A
B_OFF = 3 * A_W
G_OFF = B_OFF + 4 * H_B * DV_B
C_OFF = G_OFF + 512
NP = C_OFF + 4 * H_C * DV_C
GATE_ROWS = 40


def _cparams(sem):
    return pltpu.CompilerParams(dimension_semantics=sem, vmem_limit_bytes=VMEM_LIMIT)


def _mm(a, b):
    return jnp.dot(a.astype(BF16), b.astype(BF16), preferred_element_type=F32)


def _mmf_nt(a, b):
    return lax.dot_general(a, b, (((1,), (1,)), ((), ())), preferred_element_type=F32,
                           precision=lax.Precision.HIGHEST)


def _softplus(x):
    return jnp.maximum(x, 0.0) + jnp.log1p(jnp.exp(-jnp.abs(x)))


def _log_sigmoid(x):
    return -_softplus(-x)


def _sigmoid(x):
    return 1.0 / (1.0 + jnp.exp(-x))


def _silu(x):
    return x * _sigmoid(x)


def _head_norm(x, g_row):
    return x * lax.rsqrt(jnp.mean(x * x, axis=-1, keepdims=True) + EPS) * g_row


def _rms_kernel(x_ref, g_ref, h_ref):
    x = x_ref[...]
    y = x * lax.rsqrt(jnp.mean(x * x, axis=-1, keepdims=True) + EPS)
    h_ref[...] = (y * g_ref[...]).astype(h_ref.dtype)


def rmsnorm_call(x, g, out_dtype, tb=256, row0=0, rows=None):
    d = x.shape[1]
    rows = x.shape[0] if rows is None else rows
    rb = row0 // tb
    return pl.pallas_call(
        _rms_kernel, grid=(rows // tb,),
        in_specs=[pl.BlockSpec((tb, d), lambda i: (rb + i, 0)), pl.BlockSpec((1, d), lambda i: (0, 0))],
        out_specs=pl.BlockSpec((tb, d), lambda i: (i, 0)),
        out_shape=jax.ShapeDtypeStruct((rows, d), out_dtype),
        compiler_params=_cparams(("parallel",)), name="rmsnorm",
    )(x, g.reshape(1, d).astype(F32))


def _matmul_kernel(*refs, n_pairs, has_res):
    a_refs, w_refs = refs[:n_pairs], refs[n_pairs:2 * n_pairs]
    o_ref = refs[-1]
    acc = _mm(a_refs[0][...], w_refs[0][...])
    for a_ref, w_ref in zip(a_refs[1:], w_refs[1:]):
        acc = acc + _mm(a_ref[...], w_ref[...])
    if has_res:
        acc = acc + refs[2 * n_pairs][...]
    o_ref[...] = acc


def matmul_call(a_list, w, layer, res=None, tm=512, tn=512, name="matmul"):
    t = a_list[0].shape[0]
    n = w.shape[2]
    in_specs = [pl.BlockSpec((tm, a.shape[1]), lambda i, j: (i, 0)) for a in a_list]
    row = 0
    for a in a_list:
        k = a.shape[1]
        in_specs.append(pl.BlockSpec((None, k, tn), functools.partial(lambda i, j, rb: (layer, rb, j), rb=row // k)))
        row += k
    args = list(a_list) + [w] * len(a_list)
    if res is not None:
        in_specs.append(pl.BlockSpec((tm, tn), lambda i, j: (i, j)))
        args.append(res)
    return pl.pallas_call(
        functools.partial(_matmul_kernel, n_pairs=len(a_list), has_res=res is not None),
        grid=(t // tm, n // tn), in_specs=in_specs,
        out_specs=pl.BlockSpec((tm, tn), lambda i, j: (i, j)),
        out_shape=jax.ShapeDtypeStruct((t, n), F32),
        compiler_params=_cparams(("parallel", "parallel")), name=name,
    )(*args)


IN_TN = 512
A_BLOCKS = B_OFF // IN_TN
G_BLOCK = G_OFF // IN_TN
B_SHIFT = 2 * H_A
C_SHIFT = 2 * H_A + 2 * H_B
W_ROWS = 512


def _in_proj_kernel(a_ref, wm_ref, we_ref, o_ref, wb_ref):
    j, i = pl.program_id(0), pl.program_id(1)
    k = wm_ref.shape[0]

    def convert(shift):
        def body(r, carry):
            rs = pl.ds(pl.multiple_of(r * W_ROWS, W_ROWS), W_ROWS)
            if shift == 0:
                blk = wm_ref[rs, :]
            else:
                blk = jnp.concatenate([wm_ref[rs, shift:], we_ref[rs, :shift]], axis=1)
            wb_ref[rs, :] = blk.astype(BF16)
            return carry
        lax.fori_loop(0, k // W_ROWS, body, 0)

    first = i == 0

    @pl.when(first & (j < A_BLOCKS))
    def _():
        convert(0)

    @pl.when(first & (j >= A_BLOCKS) & (j < G_BLOCK))
    def _():
        convert(B_SHIFT)

    @pl.when(first & (j == G_BLOCK))
    def _():
        wb_ref[:, 0:LANE] = wm_ref[:, 0:LANE].astype(BF16)
        wb_ref[:, LANE:2 * LANE] = we_ref[...].astype(BF16)
        wb_ref[:, 2 * LANE:] = jnp.zeros((k, IN_TN - 2 * LANE), BF16)

    @pl.when(first & (j > G_BLOCK))
    def _():
        convert(C_SHIFT)

    o_ref[...] = jnp.dot(a_ref[...], wb_ref[...], preferred_element_type=F32)


def in_proj_call(a, w_in, layer, tm):
    t, k = a.shape
    lanes_per_tile = IN_TN // LANE
    main_blk = lambda j: jnp.where(j <= G_BLOCK, j, j - 1)
    extra_blk = lambda j: jnp.where(j < G_BLOCK, lanes_per_tile * (j + 1),
                                    jnp.where(j == G_BLOCK, B_OFF // LANE, lanes_per_tile * j))
    return pl.pallas_call(
        _in_proj_kernel, grid=(NP // IN_TN, t // tm),
        in_specs=[pl.BlockSpec((tm, k), lambda j, i: (i, 0)),
                  pl.BlockSpec((None, k, IN_TN), lambda j, i: (layer, 0, main_blk(j))),
                  pl.BlockSpec((None, k, LANE), lambda j, i: (layer, 0, extra_blk(j)))],
        out_specs=pl.BlockSpec((tm, IN_TN), lambda j, i: (i, j)),
        out_shape=jax.ShapeDtypeStruct((t, NP), F32),
        scratch_shapes=[pltpu.VMEM((k, IN_TN), BF16)],
        compiler_params=_cparams(("parallel", "arbitrary")), name="in_proj",
    )(a, w_in, w_in)


def _tri_masks(c):
    row = lax.broadcasted_iota(jnp.int32, (c, c), 0)
    col = lax.broadcasted_iota(jnp.int32, (c, c), 1)
    return row, col


def _mlstm_kernel(qk_ref, v_ref, o_ref, gc_ref, gt_ref, brow_ref, bcol_ref, ng_ref, c0_ref, n0_ref, m0_ref,
                  out_ref, c_ref, n_ref, m_ref, *, c, nseq, n_valid, has_init):
    ci = pl.program_id(1)

    @pl.when(ci == 0)
    def _():
        if has_init:
            c_ref[...] = c0_ref[...]
            n_ref[...] = n0_ref[...]
            m_ref[...] = m0_ref[...]
        else:
            c_ref[...] = jnp.zeros_like(c_ref)
            n_ref[...] = jnp.zeros_like(n_ref)
            m_ref[...] = jnp.zeros_like(m_ref)

    gc = gc_ref[...] + brow_ref[...]
    gt = gt_ref[...] + bcol_ref[...]
    row, col = _tri_masks(c)
    tri = row >= col
    padded = n_valid < c
    if padded:
        valid_c = lax.broadcasted_iota(jnp.int32, (c, 1), 0) < n_valid
        valid_r = lax.broadcasted_iota(jnp.int32, (1, c), 1) < n_valid
    sh = [(s, h) for s in range(nseq) for h in range(H_A)]
    rows = lambda s: slice(s * c, (s + 1) * c)
    heads_c = lambda x, off: jnp.stack([x[rows(s), off + h:off + h + 1] for s, h in sh], axis=0)
    heads_r = lambda x, off: jnp.stack([x[s, off + h:off + h + 1, :] for s, h in sh], axis=0)
    ig_c, lf_c = heads_c(gc, 0), heads_c(_log_sigmoid(gc), H_A)
    ig_r, lf_r = heads_r(gt, 0), heads_r(_log_sigmoid(gt), H_A)
    if padded:
        ig_c, lf_c = jnp.where(valid_c, ig_c, NEG_INF), jnp.where(valid_c, lf_c, 0.0)
        ig_r, lf_r = jnp.where(valid_r, ig_r, NEG_INF), jnp.where(valid_r, lf_r, 0.0)
    q = jnp.stack([qk_ref[rows(s), h * DK_A:(h + 1) * DK_A] for s, h in sh], axis=0) * (DK_A ** -0.5)
    k = jnp.stack([qk_ref[rows(s), (H_A + h) * DK_A:(H_A + h + 1) * DK_A] for s, h in sh], axis=0)
    v = jnp.stack([v_ref[rows(s), h * DV_A:(h + 1) * DV_A] for s, h in sh], axis=0)
    cst = c_ref[...].reshape(nseq * H_A, DK_A, DV_A)
    nst = jnp.stack([n_ref[s, h:h + 1, :] for s, h in sh], axis=0)
    m_prev = jnp.stack([m_ref[s, :, h:h + 1] for s, h in sh], axis=0)
    b_c = jnp.sum(jnp.where(tri, lf_r, 0.0), axis=2, keepdims=True)
    b_r = jnp.sum(jnp.where(row <= col, lf_c, 0.0), axis=1, keepdims=True)
    log_d = jnp.where(tri, b_c - b_r + ig_r, NEG_INF)
    m_inter = b_c + m_prev
    m_t = jnp.maximum(m_inter, jnp.max(log_d, axis=2, keepdims=True))
    s = _bmm_nt(q, k) * jnp.exp(log_d - m_t)
    inter = jnp.exp(m_inter - m_t)
    num = _bmm(s, v) + inter * _bmm(q, cst)
    den = jnp.sum(s, axis=2, keepdims=True) + inter * jnp.sum(q * nst, axis=2, keepdims=True)
    hh = num / jnp.maximum(jnp.abs(den), jnp.exp(-m_t))
    m_new = m_t[:, c - 1:c, :]
    w = jnp.exp(b_c[:, c - 1:c, :] - b_c + ig_c - m_new)
    decay = inter[:, c - 1:c, :]
    kw = k * w
    c_ref[...] = (decay * cst + _bmm_tn(kw, v)).reshape(nseq, H_A, DK_A, DV_A)
    n_new = decay * nst + jnp.sum(kw, axis=1, keepdims=True)
    for i, (s, h) in enumerate(sh):
        hs = slice(h * DV_A, (h + 1) * DV_A)
        n_ref[s, h:h + 1, :] = n_new[i]
        m_ref[s, :, h:h + 1] = m_new[i]
        out_ref[rows(s), hs] = _head_norm(hh[i], ng_ref[:, hs]) * _sigmoid(o_ref[rows(s), hs])


def _mixer_call(body, name, proj, seqs, layer, depth, tok_blocks, extra_inputs, init_states, st_shapes,
                mix_w, prev_mix, prev_states, scratch_shapes=()):
    row0, nb, nchunk, c, nseq = seqs["row0"], seqs["nb"], seqs["nchunk"], seqs["c"], seqs["nseq"]
    assert nseq == 1 or nchunk == 1
    rows = nseq * c
    rb = row0 // rows
    tok = lambda b, ci: rb + b * nchunk + ci
    st_spec = lambda shape, lyr: pl.BlockSpec(
        (None, nseq) + shape, functools.partial(lambda b, ci, n: (lyr, b) + (0,) * n, n=len(shape)))
    in_specs = [pl.BlockSpec((rows, w), functools.partial(lambda b, ci, cb: (tok(b, ci), cb), cb=cb))
                for w, cb in tok_blocks]
    args = [proj] * len(tok_blocks)
    for arr, spec in extra_inputs:
        in_specs.append(spec)
        args.append(arr)
    if init_states is None:
        for shape in st_shapes:
            in_specs.append(pl.BlockSpec((None, nseq) + shape, functools.partial(
                lambda b, ci, n: (0,) * (n + 2), n=len(shape))))
            args.append(jnp.zeros((1, nseq) + shape, F32))
    else:
        for shape, arr in zip(st_shapes, init_states):
            in_specs.append(st_spec(shape, layer))
            args.append(arr)
    n_in = len(args)
    out_specs = [pl.BlockSpec((rows, mix_w), lambda b, ci: (tok(b, ci), 0))]
    out_shape = [jax.ShapeDtypeStruct((proj.shape[0], mix_w), F32)]
    for shape in st_shapes:
        out_specs.append(st_spec(shape, layer))
        out_shape.append(jax.ShapeDtypeStruct((depth, nb) + shape, F32))
    prev = [prev_mix] + (list(prev_states) if prev_states is not None else [None] * len(st_shapes))
    aliases = {}
    for k, arr in enumerate(prev):
        if arr is not None:
            aliases[len(args)] = k
            in_specs.append(pl.BlockSpec(memory_space=pl.ANY))
            args.append(arr)
    n_alias = len(args) - n_in

    def kernel_fn(*refs):
        body(*refs[:n_in], *refs[n_in + n_alias:])

    return pl.pallas_call(
        kernel_fn, grid=(nb // nseq, nchunk), in_specs=in_specs, out_specs=out_specs, out_shape=out_shape,
        input_output_aliases=aliases, scratch_shapes=list(scratch_shapes),
        compiler_params=_cparams(("parallel", "arbitrary")), name=name)(*args)


def _const_spec(shape):
    return pl.BlockSpec(shape, functools.partial(lambda b, ci, n: (0,) * n, n=len(shape)))


def _gates_spec(seqs):
    nchunk, c, nseq = seqs["nchunk"], seqs["c"], seqs["nseq"]
    return pl.BlockSpec((nseq, GATE_ROWS, c), lambda b, ci: (b * nchunk + ci, 0, 0))


def mlstm_call(proj, gates_t, brow, bcol, norm_g, seqs, layer, depth, init_states, prev_mix, prev_states):
    body = functools.partial(_mlstm_kernel, c=seqs["c"], nseq=seqs["nseq"], n_valid=seqs["n_valid"],
                             has_init=init_states is not None)
    extra = [(gates_t, _gates_spec(seqs)), (brow, _const_spec((1, LANE))), (bcol, _const_spec((GATE_ROWS, 1))),
             (norm_g.reshape(1, A_W).astype(F32), _const_spec((1, A_W)))]
    return _mixer_call(body, "mlstm", proj, seqs, layer, depth,
                       [(A_W, 0), (A_W, 1), (A_W, 2), (LANE, G_OFF // LANE + 1)], extra, init_states,
                       [(H_A, DK_A, DV_A), (H_A, DK_A), (1, H_A)], A_W, prev_mix, prev_states)


def _bdot(a, b, dims):
    return lax.dot_general(a, b, (dims, ((0,), (0,))), preferred_element_type=F32)


def _bmm(a, b):
    return _bdot(a.astype(BF16), b.astype(BF16), ((2,), (1,)))


def _bmm_nt(a, b):
    return _bdot(a.astype(BF16), b.astype(BF16), ((2,), (2,)))


def _bmm_tn(a, b):
    return _bdot(jnp.swapaxes(a, 1, 2).astype(BF16), b.astype(BF16), ((2,), (1,)))


def _unit_lower_inverse(a, row, col, c):
    d = jnp.where(row == col, 1.0, 0.0)
    s = 1
    while s < c:
        blk = jnp.where(((row // (2 * s)) == (col // (2 * s))) & ((row % (2 * s)) >= s) & ((col % (2 * s)) < s),
                        a, 0.0)
        if s == 1:
            d = d - blk
        else:
            d = d - _bmm(_bmm(d, blk), d)
        s *= 2
    return d


def _gdn_kernel(qkv_ref, z_ref, gc_ref, gt_ref, brow_ref, bcol_ref, arow_ref, acol_ref, cw_ref, buf0_ref, ng_ref,
                s0_ref, out_ref, s_ref, xp_ref, *, c, nseq, n_valid, has_init):
    ci = pl.program_id(1)
    hk = H_B * DK_B
    rows = lambda s: slice(s * c, (s + 1) * c)

    @pl.when(ci == 0)
    def _():
        xp_ref[:, 0:8, :] = jnp.zeros((nseq, 8, xp_ref.shape[2]), F32)
        if has_init:
            s_ref[...] = s0_ref[...]
            xp_ref[:, 8 - (CONV_W - 1):8, :] = buf0_ref[...]
        else:
            s_ref[...] = jnp.zeros_like(s_ref)

    @pl.when(ci > 0)
    def _():
        xp_ref[:, 0:8, :] = xp_ref[:, c:c + 8, :]

    for s in range(nseq):
        xp_ref[s, 8:8 + c, :] = qkv_ref[rows(s), :]

    gc = gc_ref[...] + brow_ref[...]
    gt = gt_ref[...] + bcol_ref[...]
    row, col = _tri_masks(c)
    tri = row >= col
    strict = row > col
    padded = n_valid < c
    if padded:
        valid_c = lax.broadcasted_iota(jnp.int32, (c, 1), 0) < n_valid
        valid_r = lax.broadcasted_iota(jnp.int32, (1, c), 1) < n_valid
    ga, gb = 2 * H_A, 2 * H_A + H_B

    sh = [(s, h) for s in range(nseq) for h in range(H_B)]

    def conv_act(c0):
        slabs = []
        for s, h in sh:
            cs = slice(c0 + h * DK_B, c0 + (h + 1) * DK_B)
            y = cw_ref[0:1, cs] * xp_ref[s, 8 - 3:8 - 3 + c, cs]
            for j in range(1, CONV_W):
                y = y + cw_ref[j:j + 1, cs] * xp_ref[s, 8 - 3 + j:8 - 3 + j + c, cs]
            slabs.append(_silu(y))
        return jnp.stack(slabs, axis=0)

    heads_c = lambda x, off: jnp.stack([x[rows(s), off + h:off + h + 1] for s, h in sh], axis=0)
    heads_r = lambda x, off: jnp.stack([x[s, off + h:off + h + 1, :] for s, h in sh], axis=0)
    g_c = heads_c(-jnp.exp(arow_ref[...]) * _softplus(gc), ga)
    g_r = heads_r(-jnp.exp(acol_ref[...]) * _softplus(gt), ga)
    beta = heads_c(_sigmoid(gc), gb)
    if padded:
        g_c, g_r = jnp.where(valid_c, g_c, 0.0), jnp.where(valid_r, g_r, 0.0)
        beta = jnp.where(valid_c, beta, 0.0)
    q = conv_act(0)
    k = conv_act(hk)
    v = conv_act(2 * hk)
    q = q * lax.rsqrt(jnp.sum(q * q, axis=2, keepdims=True) + EPS) * (DK_B ** -0.5)
    k = k * lax.rsqrt(jnp.sum(k * k, axis=2, keepdims=True) + EPS)
    sst = s_ref[...].reshape(nseq * H_B, DK_B, DV_B)
    gcum_c = jnp.sum(jnp.where(tri, g_r, 0.0), axis=2, keepdims=True)
    gcum_r = jnp.sum(jnp.where(row <= col, g_c, 0.0), axis=1, keepdims=True)
    gam = jnp.exp(jnp.where(tri, gcum_c - gcum_r, NEG_INF))
    eg = jnp.exp(gcum_c)
    a = jnp.where(strict, beta * _bmm_nt(k, k) * gam, 0.0)
    tinv = _unit_lower_inverse(a, row, col, c)
    rhs = jnp.concatenate([beta * v, (beta * eg) * k], axis=2)
    sol = _bmm(tinv, rhs)
    u = sol[:, :, :DV_B] - _bmm(sol[:, :, DV_B:], sst)
    o = eg * _bmm(q, sst) + _bmm(_bmm_nt(q, k) * gam, u)
    g_last = gcum_c[:, c - 1:c, :]
    s_new = jnp.exp(g_last) * sst + _bmm_tn(k * jnp.exp(g_last - gcum_c), u)
    s_ref[...] = s_new.reshape(nseq, H_B, DK_B, DV_B)
    for i, (s, h) in enumerate(sh):
        hs = slice(h * DV_B, (h + 1) * DV_B)
        out_ref[rows(s), hs] = _head_norm(o[i], ng_ref[:, hs]) * _silu(z_ref[rows(s), hs])


def gdn_call(proj, gates_t, brow, bcol, arow, acol, conv_w, norm_g, seqs, layer, depth, init_states, buf0,
             prev_mix, prev_states):
    wb = 3 * H_B * DK_B
    has_init = init_states is not None
    nseq = seqs["nseq"]
    body = functools.partial(_gdn_kernel, c=seqs["c"], nseq=nseq, n_valid=seqs["n_valid"], has_init=has_init)
    if has_init:
        buf_spec = pl.BlockSpec((None, nseq, CONV_W - 1, wb), lambda b, ci: (layer, b, 0, 0))
    else:
        buf0 = jnp.zeros((1, nseq, CONV_W - 1, wb), F32)
        buf_spec = pl.BlockSpec((None, nseq, CONV_W - 1, wb), lambda b, ci: (0, 0, 0, 0))
    extra = [(gates_t, _gates_spec(seqs)), (brow, _const_spec((1, LANE))), (bcol, _const_spec((GATE_ROWS, 1))),
             (arow, _const_spec((1, LANE))), (acol, _const_spec((GATE_ROWS, 1))),
             (conv_w.astype(F32), _const_spec((CONV_W, wb))), (buf0, buf_spec),
             (norm_g.reshape(1, H_B * DV_B).astype(F32), _const_spec((1, H_B * DV_B)))]
    return _mixer_call(body, "gdn", proj, seqs, layer, depth,
                       [(wb, B_OFF // wb), (A_W, (B_OFF + wb) // A_W), (LANE, G_OFF // LANE)], extra, init_states,
                       [(H_B, DK_B, DV_B)], H_B * DV_B, prev_mix, prev_states,
                       scratch_shapes=[pltpu.VMEM((nseq, seqs["c"] + 8, wb), F32)])


def _hgrn_kernel(q_ref, f_ref, i_ref, g_ref, lb_ref, ng_ref, s0_ref, out_ref, s_ref, st_ref, *,
                 c, sub, nseq, n_valid, has_init, nchunk):
    ci = pl.program_id(1)

    sh = [(s, h) for s in range(nseq) for h in range(H_C)]

    @pl.when(ci == 0)
    def _():
        for i, (s, h) in enumerate(sh):
            st_ref[i] = s0_ref[s, h].T if has_init else jnp.zeros((DV_C, DK_C), F32)

    padded = n_valid < sub
    rowi = lax.broadcasted_iota(jnp.int32, (1, sub, 1), 1)
    heads = lambda ref, j: jnp.stack(
        [ref[s * c + j * sub:s * c + (j + 1) * sub, h * DK_C:(h + 1) * DK_C] for s, h in sh], axis=0)
    lb = jnp.stack([lb_ref[:, h * DK_C:(h + 1) * DK_C] for s, h in sh], axis=0)
    log_1mlb, log_lb = jnp.log1p(-lb), jnp.log(lb)
    sst = st_ref[...]
    for j in range(c // sub):
        cf = heads(f_ref, j)
        la = log_1mlb + _log_sigmoid(cf)
        logf = jnp.maximum(la, log_lb) + jnp.log1p(jnp.exp(-jnp.abs(la - log_lb)))
        k = (1.0 - lb) * _sigmoid(-cf)
        if padded:
            logf, k = jnp.where(rowi < n_valid, logf, 0.0), jnp.where(rowi < n_valid, k, 0.0)
        q = _silu(heads(q_ref, j))
        iv = heads(i_ref, j)
        bc = jnp.zeros((nseq * H_C, sub, DK_C), F32)
        for t in range(sub):
            bc = bc + jnp.where(rowi >= t, logf[:, t:t + 1, :], 0.0)
        o = _bmm_nt(q * jnp.exp(bc), sst)
        for t in range(sub):
            dec = jnp.exp(jnp.where(rowi >= t, bc - bc[:, t:t + 1, :], NEG_INF))
            att = jnp.sum(q * dec * k[:, t:t + 1, :], axis=2, keepdims=True)
            o = o + att * iv[:, t:t + 1, :]
        b_last = bc[:, sub - 1:sub, :]
        sst = jnp.exp(b_last) * sst + _bmm_tn(iv, k * jnp.exp(b_last - bc))
        for i, (s, h) in enumerate(sh):
            rows = slice(s * c + j * sub, s * c + (j + 1) * sub)
            hs = slice(h * DK_C, (h + 1) * DK_C)
            out_ref[rows, hs] = _head_norm(o[i], ng_ref[:, hs]) * _silu(g_ref[rows, hs])
    st_ref[...] = sst

    @pl.when(ci == nchunk - 1)
    def _():
        for i, (s, h) in enumerate(sh):
            s_ref[s, h] = st_ref[i].T


def hgrn_call(proj, lb, norm_g, seqs, sub, layer, depth, init_states, prev_mix, prev_states):
    w = H_C * DK_C
    cb = C_OFF // w
    body = functools.partial(_hgrn_kernel, c=seqs["c"], sub=sub, nseq=seqs["nseq"],
                             n_valid=min(seqs["n_valid"], sub),
                             has_init=init_states is not None, nchunk=seqs["nchunk"])
    extra = [(lb.reshape(1, w).astype(F32), _const_spec((1, w))),
             (norm_g.reshape(1, w).astype(F32), _const_spec((1, w)))]
    return _mixer_call(body, "hgrn", proj, seqs, layer, depth,
                       [(w, cb), (w, cb + 1), (w, cb + 2), (w, cb + 3)], extra, init_states,
                       [(H_C, DK_C, DV_C)], w, prev_mix, prev_states,
                       scratch_shapes=[pltpu.VMEM((seqs["nseq"] * H_C, DV_C, DK_C), F32)])


def _top_values(x, k):
    rows = []
    for _ in range(k):
        mx = jnp.max(x, axis=0, keepdims=True)
        rows.append(mx)
        x = jnp.where(x == mx, NEG_INF, x)
    return jnp.concatenate(rows, axis=0)


def _route_kernel(q_ref, sk_ref, s1_ref, s2_ref, e1_ref, e2_ref, tau_ref):
    for h in range(PEER_HEADS):
        s1 = _mmf_nt(sk_ref[h, 0], q_ref[:, (2 * h) * LANE:(2 * h + 1) * LANE])
        s2 = _mmf_nt(sk_ref[h, 1], q_ref[:, (2 * h + 1) * LANE:(2 * h + 2) * LANE])
        a = _top_values(s1, PEER_TOPK)
        b = _top_values(s2, PEER_TOPK)
        cand = jnp.concatenate([a[0:1, :] + b] + [a[i:i + 1, :] + b[:PEER_TOPK // 2, :]
                                                  for i in range(1, PEER_TOPK)], axis=0)
        best = _top_values(cand, PEER_TOPK)
        z = jnp.sum(jnp.exp(best - best[0:1, :]), axis=0, keepdims=True)
        s1_ref[h] = s1
        s2_ref[h] = s2
        e1_ref[h] = jnp.exp(s1 - a[0:1, :])
        e2_ref[h] = jnp.exp(s2 - b[0:1, :]) / z
        tau_ref[h:h + 1, :] = best[PEER_TOPK - 1:PEER_TOPK, :]


def route_call(qry, subkeys, tb=256):
    t = qry.shape[0]
    big = jax.ShapeDtypeStruct((PEER_HEADS, N_KEYS, t), F32)
    big_spec = pl.BlockSpec((PEER_HEADS, N_KEYS, tb), lambda i: (0, 0, i))
    return pl.pallas_call(
        _route_kernel, grid=(t // tb,),
        in_specs=[pl.BlockSpec((tb, qry.shape[1]), lambda i: (i, 0)),
                  pl.BlockSpec(subkeys.shape, lambda i: (0, 0, 0, 0))],
        out_specs=[big_spec] * 4 + [pl.BlockSpec((PEER_HEADS, tb), lambda i: (0, i))],
        out_shape=[big] * 4 + [jax.ShapeDtypeStruct((PEER_HEADS, t), F32)],
        compiler_params=_cparams(("parallel",)), name="peer_route",
    )(qry, subkeys.astype(F32))


PEER_SUB = 256
PEER_LANES = 256


def _peer_kernel(t_ref, x_ref, u_ref, v_ref, s1_ref, s2_ref, e1_ref, e2_ref, tau_ref, y_ref, *, eb):
    j = pl.program_id(1)
    tm = t_ref.shape[0]

    @pl.when(j == 0)
    def _():
        y_ref[...] = x_ref[...]

    for p in range(eb // PEER_SUB):
        es = slice(p * PEER_SUB, (p + 1) * PEER_SUB)
        a_t = lax.dot_general(u_ref[es, :], t_ref[...], (((1,), (1,)), ((), ())), preferred_element_type=F32)
        rows = []
        for r in range(PEER_SUB // N_KEYS):
            i1 = j * (eb // N_KEYS) + p * (PEER_SUB // N_KEYS) + r
            s1_rows = [s1_ref[h, pl.ds(i1, 1), :] for h in range(PEER_HEADS)]
            e1_rows = [e1_ref[h, pl.ds(i1, 1), :] for h in range(PEER_HEADS)]
            tiles = []
            for tc in range(tm // PEER_LANES):
                ls = slice(tc * PEER_LANES, (tc + 1) * PEER_LANES)
                gsum = None
                for h in range(PEER_HEADS):
                    sc = s1_rows[h][:, ls] + s2_ref[h, :, ls]
                    gh = jnp.where(sc >= tau_ref[h:h + 1, ls], e1_rows[h][:, ls] * e2_ref[h, :, ls], 0.0)
                    gsum = gh if gsum is None else gsum + gh
                tiles.append(jax.nn.gelu(a_t[r * N_KEYS:(r + 1) * N_KEYS, ls]) * gsum)
            rows.append(jnp.concatenate(tiles, axis=1))
        coef = jnp.concatenate(rows, axis=0)
        y_ref[...] += jnp.dot(coef.T.astype(BF16), v_ref[es, :], preferred_element_type=F32)


def peer_call(t_bf, x, u_bf, v_bf, layer, s1, s2, e1, e2, tau, tm=512, eb=512):
    t, d = t_bf.shape
    ne = u_bf.shape[1]
    once = pl.Buffered(1)
    big_spec = pl.BlockSpec((PEER_HEADS, N_KEYS, tm), lambda i, j: (0, 0, i), pipeline_mode=once)
    return pl.pallas_call(
        functools.partial(_peer_kernel, eb=eb),
        grid=(t // tm, ne // eb),
        in_specs=[pl.BlockSpec((tm, d), lambda i, j: (i, 0), pipeline_mode=once),
                  pl.BlockSpec((tm, d), lambda i, j: (i, 0), pipeline_mode=once),
                  pl.BlockSpec((None, eb, d), lambda i, j: (layer, j, 0)),
                  pl.BlockSpec((None, eb, d), lambda i, j: (layer, j, 0)),
                  big_spec, big_spec, big_spec, big_spec,
                  pl.BlockSpec((PEER_HEADS, tm), lambda i, j: (0, i), pipeline_mode=once)],
        out_specs=pl.BlockSpec((tm, d), lambda i, j: (i, 0)),
        out_shape=jax.ShapeDtypeStruct((t, d), F32),
        compiler_params=_cparams(("parallel", "arbitrary")), name="peer_dense",
    )(t_bf, x, u_bf, v_bf, s1, s2, e1, e2, tau)


def _gate_params(b_i, b_f, a_log, dt_bias):
    zeros = jnp.zeros((LANE - 2 * H_A - 2 * H_B,), F32)
    bias = jnp.concatenate([b_i.astype(F32), b_f.astype(F32), dt_bias.astype(F32), jnp.zeros((H_B,), F32), zeros])
    alog = jnp.concatenate([jnp.zeros((2 * H_A,), F32), a_log.astype(F32), jnp.zeros((H_B,), F32), zeros])
    return (bias.reshape(1, LANE), bias[:GATE_ROWS].reshape(GATE_ROWS, 1),
            alog.reshape(1, LANE), alog[:GATE_ROWS].reshape(GATE_ROWS, 1))


def _gates_rows(proj, c):
    na, nb = 2 * H_A, 2 * H_B
    g = jnp.concatenate([proj[:, G_OFF + LANE:G_OFF + LANE + na], proj[:, G_OFF + na:G_OFF + na + nb],
                         jnp.zeros((proj.shape[0], GATE_ROWS - na - nb), F32)], axis=1)
    return g.reshape(g.shape[0] // c, c, GATE_ROWS).transpose(0, 2, 1)


def _hgrn_lower_bounds(logits):
    p = jax.nn.softmax(logits.astype(F32), axis=0)
    cs = jnp.cumsum(p, axis=0)
    return cs - cs[0:1]


def kernel(x_prompt, x_sample, state_mlstm_C, state_mlstm_n, state_mlstm_m, state_gdn_S, state_gdn_conv,
           state_hgrn_S, norm_mix_g, w_in, mlstm_b_i, mlstm_b_f, mlstm_norm_g, gdn_conv_w, gdn_A_log,
           gdn_dt_bias, gdn_norm_g, hgrn_lb_logits, hgrn_norm_g, w_out, norm_ffn_g, peer_w_q, peer_subkeys,
           peer_u, peer_v, final_norm_g, *, chunk=64, hgrn_chunk=16, tm=512, tm_peer=512, eb_peer=512, tb=256):
    bp, lp, d = x_prompt.shape
    bs, ls, _ = x_sample.shape
    depth = w_in.shape[0]
    tp = bp * lp
    ts = bs * ls
    x = jnp.concatenate([x_prompt.reshape(tp, d), x_sample.reshape(ts, d)], axis=0).astype(F32)
    tm_in = next(c for c in (1024, 1088, tm) if (tp + ts) % c == 0)
    lbs = _hgrn_lower_bounds(hgrn_lb_logits)
    wb = 3 * H_B * DK_B
    u_all, v_all = peer_u.astype(BF16), peer_v.astype(BF16)
    wo_all, wq_all = w_out.astype(BF16), peer_w_q.astype(BF16)
    pseq = dict(row0=0, nb=bp, nchunk=lp // chunk, c=chunk, n_valid=chunk, nseq=1)
    sseq = dict(row0=0, nb=bs, nchunk=1, c=SAMPLE_PAD, n_valid=ls, nseq=SAMPLE_SEQS)
    a_init = [state_mlstm_C, state_mlstm_n, state_mlstm_m.reshape(depth, bs, 1, H_A)]
    st_ap = st_as = st_bp = st_bs = st_cp = st_cs = None
    bufs_p, bufs_s = [], []
    for l in range(depth):
        h = rmsnorm_call(x, norm_mix_g[l], BF16, tb=tb)
        proj = in_proj_call(h, w_in, l, tm=tm_in)
        proj_s = jnp.pad(proj[tp:].reshape(bs, ls, NP), ((0, 0), (0, SAMPLE_PAD - ls), (0, 0)))
        proj_s = proj_s.reshape(bs * SAMPLE_PAD, NP)
        brow, bcol, arow, acol = _gate_params(mlstm_b_i[l], mlstm_b_f[l], gdn_A_log[l], gdn_dt_bias[l])
        gt_p = _gates_rows(proj[:tp], chunk)
        gt_s = _gates_rows(proj_s, SAMPLE_PAD)
        mix_a, *st_ap = mlstm_call(proj, gt_p, brow, bcol, mlstm_norm_g[l], pseq, l, depth, None, None, st_ap)
        smp_a, *st_as = mlstm_call(proj_s, gt_s, brow, bcol, mlstm_norm_g[l], sseq, l, depth, a_init, None, st_as)
        gdn_w = (brow, bcol, arow, acol, gdn_conv_w[l], gdn_norm_g[l])
        mix_b, *st_bp = gdn_call(proj, gt_p, *gdn_w, pseq, l, depth, None, None, None, st_bp)
        smp_b, *st_bs = gdn_call(proj_s, gt_s, *gdn_w, sseq, l, depth, [state_gdn_S], state_gdn_conv, None, st_bs)
        mix_c, *st_cp = hgrn_call(proj, lbs[l], hgrn_norm_g[l], pseq, hgrn_chunk, l, depth, None, None, st_cp)
        smp_c, *st_cs = hgrn_call(proj_s, lbs[l], hgrn_norm_g[l], sseq, SAMPLE_PAD, l, depth, [state_hgrn_S],
                                  None, st_cs)
        valid = lambda m: m.reshape(bs, SAMPLE_PAD, m.shape[1])[:, :ls].reshape(ts, m.shape[1])
        mix_a, mix_b, mix_c = (m.at[tp:].set(valid(s)) for m, s in ((mix_a, smp_a), (mix_b, smp_b), (mix_c, smp_c)))
        bufs_p.append(proj[:tp, B_OFF:B_OFF + wb].reshape(bp, lp, wb)[:, lp - (CONV_W - 1):])
        bufs_s.append(proj[tp:, B_OFF:B_OFF + wb].reshape(bs, ls, wb)[:, ls - (CONV_W - 1):])
        x = matmul_call([mix_a, mix_b, mix_c], wo_all, l, res=x, tm=tm, name="out_proj")
        t_bf = rmsnorm_call(x, norm_ffn_g[l], BF16, tb=tb)
        qry = matmul_call([t_bf], wq_all, l, tm=tm, name="peer_query")
        s1, s2, e1, e2, tau = route_call(qry, peer_subkeys[l], tb=tb)
        x = peer_call(t_bf, x, u_all, v_all, l, s1, s2, e1, e2, tau, tm=tm_peer, eb=eb_peer)
    y_prompt = rmsnorm_call(x, final_norm_g, F32, tb=tb, row0=0, rows=tp).reshape(bp, lp, d)
    y_sample = rmsnorm_call(x, final_norm_g, F32, tb=tb, row0=tp, rows=ts).reshape(bs, ls, d)
    m_p, m_s = st_ap[2].reshape(depth, bp, H_A), st_as[2].reshape(depth, bs, H_A)
    return (y_prompt, y_sample, st_ap[0], st_as[0], st_ap[1], st_as[1], m_p, m_s, st_bp[0], st_bs[0],
            jnp.stack(bufs_p, axis=0), jnp.stack(bufs_s, axis=0), st_cp[0], st_cs[0])
```

```python
import functools

import jax
import jax.numpy as jnp
from jax import lax
from jax.experimental import pallas as pl
from jax.experimental.pallas import tpu as pltpu

F32 = jnp.float32
BF16 = jnp.bfloat16
EPS = 1e-6
NEG_INF = float("-inf")

H_A, DK_A, DV_A = 6, 128, 256
H_B, DK_B, DV_B = 12, 128, 128
H_C, DK_C, DV_C = 8, 128, 128
CONV_W = 4
PEER_HEADS, N_KEYS, PEER_TOPK = 8, 128, 16
SAMPLE_PAD = 8
SAMPLE_SEQS = 4
LANE = 128
VMEM_LIMIT = 56 * 1024 * 1024

A_W = H_A * DV_A
B_OFF = 3 * A_W
G_OFF = B_OFF + 4 * H_B * DV_B
C_OFF = G_OFF + 512
NP = C_OFF + 4 * H_C * DV_C
GATE_ROWS = 40


def _cparams(sem):
    return pltpu.CompilerParams(dimension_semantics=sem, vmem_limit_bytes=VMEM_LIMIT)


def _mm(a, b):
    return jnp.dot(a.astype(BF16), b.astype(BF16), preferred_element_type=F32)


def _mmf_nt(a, b):
    return lax.dot_general(a, b, (((1,), (1,)), ((), ())), preferred_element_type=F32,
                           precision=lax.Precision.HIGHEST)


def _softplus(x):
    return jnp.maximum(x, 0.0) + jnp.log1p(jnp.exp(-jnp.abs(x)))


def _log_sigmoid(x):
    return -_softplus(-x)


def _sigmoid(x):
    return 1.0 / (1.0 + jnp.exp(-x))


def _silu(x):
    return x * _sigmoid(x)


def _head_norm(x, g_row):
    return x * lax.rsqrt(jnp.mean(x * x, axis=-1, keepdims=True) + EPS) * g_row


def _rms_kernel(x_ref, g_ref, h_ref):
    x = x_ref[...]
    y = x * lax.rsqrt(jnp.mean(x * x, axis=-1, keepdims=True) + EPS)
    h_ref[...] = (y * g_ref[...]).astype(h_ref.dtype)


def rmsnorm_call(x, g, out_dtype, tb=256, row0=0, rows=None):
    d = x.shape[1]
    rows = x.shape[0] if rows is None else rows
    rb = row0 // tb
    return pl.pallas_call(
        _rms_kernel, grid=(rows // tb,),
        in_specs=[pl.BlockSpec((tb, d), lambda i: (rb + i, 0)), pl.BlockSpec((1, d), lambda i: (0, 0))],
        out_specs=pl.BlockSpec((tb, d), lambda i: (i, 0)),
        out_shape=jax.ShapeDtypeStruct((rows, d), out_dtype),
        compiler_params=_cparams(("parallel",)), name="rmsnorm",
    )(x, g.reshape(1, d).astype(F32))


def _matmul_kernel(*refs, n_pairs, has_res):
    a_refs, w_refs = refs[:n_pairs], refs[n_pairs:2 * n_pairs]
    o_ref = refs[-1]
    acc = _mm(a_refs[0][...], w_refs[0][...])
    for a_ref, w_ref in zip(a_refs[1:], w_refs[1:]):
        acc = acc + _mm(a_ref[...], w_ref[...])
    if has_res:
        acc = acc + refs[2 * n_pairs][...]
    o_ref[...] = acc


def matmul_call(a_list, w, layer, res=None, tm=512, tn=512, name="matmul"):
    t = a_list[0].shape[0]
    n = w.shape[2]
    in_specs = [pl.BlockSpec((tm, a.shape[1]), lambda i, j: (i, 0)) for a in a_list]
    row = 0
    for a in a_list:
        k = a.shape[1]
        in_specs.append(pl.BlockSpec((None, k, tn), functools.partial(lambda i, j, rb: (layer, rb, j), rb=row // k)))
        row += k
    args = list(a_list) + [w] * len(a_list)
    if res is not None:
        in_specs.append(pl.BlockSpec((tm, tn), lambda i, j: (i, j)))
        args.append(res)
    return pl.pallas_call(
        functools.partial(_matmul_kernel, n_pairs=len(a_list), has_res=res is not None),
        grid=(t // tm, n // tn), in_specs=in_specs,
        out_specs=pl.BlockSpec((tm, tn), lambda i, j: (i, j)),
        out_shape=jax.ShapeDtypeStruct((t, n), F32),
        compiler_params=_cparams(("parallel", "parallel")), name=name,
    )(*args)


IN_TN = 512
A_BLOCKS = B_OFF // IN_TN
G_BLOCK = G_OFF // IN_TN
B_SHIFT = 2 * H_A
C_SHIFT = 2 * H_A + 2 * H_B
W_ROWS = 512


def _in_proj_kernel(a_ref, wm_ref, we_ref, o_ref, wb_ref):
    j, i = pl.program_id(0), pl.program_id(1)
    k = wm_ref.shape[0]

    def convert(shift):
        def body(r, carry):
            rs = pl.ds(pl.multiple_of(r * W_ROWS, W_ROWS), W_ROWS)
            if shift == 0:
                blk = wm_ref[rs, :]
            else:
                blk = jnp.concatenate([wm_ref[rs, shift:], we_ref[rs, :shift]], axis=1)
            wb_ref[rs, :] = blk.astype(BF16)
            return carry
        lax.fori_loop(0, k // W_ROWS, body, 0)

    first = i == 0

    @pl.when(first & (j < A_BLOCKS))
    def _():
        convert(0)

    @pl.when(first & (j >= A_BLOCKS) & (j < G_BLOCK))
    def _():
        convert(B_SHIFT)

    @pl.when(first & (j == G_BLOCK))
    def _():
        wb_ref[:, 0:LANE] = wm_ref[:, 0:LANE].astype(BF16)
        wb_ref[:, LANE:2 * LANE] = we_ref[...].astype(BF16)
        wb_ref[:, 2 * LANE:] = jnp.zeros((k, IN_TN - 2 * LANE), BF16)

    @pl.when(first & (j > G_BLOCK))
    def _():
        convert(C_SHIFT)

    o_ref[...] = jnp.dot(a_ref[...], wb_ref[...], preferred_element_type=F32)


def in_proj_call(a, w_in, layer, tm):
    t, k = a.shape
    lanes_per_tile = IN_TN // LANE
    main_blk = lambda j: jnp.where(j <= G_BLOCK, j, j - 1)
    extra_blk = lambda j: jnp.where(j < G_BLOCK, lanes_per_tile * (j + 1),
                                    jnp.where(j == G_BLOCK, B_OFF // LANE, lanes_per_tile * j))
    return pl.pallas_call(
        _in_proj_kernel, grid=(NP // IN_TN, t // tm),
        in_specs=[pl.BlockSpec((tm, k), lambda j, i: (i, 0)),
                  pl.BlockSpec((None, k, IN_TN), lambda j, i: (layer, 0, main_blk(j))),
                  pl.BlockSpec((None, k, LANE), lambda j, i: (layer, 0, extra_blk(j)))],
        out_specs=pl.BlockSpec((tm, IN_TN), lambda j, i: (i, j)),
        out_shape=jax.ShapeDtypeStruct((t, NP), F32),
        scratch_shapes=[pltpu.VMEM((k, IN_TN), BF16)],
        compiler_params=_cparams(("parallel", "arbitrary")), name="in_proj",
    )(a, w_in, w_in)


def _tri_masks(c):
    row = lax.broadcasted_iota(jnp.int32, (c, c), 0)
    col = lax.broadcasted_iota(jnp.int32, (c, c), 1)
    return row, col


def _mlstm_kernel(qk_ref, v_ref, o_ref, gc_ref, gt_ref, brow_ref, bcol_ref, ng_ref, c0_ref, n0_ref, m0_ref,
                  out_ref, c_ref, n_ref, m_ref, *, c, nseq, n_valid, has_init):
    ci = pl.program_id(1)

    @pl.when(ci == 0)
    def _():
        if has_init:
            c_ref[...] = c0_ref[...]
            n_ref[...] = n0_ref[...]
            m_ref[...] = m0_ref[...]
        else:
            c_ref[...] = jnp.zeros_like(c_ref)
            n_ref[...] = jnp.zeros_like(n_ref)
            m_ref[...] = jnp.zeros_like(m_ref)

    gc = gc_ref[...] + brow_ref[...]
    gt = gt_ref[...] + bcol_ref[...]
    row, col = _tri_masks(c)
    tri = row >= col
    padded = n_valid < c
    if padded:
        valid_c = lax.broadcasted_iota(jnp.int32, (c, 1), 0) < n_valid
        valid_r = lax.broadcasted_iota(jnp.int32, (1, c), 1) < n_valid
    sh = [(s, h) for s in range(nseq) for h in range(H_A)]
    rows = lambda s: slice(s * c, (s + 1) * c)
    heads_c = lambda x, off: jnp.stack([x[rows(s), off + h:off + h + 1] for s, h in sh], axis=0)
    heads_r = lambda x, off: jnp.stack([x[s, off + h:off + h + 1, :] for s, h in sh], axis=0)
    ig_c, lf_c = heads_c(gc, 0), heads_c(_log_sigmoid(gc), H_A)
    ig_r, lf_r = heads_r(gt, 0), heads_r(_log_sigmoid(gt), H_A)
    if padded:
        ig_c, lf_c = jnp.where(valid_c, ig_c, NEG_INF), jnp.where(valid_c, lf_c, 0.0)
        ig_r, lf_r = jnp.where(valid_r, ig_r, NEG_INF), jnp.where(valid_r, lf_r, 0.0)
    q = jnp.stack([qk_ref[rows(s), h * DK_A:(h + 1) * DK_A] for s, h in sh], axis=0) * (DK_A ** -0.5)
    k = jnp.stack([qk_ref[rows(s), (H_A + h) * DK_A:(H_A + h + 1) * DK_A] for s, h in sh], axis=0)
    v = jnp.stack([v_ref[rows(s), h * DV_A:(h + 1) * DV_A] for s, h in sh], axis=0)
    cst = c_ref[...].reshape(nseq * H_A, DK_A, DV_A)
    nst = jnp.stack([n_ref[s, h:h + 1, :] for s, h in sh], axis=0)
    m_prev = jnp.stack([m_ref[s, :, h:h + 1] for s, h in sh], axis=0)
    b_c = jnp.sum(jnp.where(tri, lf_r, 0.0), axis=2, keepdims=True)
    b_r = jnp.sum(jnp.where(row <= col, lf_c, 0.0), axis=1, keepdims=True)
    log_d = jnp.where(tri, b_c - b_r + ig_r, NEG_INF)
    m_inter = b_c + m_prev
    m_t = jnp.maximum(m_inter, jnp.max(log_d, axis=2, keepdims=True))
    s = _bmm_nt(q, k) * jnp.exp(log_d - m_t)
    inter = jnp.exp(m_inter - m_t)
    num = _bmm(s, v) + inter * _bmm(q, cst)
    den = jnp.sum(s, axis=2, keepdims=True) + inter * jnp.sum(q * nst, axis=2, keepdims=True)
    hh = num / jnp.maximum(jnp.abs(den), jnp.exp(-m_t))
    m_new = m_t[:, c - 1:c, :]
    w = jnp.exp(b_c[:, c - 1:c, :] - b_c + ig_c - m_new)
    decay = inter[:, c - 1:c, :]
    kw = k * w
    c_ref[...] = (decay * cst + _bmm_tn(kw, v)).reshape(nseq, H_A, DK_A, DV_A)
    n_new = decay * nst + jnp.sum(kw, axis=1, keepdims=True)
    for i, (s, h) in enumerate(sh):
        hs = slice(h * DV_A, (h + 1) * DV_A)
        n_ref[s, h:h + 1, :] = n_new[i]
        m_ref[s, :, h:h + 1] = m_new[i]
        out_ref[rows(s), hs] = _head_norm(hh[i], ng_ref[:, hs]) * _sigmoid(o_ref[rows(s), hs])


def _mixer_call(body, name, proj, seqs, layer, depth, tok_blocks, extra_inputs, init_states, st_shapes,
                mix_w, prev_mix, prev_states, scratch_shapes=()):
    row0, nb, nchunk, c, nseq = seqs["row0"], seqs["nb"], seqs["nchunk"], seqs["c"], seqs["nseq"]
    assert nseq == 1 or nchunk == 1
    rows = nseq * c
    rb = row0 // rows
    tok = lambda b, ci: rb + b * nchunk + ci
    st_spec = lambda shape, lyr: pl.BlockSpec(
        (None, nseq) + shape, functools.partial(lambda b, ci, n: (lyr, b) + (0,) * n, n=len(shape)))
    in_specs = [pl.BlockSpec((rows, w), functools.partial(lambda b, ci, cb: (tok(b, ci), cb), cb=cb))
                for w, cb in tok_blocks]
    args = [proj] * len(tok_blocks)
    for arr, spec in extra_inputs:
        in_specs.append(spec)
        args.append(arr)
    if init_states is None:
        for shape in st_shapes:
            in_specs.append(pl.BlockSpec((None, nseq) + shape, functools.partial(
                lambda b, ci, n: (0,) * (n + 2), n=len(shape))))
            args.append(jnp.zeros((1, nseq) + shape, F32))
    else:
        for shape, arr in zip(st_shapes, init_states):
            in_specs.append(st_spec(shape, layer))
            args.append(arr)
    n_in = len(args)
    out_specs = [pl.BlockSpec((rows, mix_w), lambda b, ci: (tok(b, ci), 0))]
    out_shape = [jax.ShapeDtypeStruct((proj.shape[0], mix_w), F32)]
    for shape in st_shapes:
        out_specs.append(st_spec(shape, layer))
        out_shape.append(jax.ShapeDtypeStruct((depth, nb) + shape, F32))
    prev = [prev_mix] + (list(prev_states) if prev_states is not None else [None] * len(st_shapes))
    aliases = {}
    for k, arr in enumerate(prev):
        if arr is not None:
            aliases[len(args)] = k
            in_specs.append(pl.BlockSpec(memory_space=pl.ANY))
            args.append(arr)
    n_alias = len(args) - n_in

    def kernel_fn(*refs):
        body(*refs[:n_in], *refs[n_in + n_alias:])

    return pl.pallas_call(
        kernel_fn, grid=(nb // nseq, nchunk), in_specs=in_specs, out_specs=out_specs, out_shape=out_shape,
        input_output_aliases=aliases, scratch_shapes=list(scratch_shapes),
        compiler_params=_cparams(("parallel", "arbitrary")), name=name)(*args)


def _const_spec(shape):
    return pl.BlockSpec(shape, functools.partial(lambda b, ci, n: (0,) * n, n=len(shape)))


def _gates_spec(seqs):
    nchunk, c, nseq = seqs["nchunk"], seqs["c"], seqs["nseq"]
    return pl.BlockSpec((nseq, GATE_ROWS, c), lambda b, ci: (b * nchunk + ci, 0, 0))


def mlstm_call(proj, gates_t, brow, bcol, norm_g, seqs, layer, depth, init_states, prev_mix, prev_states):
    body = functools.partial(_mlstm_kernel, c=seqs["c"], nseq=seqs["nseq"], n_valid=seqs["n_valid"],
                             has_init=init_states is not None)
    extra = [(gates_t, _gates_spec(seqs)), (brow, _const_spec((1, LANE))), (bcol, _const_spec((GATE_ROWS, 1))),
             (norm_g.reshape(1, A_W).astype(F32), _const_spec((1, A_W)))]
    return _mixer_call(body, "mlstm", proj, seqs, layer, depth,
                       [(A_W, 0), (A_W, 1), (A_W, 2), (LANE, G_OFF // LANE + 1)], extra, init_states,
                       [(H_A, DK_A, DV_A), (H_A, DK_A), (1, H_A)], A_W, prev_mix, prev_states)


def _bdot(a, b, dims):
    return lax.dot_general(a, b, (dims, ((0,), (0,))), preferred_element_type=F32)


def _bmm(a, b):
    return _bdot(a.astype(BF16), b.astype(BF16), ((2,), (1,)))


def _bmm_nt(a, b):
    return _bdot(a.astype(BF16), b.astype(BF16), ((2,), (2,)))


def _bmm_tn(a, b):
    return _bdot(jnp.swapaxes(a, 1, 2).astype(BF16), b.astype(BF16), ((2,), (1,)))


def _unit_lower_inverse(a, row, col, c):
    d = jnp.where(row == col, 1.0, 0.0)
    s = 1
    while s < c:
        blk = jnp.where(((row // (2 * s)) == (col // (2 * s))) & ((row % (2 * s)) >= s) & ((col % (2 * s)) < s),
                        a, 0.0)
        if s == 1:
            d = d - blk
        else:
            d = d - _bmm(_bmm(d, blk), d)
        s *= 2
    return d


def _gdn_kernel(qkv_ref, z_ref, gc_ref, gt_ref, brow_ref, bcol_ref, arow_ref, acol_ref, cw_ref, buf0_ref, ng_ref,
                s0_ref, out_ref, s_ref, xp_ref, *, c, nseq, n_valid, has_init):
    ci = pl.program_id(1)
    hk = H_B * DK_B
    rows = lambda s: slice(s * c, (s + 1) * c)

    @pl.when(ci == 0)
    def _():
        xp_ref[:, 0:8, :] = jnp.zeros((nseq, 8, xp_ref.shape[2]), F32)
        if has_init:
            s_ref[...] = s0_ref[...]
            xp_ref[:, 8 - (CONV_W - 1):8, :] = buf0_ref[...]
        else:
            s_ref[...] = jnp.zeros_like(s_ref)

    @pl.when(ci > 0)
    def _():
        xp_ref[:, 0:8, :] = xp_ref[:, c:c + 8, :]

    for s in range(nseq):
        xp_ref[s, 8:8 + c, :] = qkv_ref[rows(s), :]

    gc = gc_ref[...] + brow_ref[...]
    gt = gt_ref[...] + bcol_ref[...]
    row, col = _tri_masks(c)
    tri = row >= col
    strict = row > col
    padded = n_valid < c
    if padded:
        valid_c = lax.broadcasted_iota(jnp.int32, (c, 1), 0) < n_valid
        valid_r = lax.broadcasted_iota(jnp.int32, (1, c), 1) < n_valid
    ga, gb = 2 * H_A, 2 * H_A + H_B

    sh = [(s, h) for s in range(nseq) for h in range(H_B)]

    def conv_act(c0):
        slabs = []
        for s, h in sh:
            cs = slice(c0 + h * DK_B, c0 + (h + 1) * DK_B)
            y = cw_ref[0:1, cs] * xp_ref[s, 8 - 3:8 - 3 + c, cs]
            for j in range(1, CONV_W):
                y = y + cw_ref[j:j + 1, cs] * xp_ref[s, 8 - 3 + j:8 - 3 + j + c, cs]
            slabs.append(_silu(y))
        return jnp.stack(slabs, axis=0)

    heads_c = lambda x, off: jnp.stack([x[rows(s), off + h:off + h + 1] for s, h in sh], axis=0)
    heads_r = lambda x, off: jnp.stack([x[s, off + h:off + h + 1, :] for s, h in sh], axis=0)
    g_c = heads_c(-jnp.exp(arow_ref[...]) * _softplus(gc), ga)
    g_r = heads_r(-jnp.exp(acol_ref[...]) * _softplus(gt), ga)
    beta = heads_c(_sigmoid(gc), gb)
    if padded:
        g_c, g_r = jnp.where(valid_c, g_c, 0.0), jnp.where(valid_r, g_r, 0.0)
        beta = jnp.where(valid_c, beta, 0.0)
    q = conv_act(0)
    k = conv_act(hk)
    v = conv_act(2 * hk)
    q = q * lax.rsqrt(jnp.sum(q * q, axis=2, keepdims=True) + EPS) * (DK_B ** -0.5)
    k = k * lax.rsqrt(jnp.sum(k * k, axis=2, keepdims=True) + EPS)
    sst = s_ref[...].reshape(nseq * H_B, DK_B, DV_B)
    gcum_c = jnp.sum(jnp.where(tri, g_r, 0.0), axis=2, keepdims=True)
    gcum_r = jnp.sum(jnp.where(row <= col, g_c, 0.0), axis=1, keepdims=True)
    gam = jnp.exp(jnp.where(tri, gcum_c - gcum_r, NEG_INF))
    eg = jnp.exp(gcum_c)
    a = jnp.where(strict, beta * _bmm_nt(k, k) * gam, 0.0)
    tinv = _unit_lower_inverse(a, row, col, c)
    rhs = jnp.concatenate([beta * v, (beta * eg) * k], axis=2)
    sol = _bmm(tinv, rhs)
    u = sol[:, :, :DV_B] - _bmm(sol[:, :, DV_B:], sst)
    o = eg * _bmm(q, sst) + _bmm(_bmm_nt(q, k) * gam, u)
    g_last = gcum_c[:, c - 1:c, :]
    s_new = jnp.exp(g_last) * sst + _bmm_tn(k * jnp.exp(g_last - gcum_c), u)
    s_ref[...] = s_new.reshape(nseq, H_B, DK_B, DV_B)
    for i, (s, h) in enumerate(sh):
        hs = slice(h * DV_B, (h + 1) * DV_B)
        out_ref[rows(s), hs] = _head_norm(o[i], ng_ref[:, hs]) * _silu(z_ref[rows(s), hs])


def gdn_call(proj, gates_t, brow, bcol, arow, acol, conv_w, norm_g, seqs, layer, depth, init_states, buf0,
             prev_mix, prev_states):
    wb = 3 * H_B * DK_B
    has_init = init_states is not None
    nseq = seqs["nseq"]
    body = functools.partial(_gdn_kernel, c=seqs["c"], nseq=nseq, n_valid=seqs["n_valid"], has_init=has_init)
    if has_init:
        buf_spec = pl.BlockSpec((None, nseq, CONV_W - 1, wb), lambda b, ci: (layer, b, 0, 0))
    else:
        buf0 = jnp.zeros((1, nseq, CONV_W - 1, wb), F32)
        buf_spec = pl.BlockSpec((None, nseq, CONV_W - 1, wb), lambda b, ci: (0, 0, 0, 0))
    extra = [(gates_t, _gates_spec(seqs)), (brow, _const_spec((1, LANE))), (bcol, _const_spec((GATE_ROWS, 1))),
             (arow, _const_spec((1, LANE))), (acol, _const_spec((GATE_ROWS, 1))),
             (conv_w.astype(F32), _const_spec((CONV_W, wb))), (buf0, buf_spec),
             (norm_g.reshape(1, H_B * DV_B).astype(F32), _const_spec((1, H_B * DV_B)))]
    return _mixer_call(body, "gdn", proj, seqs, layer, depth,
                       [(wb, B_OFF // wb), (A_W, (B_OFF + wb) // A_W), (LANE, G_OFF // LANE)], extra, init_states,
                       [(H_B, DK_B, DV_B)], H_B * DV_B, prev_mix, prev_states,
                       scratch_shapes=[pltpu.VMEM((nseq, seqs["c"] + 8, wb), F32)])


def _hgrn_kernel(q_ref, f_ref, i_ref, g_ref, lb_ref, ng_ref, s0_ref, out_ref, s_ref, st_ref, *,
                 c, sub, nseq, n_valid, has_init, nchunk):
    ci = pl.program_id(1)

    sh = [(s, h) for s in range(nseq) for h in range(H_C)]

    @pl.when(ci == 0)
    def _():
        for i, (s, h) in enumerate(sh):
            st_ref[i] = s0_ref[s, h].T if has_init else jnp.zeros((DV_C, DK_C), F32)

    padded = n_valid < sub
    rowi = lax.broadcasted_iota(jnp.int32, (1, sub, 1), 1)
    heads = lambda ref, j: jnp.stack(
        [ref[s * c + j * sub:s * c + (j + 1) * sub, h * DK_C:(h + 1) * DK_C] for s, h in sh], axis=0)
    lb = jnp.stack([lb_ref[:, h * DK_C:(h + 1) * DK_C] for s, h in sh], axis=0)
    log_1mlb, log_lb = jnp.log1p(-lb), jnp.log(lb)
    sst = st_ref[...]
    for j in range(c // sub):
        cf = heads(f_ref, j)
        la = log_1mlb + _log_sigmoid(cf)
        logf = jnp.maximum(la, log_lb) + jnp.log1p(jnp.exp(-jnp.abs(la - log_lb)))
        k = (1.0 - lb) * _sigmoid(-cf)
        if padded:
            logf, k = jnp.where(rowi < n_valid, logf, 0.0), jnp.where(rowi < n_valid, k, 0.0)
        q = _silu(heads(q_ref, j))
        iv = heads(i_ref, j)
        bc = jnp.zeros((nseq * H_C, sub, DK_C), F32)
        for t in range(sub):
            bc = bc + jnp.where(rowi >= t, logf[:, t:t + 1, :], 0.0)
        o = _bmm_nt(q * jnp.exp(bc), sst)
        for t in range(sub):
            dec = jnp.exp(jnp.where(rowi >= t, bc - bc[:, t:t + 1, :], NEG_INF))
            att = jnp.sum(q * dec * k[:, t:t + 1, :], axis=2, keepdims=True)
            o = o + att * iv[:, t:t + 1, :]
        b_last = bc[:, sub - 1:sub, :]
        sst = jnp.exp(b_last) * sst + _bmm_tn(iv, k * jnp.exp(b_last - bc))
        for i, (s, h) in enumerate(sh):
            rows = slice(s * c + j * sub, s * c + (j + 1) * sub)
            hs = slice(h * DK_C, (h + 1) * DK_C)
            out_ref[rows, hs] = _head_norm(o[i], ng_ref[:, hs]) * _silu(g_ref[rows, hs])
    st_ref[...] = sst

    @pl.when(ci == nchunk - 1)
    def _():
        for i, (s, h) in enumerate(sh):
            s_ref[s, h] = st_ref[i].T


def hgrn_call(proj, lb, norm_g, seqs, sub, layer, depth, init_states, prev_mix, prev_states):
    w = H_C * DK_C
    cb = C_OFF // w
    body = functools.partial(_hgrn_kernel, c=seqs["c"], sub=sub, nseq=seqs["nseq"],
                             n_valid=min(seqs["n_valid"], sub),
                             has_init=init_states is not None, nchunk=seqs["nchunk"])
    extra = [(lb.reshape(1, w).astype(F32), _const_spec((1, w))),
             (norm_g.reshape(1, w).astype(F32), _const_spec((1, w)))]
    return _mixer_call(body, "hgrn", proj, seqs, layer, depth,
                       [(w, cb), (w, cb + 1), (w, cb + 2), (w, cb + 3)], extra, init_states,
                       [(H_C, DK_C, DV_C)], w, prev_mix, prev_states,
                       scratch_shapes=[pltpu.VMEM((seqs["nseq"] * H_C, DV_C, DK_C), F32)])


def _top_values(x, k):
    rows = []
    for _ in range(k):
        mx = jnp.max(x, axis=0, keepdims=True)
        rows.append(mx)
        x = jnp.where(x == mx, NEG_INF, x)
    return jnp.concatenate(rows, axis=0)


def _route_kernel(q_ref, sk_ref, s1_ref, s2_ref, e1_ref, e2_ref, tau_ref):
    for h in range(PEER_HEADS):
        s1 = _mmf_nt(sk_ref[h, 0], q_ref[:, (2 * h) * LANE:(2 * h + 1) * LANE])
        s2 = _mmf_nt(sk_ref[h, 1], q_ref[:, (2 * h + 1) * LANE:(2 * h + 2) * LANE])
        a = _top_values(s1, PEER_TOPK)
        b = _top_values(s2, PEER_TOPK)
        cand = jnp.concatenate([a[0:1, :] + b] + [a[i:i + 1, :] + b[:PEER_TOPK // 2, :]
                                                  for i in range(1, PEER_TOPK)], axis=0)
        best = _top_values(cand, PEER_TOPK)
        z = jnp.sum(jnp.exp(best - best[0:1, :]), axis=0, keepdims=True)
        s1_ref[h] = s1
        s2_ref[h] = s2
        e1_ref[h] = jnp.exp(s1 - a[0:1, :])
        e2_ref[h] = jnp.exp(s2 - b[0:1, :]) / z
        tau_ref[h:h + 1, :] = best[PEER_TOPK - 1:PEER_TOPK, :]


def route_call(qry, subkeys, tb=256):
    t = qry.shape[0]
    big = jax.ShapeDtypeStruct((PEER_HEADS, N_KEYS, t), F32)
    big_spec = pl.BlockSpec((PEER_HEADS, N_KEYS, tb), lambda i: (0, 0, i))
    return pl.pallas_call(
        _route_kernel, grid=(t // tb,),
        in_specs=[pl.BlockSpec((tb, qry.shape[1]), lambda i: (i, 0)),
                  pl.BlockSpec(subkeys.shape, lambda i: (0, 0, 0, 0))],
        out_specs=[big_spec] * 4 + [pl.BlockSpec((PEER_HEADS, tb), lambda i: (0, i))],
        out_shape=[big] * 4 + [jax.ShapeDtypeStruct((PEER_HEADS, t), F32)],
        compiler_params=_cparams(("parallel",)), name="peer_route",
    )(qry, subkeys.astype(F32))


PEER_SUB = 256
PEER_LANES = 256


def _peer_kernel(t_ref, x_ref, u_ref, v_ref, s1_ref, s2_ref, e1_ref, e2_ref, tau_ref, y_ref, *, eb):
    j = pl.program_id(1)
    tm = t_ref.shape[0]

    @pl.when(j == 0)
    def _():
        y_ref[...] = x_ref[...]

    for p in range(eb // PEER_SUB):
        es = slice(p * PEER_SUB, (p + 1) * PEER_SUB)
        a_t = lax.dot_general(u_ref[es, :], t_ref[...], (((1,), (1,)), ((), ())), preferred_element_type=F32)
        rows = []
        for r in range(PEER_SUB // N_KEYS):
            i1 = j * (eb // N_KEYS) + p * (PEER_SUB // N_KEYS) + r
            s1_rows = [s1_ref[h, pl.ds(i1, 1), :] for h in range(PEER_HEADS)]
            e1_rows = [e1_ref[h, pl.ds(i1, 1), :] for h in range(PEER_HEADS)]
            tiles = []
            for tc in range(tm // PEER_LANES):
                ls = slice(tc * PEER_LANES, (tc + 1) * PEER_LANES)
                gsum = None
                for h in range(PEER_HEADS):
                    sc = s1_rows[h][:, ls] + s2_ref[h, :, ls]
                    gh = jnp.where(sc >= tau_ref[h:h + 1, ls], e1_rows[h][:, ls] * e2_ref[h, :, ls], 0.0)
                    gsum = gh if gsum is None else gsum + gh
                tiles.append(jax.nn.gelu(a_t[r * N_KEYS:(r + 1) * N_KEYS, ls]) * gsum)
            rows.append(jnp.concatenate(tiles, axis=1))
        coef = jnp.concatenate(rows, axis=0)
        y_ref[...] += jnp.dot(coef.T.astype(BF16), v_ref[es, :], preferred_element_type=F32)


def peer_call(t_bf, x, u_bf, v_bf, layer, s1, s2, e1, e2, tau, tm=512, eb=512):
    t, d = t_bf.shape
    ne = u_bf.shape[1]
    once = pl.Buffered(1)
    big_spec = pl.BlockSpec((PEER_HEADS, N_KEYS, tm), lambda i, j: (0, 0, i), pipeline_mode=once)
    return pl.pallas_call(
        functools.partial(_peer_kernel, eb=eb),
        grid=(t // tm, ne // eb),
        in_specs=[pl.BlockSpec((tm, d), lambda i, j: (i, 0), pipeline_mode=once),
                  pl.BlockSpec((tm, d), lambda i, j: (i, 0), pipeline_mode=once),
                  pl.BlockSpec((None, eb, d), lambda i, j: (layer, j, 0)),
                  pl.BlockSpec((None, eb, d), lambda i, j: (layer, j, 0)),
                  big_spec, big_spec, big_spec, big_spec,
                  pl.BlockSpec((PEER_HEADS, tm), lambda i, j: (0, i), pipeline_mode=once)],
        out_specs=pl.BlockSpec((tm, d), lambda i, j: (i, 0)),
        out_shape=jax.ShapeDtypeStruct((t, d), F32),
        compiler_params=_cparams(("parallel", "arbitrary")), name="peer_dense",
    )(t_bf, x, u_bf, v_bf, s1, s2, e1, e2, tau)


def _gate_params(b_i, b_f, a_log, dt_bias):
    zeros = jnp.zeros((LANE - 2 * H_A - 2 * H_B,), F32)
    bias = jnp.concatenate([b_i.astype(F32), b_f.astype(F32), dt_bias.astype(F32), jnp.zeros((H_B,), F32), zeros])
    alog = jnp.concatenate([jnp.zeros((2 * H_A,), F32), a_log.astype(F32), jnp.zeros((H_B,), F32), zeros])
    return (bias.reshape(1, LANE), bias[:GATE_ROWS].reshape(GATE_ROWS, 1),
            alog.reshape(1, LANE), alog[:GATE_ROWS].reshape(GATE_ROWS, 1))


def _gates_rows(proj, c):
    na, nb = 2 * H_A, 2 * H_B
    g = jnp.concatenate([proj[:, G_OFF + LANE:G_OFF + LANE + na], proj[:, G_OFF + na:G_OFF + na + nb],
                         jnp.zeros((proj.shape[0], GATE_ROWS - na - nb), F32)], axis=1)
    return g.reshape(g.shape[0] // c, c, GATE_ROWS).transpose(0, 2, 1)


def _hgrn_lower_bounds(logits):
    p = jax.nn.softmax(logits.astype(F32), axis=0)
    cs = jnp.cumsum(p, axis=0)
    return cs - cs[0:1]


def kernel(x_prompt, x_sample, state_mlstm_C, state_mlstm_n, state_mlstm_m, state_gdn_S, state_gdn_conv,
           state_hgrn_S, norm_mix_g, w_in, mlstm_b_i, mlstm_b_f, mlstm_norm_g, gdn_conv_w, gdn_A_log,
           gdn_dt_bias, gdn_norm_g, hgrn_lb_logits, hgrn_norm_g, w_out, norm_ffn_g, peer_w_q, peer_subkeys,
           peer_u, peer_v, final_norm_g, *, chunk=64, hgrn_chunk=16, tm=512, tm_peer=512, eb_peer=512, tb=256):
    bp, lp, d = x_prompt.shape
    bs, ls, _ = x_sample.shape
    depth = w_in.shape[0]
    tp = bp * lp
    ts = bs * ls
    x = jnp.concatenate([x_prompt.reshape(tp, d), x_sample.reshape(ts, d)], axis=0).astype(F32)
    tm_in = next(c for c in (1024, 1088, tm) if (tp + ts) % c == 0)
    lbs = _hgrn_lower_bounds(hgrn_lb_logits)
    wb = 3 * H_B * DK_B
    u_all, v_all = peer_u.astype(BF16), peer_v.astype(BF16)
    wo_all, wq_all = w_out.astype(BF16), peer_w_q.astype(BF16)
    pseq = dict(row0=0, nb=bp, nchunk=lp // chunk, c=chunk, n_valid=chunk, nseq=1)
    sseq = dict(row0=0, nb=bs, nchunk=1, c=SAMPLE_PAD, n_valid=ls, nseq=SAMPLE_SEQS)
    a_init = [state_mlstm_C, state_mlstm_n, state_mlstm_m.reshape(depth, bs, 1, H_A)]
    st_ap = st_as = st_bp = st_bs = st_cp = st_cs = None
    bufs_p, bufs_s = [], []
    for l in range(depth):
        h = rmsnorm_call(x, norm_mix_g[l], BF16, tb=tb)
        proj = in_proj_call(h, w_in, l, tm=tm_in)
        proj_s = jnp.pad(proj[tp:].reshape(bs, ls, NP), ((0, 0), (0, SAMPLE_PAD - ls), (0, 0)))
        proj_s = proj_s.reshape(bs * SAMPLE_PAD, NP)
        brow, bcol, arow, acol = _gate_params(mlstm_b_i[l], mlstm_b_f[l], gdn_A_log[l], gdn_dt_bias[l])
        gt_p = _gates_rows(proj[:tp], chunk)
        gt_s = _gates_rows(proj_s, SAMPLE_PAD)
        mix_a, *st_ap = mlstm_call(proj, gt_p, brow, bcol, mlstm_norm_g[l], pseq, l, depth, None, None, st_ap)
        smp_a, *st_as = mlstm_call(proj_s, gt_s, brow, bcol, mlstm_norm_g[l], sseq, l, depth, a_init, None, st_as)
        gdn_w = (brow, bcol, arow, acol, gdn_conv_w[l], gdn_norm_g[l])
        mix_b, *st_bp = gdn_call(proj, gt_p, *gdn_w, pseq, l, depth, None, None, None, st_bp)
        smp_b, *st_bs = gdn_call(proj_s, gt_s, *gdn_w, sseq, l, depth, [state_gdn_S], state_gdn_conv, None, st_bs)
        mix_c, *st_cp = hgrn_call(proj, lbs[l], hgrn_norm_g[l], pseq, hgrn_chunk, l, depth, None, None, st_cp)
        smp_c, *st_cs = hgrn_call(proj_s, lbs[l], hgrn_norm_g[l], sseq, SAMPLE_PAD, l, depth, [state_hgrn_S],
                                  None, st_cs)
        valid = lambda m: m.reshape(bs, SAMPLE_PAD, m.shape[1])[:, :ls].reshape(ts, m.shape[1])
        mix_a, mix_b, mix_c = (m.at[tp:].set(valid(s)) for m, s in ((mix_a, smp_a), (mix_b, smp_b), (mix_c, smp_c)))
        bufs_p.append(jnp.stack([proj[(b + 1) * lp - (CONV_W - 1):(b + 1) * lp, B_OFF:B_OFF + wb]
                                 for b in range(bp)], axis=0))
        bufs_s.append(proj[tp:, B_OFF:B_OFF + wb].reshape(bs, ls, wb)[:, ls - (CONV_W - 1):])
        x = matmul_call([mix_a, mix_b, mix_c], wo_all, l, res=x, tm=tm, name="out_proj")
        t_bf = rmsnorm_call(x, norm_ffn_g[l], BF16, tb=tb)
        qry = matmul_call([t_bf], wq_all, l, tm=tm, name="peer_query")
        s1, s2, e1, e2, tau = route_call(qry, peer_subkeys[l], tb=tb)
        x = peer_call(t_bf, x, u_all, v_all, l, s1, s2, e1, e2, tau, tm=tm_peer, eb=eb_peer)
    y_prompt = rmsnorm_call(x, final_norm_g, F32, tb=tb, row0=0, rows=tp).reshape(bp, lp, d)
    y_sample = rmsnorm_call(x, final_norm_g, F32, tb=tb, row0=tp, rows=ts).reshape(bs, ls, d)
    m_p, m_s = st_ap[2].reshape(depth, bp, H_A), st_as[2].reshape(depth, bs, H_A)
    return (y_prompt, y_sample, st_ap[0], st_as[0], st_ap[1], st_as[1], m_p, m_s, st_bp[0], st_bs[0],
            jnp.stack(bufs_p, axis=0), jnp.stack(bufs_s, axis=0), st_cp[0], st_cs[0])
```

```python
import functools

import jax
import jax.numpy as jnp
from jax import lax
from jax.experimental import pallas as pl
from jax.experimental.pallas import tpu as pltpu

F32 = jnp.float32
BF16 = jnp.bfloat16
EPS = 1e-6
NEG_INF = float("-inf")

H_A, DK_A, DV_A = 6, 128, 256
H_B, DK_B, DV_B = 12, 128, 128
H_C, DK_C, DV_C = 8, 128, 128
CONV_W = 4
PEER_HEADS, N_KEYS, PEER_TOPK = 8, 128, 16
SAMPLE_PAD = 8
SAMPLE_SEQS = 4
LANE = 128
VMEM_LIMIT = 56 * 1024 * 1024

A_W = H_A * DV_A
B_OFF = 3 * A_W
IN_TN = 512
G_OFF = B_OFF + 4 * H_B * DV_B
C_OFF = G_OFF + IN_TN
NP = C_OFF + 4 * H_C * DV_C
GATE_ROWS = 40


def _cparams(sem):
    return pltpu.CompilerParams(dimension_semantics=sem, vmem_limit_bytes=VMEM_LIMIT)


def _mm(a, b):
    return jnp.dot(a.astype(BF16), b.astype(BF16), preferred_element_type=F32)


def _mmf_nt(a, b):
    return lax.dot_general(a, b, (((1,), (1,)), ((), ())), preferred_element_type=F32,
                           precision=lax.Precision.HIGHEST)


def _softplus(x):
    return jnp.maximum(x, 0.0) + jnp.log1p(jnp.exp(-jnp.abs(x)))


def _log_sigmoid(x):
    return -_softplus(-x)


def _sigmoid(x):
    return 1.0 / (1.0 + jnp.exp(-x))


def _silu(x):
    return x * _sigmoid(x)


def _head_norm(x, g_row):
    return x * lax.rsqrt(jnp.mean(x * x, axis=-1, keepdims=True) + EPS) * g_row


def _rms_kernel(x_ref, g_ref, h_ref):
    x = x_ref[...]
    y = x * lax.rsqrt(jnp.mean(x * x, axis=-1, keepdims=True) + EPS)
    h_ref[...] = (y * g_ref[...]).astype(h_ref.dtype)


def rmsnorm_call(x, g, out_dtype, tb=256, row0=0, rows=None):
    d = x.shape[1]
    rows = x.shape[0] if rows is None else rows
    rb = row0 // tb
    return pl.pallas_call(
        _rms_kernel, grid=(rows // tb,),
        in_specs=[pl.BlockSpec((tb, d), lambda i: (rb + i, 0)), pl.BlockSpec((1, d), lambda i: (0, 0))],
        out_specs=pl.BlockSpec((tb, d), lambda i: (i, 0)),
        out_shape=jax.ShapeDtypeStruct((rows, d), out_dtype),
        compiler_params=_cparams(("parallel",)), name="rmsnorm",
    )(x, g.reshape(1, d).astype(F32))


def _matmul_kernel(*refs, n_pairs, has_res):
    a_refs, w_refs = refs[:n_pairs], refs[n_pairs:2 * n_pairs]
    o_ref = refs[-1]
    acc = _mm(a_refs[0][...], w_refs[0][...])
    for a_ref, w_ref in zip(a_refs[1:], w_refs[1:]):
        acc = acc + _mm(a_ref[...], w_ref[...])
    if has_res:
        acc = acc + refs[2 * n_pairs][...]
    o_ref[...] = acc


def matmul_call(a_list, w, layer, res=None, tm=512, tn=512, name="matmul"):
    t = a_list[0].shape[0]
    n = w.shape[2]
    in_specs = [pl.BlockSpec((tm, a.shape[1]), lambda i, j: (i, 0)) for a in a_list]
    row = 0
    for a in a_list:
        k = a.shape[1]
        in_specs.append(pl.BlockSpec((None, k, tn), functools.partial(lambda i, j, rb: (layer, rb, j), rb=row // k)))
        row += k
    args = list(a_list) + [w] * len(a_list)
    if res is not None:
        in_specs.append(pl.BlockSpec((tm, tn), lambda i, j: (i, j)))
        args.append(res)
    return pl.pallas_call(
        functools.partial(_matmul_kernel, n_pairs=len(a_list), has_res=res is not None),
        grid=(t // tm, n // tn), in_specs=in_specs,
        out_specs=pl.BlockSpec((tm, tn), lambda i, j: (i, j)),
        out_shape=jax.ShapeDtypeStruct((t, n), F32),
        compiler_params=_cparams(("parallel", "parallel")), name=name,
    )(*args)


A_BLOCKS = B_OFF // IN_TN
G_BLOCK = G_OFF // IN_TN
B_SHIFT = 2 * H_A
C_SHIFT = 2 * H_A + 2 * H_B
W_ROWS = 512


def _in_proj_kernel(a_ref, wm_ref, we_ref, o_ref, wb_ref):
    j, i = pl.program_id(0), pl.program_id(1)
    k = wm_ref.shape[0]

    def convert(shift):
        def body(r, carry):
            rs = pl.ds(pl.multiple_of(r * W_ROWS, W_ROWS), W_ROWS)
            if shift == 0:
                blk = wm_ref[rs, :]
            else:
                blk = jnp.concatenate([wm_ref[rs, shift:], we_ref[rs, :shift]], axis=1)
            wb_ref[rs, :] = blk.astype(BF16)
            return carry
        lax.fori_loop(0, k // W_ROWS, body, 0)

    first = i == 0

    @pl.when(first & (j < A_BLOCKS))
    def _():
        convert(0)

    @pl.when(first & (j >= A_BLOCKS) & (j < G_BLOCK))
    def _():
        convert(B_SHIFT)

    @pl.when(first & (j == G_BLOCK))
    def _():
        wb_ref[:, 0:LANE] = wm_ref[:, 0:LANE].astype(BF16)
        wb_ref[:, LANE:2 * LANE] = we_ref[...].astype(BF16)
        wb_ref[:, 2 * LANE:] = jnp.zeros((k, IN_TN - 2 * LANE), BF16)

    @pl.when(first & (j > G_BLOCK))
    def _():
        convert(C_SHIFT)

    o_ref[...] = jnp.dot(a_ref[...], wb_ref[...], preferred_element_type=F32)


def in_proj_call(a, w_in, layer, tm):
    t, k = a.shape
    lanes_per_tile = IN_TN // LANE
    main_blk = lambda j: jnp.where(j <= G_BLOCK, j, j - 1)
    extra_blk = lambda j: jnp.where(j < G_BLOCK, lanes_per_tile * (j + 1),
                                    jnp.where(j == G_BLOCK, B_OFF // LANE, lanes_per_tile * j))
    return pl.pallas_call(
        _in_proj_kernel, grid=(NP // IN_TN, t // tm),
        in_specs=[pl.BlockSpec((tm, k), lambda j, i: (i, 0)),
                  pl.BlockSpec((None, k, IN_TN), lambda j, i: (layer, 0, main_blk(j))),
                  pl.BlockSpec((None, k, LANE), lambda j, i: (layer, 0, extra_blk(j)))],
        out_specs=pl.BlockSpec((tm, IN_TN), lambda j, i: (i, j)),
        out_shape=jax.ShapeDtypeStruct((t, NP), F32),
        scratch_shapes=[pltpu.VMEM((k, IN_TN), BF16)],
        compiler_params=_cparams(("parallel", "arbitrary")), name="in_proj",
    )(a, w_in, w_in)


def _tri_masks(c):
    row = lax.broadcasted_iota(jnp.int32, (c, c), 0)
    col = lax.broadcasted_iota(jnp.int32, (c, c), 1)
    return row, col


def _mlstm_kernel(qk_ref, v_ref, o_ref, gc_ref, gt_ref, brow_ref, bcol_ref, ng_ref, c0_ref, n0_ref, m0_ref,
                  out_ref, c_ref, n_ref, m_ref, *, c, nseq, n_valid, has_init):
    ci = pl.program_id(1)

    @pl.when(ci == 0)
    def _():
        if has_init:
            c_ref[...] = c0_ref[...]
            n_ref[...] = n0_ref[...]
            m_ref[...] = m0_ref[...]
        else:
            c_ref[...] = jnp.zeros_like(c_ref)
            n_ref[...] = jnp.zeros_like(n_ref)
            m_ref[...] = jnp.zeros_like(m_ref)

    gc = gc_ref[...] + brow_ref[...]
    gt = gt_ref[...] + bcol_ref[...]
    row, col = _tri_masks(c)
    tri = row >= col
    padded = n_valid < c
    if padded:
        valid_c = lax.broadcasted_iota(jnp.int32, (c, 1), 0) < n_valid
        valid_r = lax.broadcasted_iota(jnp.int32, (1, c), 1) < n_valid
    sh = [(s, h) for s in range(nseq) for h in range(H_A)]
    rows = lambda s: slice(s * c, (s + 1) * c)
    heads_c = lambda x, off: jnp.stack([x[rows(s), off + h:off + h + 1] for s, h in sh], axis=0)
    heads_r = lambda x, off: jnp.stack([x[s, off + h:off + h + 1, :] for s, h in sh], axis=0)
    ig_c, lf_c = heads_c(gc, 0), heads_c(_log_sigmoid(gc), H_A)
    ig_r, lf_r = heads_r(gt, 0), heads_r(_log_sigmoid(gt), H_A)
    if padded:
        ig_c, lf_c = jnp.where(valid_c, ig_c, NEG_INF), jnp.where(valid_c, lf_c, 0.0)
        ig_r, lf_r = jnp.where(valid_r, ig_r, NEG_INF), jnp.where(valid_r, lf_r, 0.0)
    q = jnp.stack([qk_ref[rows(s), h * DK_A:(h + 1) * DK_A] for s, h in sh], axis=0) * (DK_A ** -0.5)
    k = jnp.stack([qk_ref[rows(s), (H_A + h) * DK_A:(H_A + h + 1) * DK_A] for s, h in sh], axis=0)
    v = jnp.stack([v_ref[rows(s), h * DV_A:(h + 1) * DV_A] for s, h in sh], axis=0)
    cst = c_ref[...].reshape(nseq * H_A, DK_A, DV_A)
    nst = jnp.stack([n_ref[s, h:h + 1, :] for s, h in sh], axis=0)
    m_prev = jnp.stack([m_ref[s, :, h:h + 1] for s, h in sh], axis=0)
    b_c = jnp.sum(jnp.where(tri, lf_r, 0.0), axis=2, keepdims=True)
    b_r = jnp.sum(jnp.where(row <= col, lf_c, 0.0), axis=1, keepdims=True)
    log_d = jnp.where(tri, b_c - b_r + ig_r, NEG_INF)
    m_inter = b_c + m_prev
    m_t = jnp.maximum(m_inter, jnp.max(log_d, axis=2, keepdims=True))
    s = _bmm_nt(q, k) * jnp.exp(log_d - m_t)
    inter = jnp.exp(m_inter - m_t)
    num = _bmm(s, v) + inter * _bmm(q, cst)
    den = jnp.sum(s, axis=2, keepdims=True) + inter * jnp.sum(q * nst, axis=2, keepdims=True)
    hh = num / jnp.maximum(jnp.abs(den), jnp.exp(-m_t))
    m_new = m_t[:, c - 1:c, :]
    w = jnp.exp(b_c[:, c - 1:c, :] - b_c + ig_c - m_new)
    decay = inter[:, c - 1:c, :]
    kw = k * w
    c_ref[...] = (decay * cst + _bmm_tn(kw, v)).reshape(nseq, H_A, DK_A, DV_A)
    n_new = decay * nst + jnp.sum(kw, axis=1, keepdims=True)
    for i, (s, h) in enumerate(sh):
        hs = slice(h * DV_A, (h + 1) * DV_A)
        n_ref[s, h:h + 1, :] = n_new[i]
        m_ref[s, :, h:h + 1] = m_new[i]
        out_ref[rows(s), hs] = _head_norm(hh[i], ng_ref[:, hs]) * _sigmoid(o_ref[rows(s), hs])


def _mixer_call(body, name, proj, seqs, layer, depth, tok_blocks, extra_inputs, init_states, st_shapes,
                mix_w, prev_mix, prev_states, scratch_shapes=()):
    row0, nb, nchunk, c, nseq = seqs["row0"], seqs["nb"], seqs["nchunk"], seqs["c"], seqs["nseq"]
    assert nseq == 1 or nchunk == 1
    rows = nseq * c
    rb = row0 // rows
    tok = lambda b, ci: rb + b * nchunk + ci
    st_spec = lambda shape, lyr: pl.BlockSpec(
        (None, nseq) + shape, functools.partial(lambda b, ci, n: (lyr, b) + (0,) * n, n=len(shape)))
    in_specs = [pl.BlockSpec((rows, w), functools.partial(lambda b, ci, cb: (tok(b, ci), cb), cb=cb))
                for w, cb in tok_blocks]
    args = [proj] * len(tok_blocks)
    for arr, spec in extra_inputs:
        in_specs.append(spec)
        args.append(arr)
    if init_states is None:
        for shape in st_shapes:
            in_specs.append(pl.BlockSpec((None, nseq) + shape, functools.partial(
                lambda b, ci, n: (0,) * (n + 2), n=len(shape))))
            args.append(jnp.zeros((1, nseq) + shape, F32))
    else:
        for shape, arr in zip(st_shapes, init_states):
            in_specs.append(st_spec(shape, layer))
            args.append(arr)
    n_in = len(args)
    out_specs = [pl.BlockSpec((rows, mix_w), lambda b, ci: (tok(b, ci), 0))]
    out_shape = [jax.ShapeDtypeStruct((proj.shape[0], mix_w), F32)]
    for shape in st_shapes:
        out_specs.append(st_spec(shape, layer))
        out_shape.append(jax.ShapeDtypeStruct((depth, nb) + shape, F32))
    prev = [prev_mix] + (list(prev_states) if prev_states is not None else [None] * len(st_shapes))
    aliases = {}
    for k, arr in enumerate(prev):
        if arr is not None:
            aliases[len(args)] = k
            in_specs.append(pl.BlockSpec(memory_space=pl.ANY))
            args.append(arr)
    n_alias = len(args) - n_in

    def kernel_fn(*refs):
        body(*refs[:n_in], *refs[n_in + n_alias:])

    return pl.pallas_call(
        kernel_fn, grid=(nb // nseq, nchunk), in_specs=in_specs, out_specs=out_specs, out_shape=out_shape,
        input_output_aliases=aliases, scratch_shapes=list(scratch_shapes),
        compiler_params=_cparams(("parallel", "arbitrary")), name=name)(*args)


def _const_spec(shape):
    return pl.BlockSpec(shape, functools.partial(lambda b, ci, n: (0,) * n, n=len(shape)))


def _gates_spec(seqs):
    nchunk, c, nseq = seqs["nchunk"], seqs["c"], seqs["nseq"]
    return pl.BlockSpec((nseq, GATE_ROWS, c), lambda b, ci: (b * nchunk + ci, 0, 0))


def mlstm_call(proj, gates_t, brow, bcol, norm_g, seqs, layer, depth, init_states, prev_mix, prev_states):
    body = functools.partial(_mlstm_kernel, c=seqs["c"], nseq=seqs["nseq"], n_valid=seqs["n_valid"],
                             has_init=init_states is not None)
    extra = [(gates_t, _gates_spec(seqs)), (brow, _const_spec((1, LANE))), (bcol, _const_spec((GATE_ROWS, 1))),
             (norm_g.reshape(1, A_W).astype(F32), _const_spec((1, A_W)))]
    return _mixer_call(body, "mlstm", proj, seqs, layer, depth,
                       [(A_W, 0), (A_W, 1), (A_W, 2), (LANE, G_OFF // LANE + 1)], extra, init_states,
                       [(H_A, DK_A, DV_A), (H_A, DK_A), (1, H_A)], A_W, prev_mix, prev_states)


def _bdot(a, b, dims):
    return lax.dot_general(a, b, (dims, ((0,), (0,))), preferred_element_type=F32)


def _bmm(a, b):
    return _bdot(a.astype(BF16), b.astype(BF16), ((2,), (1,)))


def _bmm_nt(a, b):
    return _bdot(a.astype(BF16), b.astype(BF16), ((2,), (2,)))


def _bmm_tn(a, b):
    return _bdot(jnp.swapaxes(a, 1, 2).astype(BF16), b.astype(BF16), ((2,), (1,)))


def _unit_lower_inverse(a, row, col, c):
    d = jnp.where(row == col, 1.0, 0.0)
    s = 1
    while s < c:
        blk = jnp.where(((row // (2 * s)) == (col // (2 * s))) & ((row % (2 * s)) >= s) & ((col % (2 * s)) < s),
                        a, 0.0)
        if s == 1:
            d = d - blk
        else:
            d = d - _bmm(_bmm(d, blk), d)
        s *= 2
    return d


def _gdn_kernel(qkv_ref, z_ref, gc_ref, gt_ref, brow_ref, bcol_ref, arow_ref, acol_ref, cw_ref, buf0_ref, ng_ref,
                s0_ref, out_ref, s_ref, xp_ref, *, c, nseq, n_valid, has_init):
    ci = pl.program_id(1)
    hk = H_B * DK_B
    rows = lambda s: slice(s * c, (s + 1) * c)

    @pl.when(ci == 0)
    def _():
        xp_ref[:, 0:8, :] = jnp.zeros((nseq, 8, xp_ref.shape[2]), F32)
        if has_init:
            s_ref[...] = s0_ref[...]
            xp_ref[:, 8 - (CONV_W - 1):8, :] = buf0_ref[...]
        else:
            s_ref[...] = jnp.zeros_like(s_ref)

    @pl.when(ci > 0)
    def _():
        xp_ref[:, 0:8, :] = xp_ref[:, c:c + 8, :]

    for s in range(nseq):
        xp_ref[s, 8:8 + c, :] = qkv_ref[rows(s), :]

    gc = gc_ref[...] + brow_ref[...]
    gt = gt_ref[...] + bcol_ref[...]
    row, col = _tri_masks(c)
    tri = row >= col
    strict = row > col
    padded = n_valid < c
    if padded:
        valid_c = lax.broadcasted_iota(jnp.int32, (c, 1), 0) < n_valid
        valid_r = lax.broadcasted_iota(jnp.int32, (1, c), 1) < n_valid
    ga, gb = 2 * H_A, 2 * H_A + H_B

    sh = [(s, h) for s in range(nseq) for h in range(H_B)]

    def conv_act(c0):
        slabs = []
        for s, h in sh:
            cs = slice(c0 + h * DK_B, c0 + (h + 1) * DK_B)
            y = cw_ref[0:1, cs] * xp_ref[s, 8 - 3:8 - 3 + c, cs]
            for j in range(1, CONV_W):
                y = y + cw_ref[j:j + 1, cs] * xp_ref[s, 8 - 3 + j:8 - 3 + j + c, cs]
            slabs.append(_silu(y))
        return jnp.stack(slabs, axis=0)

    heads_c = lambda x, off: jnp.stack([x[rows(s), off + h:off + h + 1] for s, h in sh], axis=0)
    heads_r = lambda x, off: jnp.stack([x[s, off + h:off + h + 1, :] for s, h in sh], axis=0)
    g_c = heads_c(-jnp.exp(arow_ref[...]) * _softplus(gc), ga)
    g_r = heads_r(-jnp.exp(acol_ref[...]) * _softplus(gt), ga)
    beta = heads_c(_sigmoid(gc), gb)
    if padded:
        g_c, g_r = jnp.where(valid_c, g_c, 0.0), jnp.where(valid_r, g_r, 0.0)
        beta = jnp.where(valid_c, beta, 0.0)
    q = conv_act(0)
    k = conv_act(hk)
    v = conv_act(2 * hk)
    q = q * lax.rsqrt(jnp.sum(q * q, axis=2, keepdims=True) + EPS) * (DK_B ** -0.5)
    k = k * lax.rsqrt(jnp.sum(k * k, axis=2, keepdims=True) + EPS)
    sst = s_ref[...].reshape(nseq * H_B, DK_B, DV_B)
    gcum_c = jnp.sum(jnp.where(tri, g_r, 0.0), axis=2, keepdims=True)
    gcum_r = jnp.sum(jnp.where(row <= col, g_c, 0.0), axis=1, keepdims=True)
    gam = jnp.exp(jnp.where(tri, gcum_c - gcum_r, NEG_INF))
    eg = jnp.exp(gcum_c)
    a = jnp.where(strict, beta * _bmm_nt(k, k) * gam, 0.0)
    tinv = _unit_lower_inverse(a, row, col, c)
    rhs = jnp.concatenate([beta * v, (beta * eg) * k], axis=2)
    sol = _bmm(tinv, rhs)
    u = sol[:, :, :DV_B] - _bmm(sol[:, :, DV_B:], sst)
    o = eg * _bmm(q, sst) + _bmm(_bmm_nt(q, k) * gam, u)
    g_last = gcum_c[:, c - 1:c, :]
    s_new = jnp.exp(g_last) * sst + _bmm_tn(k * jnp.exp(g_last - gcum_c), u)
    s_ref[...] = s_new.reshape(nseq, H_B, DK_B, DV_B)
    for i, (s, h) in enumerate(sh):
        hs = slice(h * DV_B, (h + 1) * DV_B)
        out_ref[rows(s), hs] = _head_norm(o[i], ng_ref[:, hs]) * _silu(z_ref[rows(s), hs])


def gdn_call(proj, gates_t, brow, bcol, arow, acol, conv_w, norm_g, seqs, layer, depth, init_states, buf0,
             prev_mix, prev_states):
    wb = 3 * H_B * DK_B
    has_init = init_states is not None
    nseq = seqs["nseq"]
    body = functools.partial(_gdn_kernel, c=seqs["c"], nseq=nseq, n_valid=seqs["n_valid"], has_init=has_init)
    if has_init:
        buf_spec = pl.BlockSpec((None, nseq, CONV_W - 1, wb), lambda b, ci: (layer, b, 0, 0))
    else:
        buf0 = jnp.zeros((1, nseq, CONV_W - 1, wb), F32)
        buf_spec = pl.BlockSpec((None, nseq, CONV_W - 1, wb), lambda b, ci: (0, 0, 0, 0))
    extra = [(gates_t, _gates_spec(seqs)), (brow, _const_spec((1, LANE))), (bcol, _const_spec((GATE_ROWS, 1))),
             (arow, _const_spec((1, LANE))), (acol, _const_spec((GATE_ROWS, 1))),
             (conv_w.astype(F32), _const_spec((CONV_W, wb))), (buf0, buf_spec),
             (norm_g.reshape(1, H_B * DV_B).astype(F32), _const_spec((1, H_B * DV_B)))]
    return _mixer_call(body, "gdn", proj, seqs, layer, depth,
                       [(wb, B_OFF // wb), (A_W, (B_OFF + wb) // A_W), (LANE, G_OFF // LANE)], extra, init_states,
                       [(H_B, DK_B, DV_B)], H_B * DV_B, prev_mix, prev_states,
                       scratch_shapes=[pltpu.VMEM((nseq, seqs["c"] + 8, wb), F32)])


def _hgrn_kernel(q_ref, f_ref, i_ref, g_ref, lb_ref, ng_ref, s0_ref, out_ref, s_ref, st_ref, *,
                 c, sub, nseq, n_valid, has_init, nchunk):
    ci = pl.program_id(1)

    sh = [(s, h) for s in range(nseq) for h in range(H_C)]

    @pl.when(ci == 0)
    def _():
        for i, (s, h) in enumerate(sh):
            st_ref[i] = s0_ref[s, h].T if has_init else jnp.zeros((DV_C, DK_C), F32)

    padded = n_valid < sub
    rowi = lax.broadcasted_iota(jnp.int32, (1, sub, 1), 1)
    heads = lambda ref, j: jnp.stack(
        [ref[s * c + j * sub:s * c + (j + 1) * sub, h * DK_C:(h + 1) * DK_C] for s, h in sh], axis=0)
    lb = jnp.stack([lb_ref[:, h * DK_C:(h + 1) * DK_C] for s, h in sh], axis=0)
    log_1mlb, log_lb = jnp.log1p(-lb), jnp.log(lb)
    sst = st_ref[...]
    for j in range(c // sub):
        cf = heads(f_ref, j)
        la = log_1mlb + _log_sigmoid(cf)
        logf = jnp.maximum(la, log_lb) + jnp.log1p(jnp.exp(-jnp.abs(la - log_lb)))
        k = (1.0 - lb) * _sigmoid(-cf)
        if padded:
            logf, k = jnp.where(rowi < n_valid, logf, 0.0), jnp.where(rowi < n_valid, k, 0.0)
        q = _silu(heads(q_ref, j))
        iv = heads(i_ref, j)
        bc = jnp.zeros((nseq * H_C, sub, DK_C), F32)
        for t in range(sub):
            bc = bc + jnp.where(rowi >= t, logf[:, t:t + 1, :], 0.0)
        o = _bmm_nt(q * jnp.exp(bc), sst)
        for t in range(sub):
            dec = jnp.exp(jnp.where(rowi >= t, bc - bc[:, t:t + 1, :], NEG_INF))
            att = jnp.sum(q * dec * k[:, t:t + 1, :], axis=2, keepdims=True)
            o = o + att * iv[:, t:t + 1, :]
        b_last = bc[:, sub - 1:sub, :]
        sst = jnp.exp(b_last) * sst + _bmm_tn(iv, k * jnp.exp(b_last - bc))
        for i, (s, h) in enumerate(sh):
            rows = slice(s * c + j * sub, s * c + (j + 1) * sub)
            hs = slice(h * DK_C, (h + 1) * DK_C)
            out_ref[rows, hs] = _head_norm(o[i], ng_ref[:, hs]) * _silu(g_ref[rows, hs])
    st_ref[...] = sst

    @pl.when(ci == nchunk - 1)
    def _():
        for i, (s, h) in enumerate(sh):
            s_ref[s, h] = st_ref[i].T


def hgrn_call(proj, lb, norm_g, seqs, sub, layer, depth, init_states, prev_mix, prev_states):
    w = H_C * DK_C
    cb = C_OFF // w
    body = functools.partial(_hgrn_kernel, c=seqs["c"], sub=sub, nseq=seqs["nseq"],
                             n_valid=min(seqs["n_valid"], sub),
                             has_init=init_states is not None, nchunk=seqs["nchunk"])
    extra = [(lb.reshape(1, w).astype(F32), _const_spec((1, w))),
             (norm_g.reshape(1, w).astype(F32), _const_spec((1, w)))]
    return _mixer_call(body, "hgrn", proj, seqs, layer, depth,
                       [(w, cb), (w, cb + 1), (w, cb + 2), (w, cb + 3)], extra, init_states,
                       [(H_C, DK_C, DV_C)], w, prev_mix, prev_states,
                       scratch_shapes=[pltpu.VMEM((seqs["nseq"] * H_C, DV_C, DK_C), F32)])


def _top_values(x, k):
    rows = []
    for _ in range(k):
        mx = jnp.max(x, axis=0, keepdims=True)
        rows.append(mx)
        x = jnp.where(x == mx, NEG_INF, x)
    return jnp.concatenate(rows, axis=0)


def _route_kernel(q_ref, sk_ref, s1_ref, s2_ref, e1_ref, e2_ref, tau_ref):
    for h in range(PEER_HEADS):
        s1 = _mmf_nt(sk_ref[h, 0], q_ref[:, (2 * h) * LANE:(2 * h + 1) * LANE])
        s2 = _mmf_nt(sk_ref[h, 1], q_ref[:, (2 * h + 1) * LANE:(2 * h + 2) * LANE])
        a = _top_values(s1, PEER_TOPK)
        b = _top_values(s2, PEER_TOPK)
        cand = jnp.concatenate([a[0:1, :] + b] + [a[i:i + 1, :] + b[:PEER_TOPK // 2, :]
                                                  for i in range(1, PEER_TOPK)], axis=0)
        best = _top_values(cand, PEER_TOPK)
        z = jnp.sum(jnp.exp(best - best[0:1, :]), axis=0, keepdims=True)
        s1_ref[h] = s1
        s2_ref[h] = s2
        e1_ref[h] = jnp.exp(s1 - a[0:1, :])
        e2_ref[h] = jnp.exp(s2 - b[0:1, :]) / z
        tau_ref[h:h + 1, :] = best[PEER_TOPK - 1:PEER_TOPK, :]


def route_call(qry, subkeys, tb=256):
    t = qry.shape[0]
    big = jax.ShapeDtypeStruct((PEER_HEADS, N_KEYS, t), F32)
    big_spec = pl.BlockSpec((PEER_HEADS, N_KEYS, tb), lambda i: (0, 0, i))
    return pl.pallas_call(
        _route_kernel, grid=(t // tb,),
        in_specs=[pl.BlockSpec((tb, qry.shape[1]), lambda i: (i, 0)),
                  pl.BlockSpec(subkeys.shape, lambda i: (0, 0, 0, 0))],
        out_specs=[big_spec] * 4 + [pl.BlockSpec((PEER_HEADS, tb), lambda i: (0, i))],
        out_shape=[big] * 4 + [jax.ShapeDtypeStruct((PEER_HEADS, t), F32)],
        compiler_params=_cparams(("parallel",)), name="peer_route",
    )(qry, subkeys.astype(F32))


PEER_SUB = 256
PEER_LANES = 256


def _peer_kernel(t_ref, x_ref, u_ref, v_ref, s1_ref, s2_ref, e1_ref, e2_ref, tau_ref, y_ref, *, eb):
    j = pl.program_id(1)
    tm = t_ref.shape[0]

    @pl.when(j == 0)
    def _():
        y_ref[...] = x_ref[...]

    for p in range(eb // PEER_SUB):
        es = slice(p * PEER_SUB, (p + 1) * PEER_SUB)
        a_t = lax.dot_general(u_ref[es, :], t_ref[...], (((1,), (1,)), ((), ())), preferred_element_type=F32)
        rows = []
        for r in range(PEER_SUB // N_KEYS):
            i1 = j * (eb // N_KEYS) + p * (PEER_SUB // N_KEYS) + r
            s1_rows = [s1_ref[h, pl.ds(i1, 1), :] for h in range(PEER_HEADS)]
            e1_rows = [e1_ref[h, pl.ds(i1, 1), :] for h in range(PEER_HEADS)]
            tiles = []
            for tc in range(tm // PEER_LANES):
                ls = slice(tc * PEER_LANES, (tc + 1) * PEER_LANES)
                gsum = None
                for h in range(PEER_HEADS):
                    sc = s1_rows[h][:, ls] + s2_ref[h, :, ls]
                    gh = jnp.where(sc >= tau_ref[h:h + 1, ls], e1_rows[h][:, ls] * e2_ref[h, :, ls], 0.0)
                    gsum = gh if gsum is None else gsum + gh
                tiles.append(jax.nn.gelu(a_t[r * N_KEYS:(r + 1) * N_KEYS, ls]) * gsum)
            rows.append(jnp.concatenate(tiles, axis=1))
        coef = jnp.concatenate(rows, axis=0)
        y_ref[...] += jnp.dot(coef.T.astype(BF16), v_ref[es, :], preferred_element_type=F32)


def peer_call(t_bf, x, u_bf, v_bf, layer, s1, s2, e1, e2, tau, tm=512, eb=512):
    t, d = t_bf.shape
    ne = u_bf.shape[1]
    once = pl.Buffered(1)
    big_spec = pl.BlockSpec((PEER_HEADS, N_KEYS, tm), lambda i, j: (0, 0, i), pipeline_mode=once)
    return pl.pallas_call(
        functools.partial(_peer_kernel, eb=eb),
        grid=(t // tm, ne // eb),
        in_specs=[pl.BlockSpec((tm, d), lambda i, j: (i, 0), pipeline_mode=once),
                  pl.BlockSpec((tm, d), lambda i, j: (i, 0), pipeline_mode=once),
                  pl.BlockSpec((None, eb, d), lambda i, j: (layer, j, 0)),
                  pl.BlockSpec((None, eb, d), lambda i, j: (layer, j, 0)),
                  big_spec, big_spec, big_spec, big_spec,
                  pl.BlockSpec((PEER_HEADS, tm), lambda i, j: (0, i), pipeline_mode=once)],
        out_specs=pl.BlockSpec((tm, d), lambda i, j: (i, 0)),
        out_shape=jax.ShapeDtypeStruct((t, d), F32),
        compiler_params=_cparams(("parallel", "arbitrary")), name="peer_dense",
    )(t_bf, x, u_bf, v_bf, s1, s2, e1, e2, tau)


def _gate_params(b_i, b_f, a_log, dt_bias):
    zeros = jnp.zeros((LANE - 2 * H_A - 2 * H_B,), F32)
    bias = jnp.concatenate([b_i.astype(F32), b_f.astype(F32), dt_bias.astype(F32), jnp.zeros((H_B,), F32), zeros])
    alog = jnp.concatenate([jnp.zeros((2 * H_A,), F32), a_log.astype(F32), jnp.zeros((H_B,), F32), zeros])
    return (bias.reshape(1, LANE), bias[:GATE_ROWS].reshape(GATE_ROWS, 1),
            alog.reshape(1, LANE), alog[:GATE_ROWS].reshape(GATE_ROWS, 1))


def _gates_rows(proj, c):
    na, nb = 2 * H_A, 2 * H_B
    g = jnp.concatenate([proj[:, G_OFF + LANE:G_OFF + LANE + na], proj[:, G_OFF + na:G_OFF + na + nb],
                         jnp.zeros((proj.shape[0], GATE_ROWS - na - nb), F32)], axis=1)
    return g.reshape(g.shape[0] // c, c, GATE_ROWS).transpose(0, 2, 1)


def _hgrn_lower_bounds(logits):
    p = jax.nn.softmax(logits.astype(F32), axis=0)
    cs = jnp.cumsum(p, axis=0)
    return cs - cs[0:1]


def kernel(x_prompt, x_sample, state_mlstm_C, state_mlstm_n, state_mlstm_m, state_gdn_S, state_gdn_conv,
           state_hgrn_S, norm_mix_g, w_in, mlstm_b_i, mlstm_b_f, mlstm_norm_g, gdn_conv_w, gdn_A_log,
           gdn_dt_bias, gdn_norm_g, hgrn_lb_logits, hgrn_norm_g, w_out, norm_ffn_g, peer_w_q, peer_subkeys,
           peer_u, peer_v, final_norm_g, *, chunk=64, hgrn_chunk=16, tm=512, tm_peer=512, eb_peer=512, tb=256):
    bp, lp, d = x_prompt.shape
    bs, ls, _ = x_sample.shape
    depth = w_in.shape[0]
    tp = bp * lp
    ts = bs * ls
    x = jnp.concatenate([x_prompt.reshape(tp, d), x_sample.reshape(ts, d)], axis=0).astype(F32)
    tm_in = next(c for c in (1024, 1088, tm) if (tp + ts) % c == 0)
    lbs = _hgrn_lower_bounds(hgrn_lb_logits)
    wb = 3 * H_B * DK_B
    u_all, v_all = peer_u.astype(BF16), peer_v.astype(BF16)
    wo_all, wq_all = w_out.astype(BF16), peer_w_q.astype(BF16)
    pseq = dict(row0=0, nb=bp, nchunk=lp // chunk, c=chunk, n_valid=chunk, nseq=1)
    sseq = dict(row0=0, nb=bs, nchunk=1, c=SAMPLE_PAD, n_valid=ls, nseq=SAMPLE_SEQS)
    a_init = [state_mlstm_C, state_mlstm_n, state_mlstm_m.reshape(depth, bs, 1, H_A)]
    st_ap = st_as = st_bp = st_bs = st_cp = st_cs = None
    bufs_p, bufs_s = [], []
    for l in range(depth):
        h = rmsnorm_call(x, norm_mix_g[l], BF16, tb=tb)
        proj = in_proj_call(h, w_in, l, tm=tm_in)
        proj_s = jnp.pad(proj[tp:].reshape(bs, ls, NP), ((0, 0), (0, SAMPLE_PAD - ls), (0, 0)))
        proj_s = proj_s.reshape(bs * SAMPLE_PAD, NP)
        brow, bcol, arow, acol = _gate_params(mlstm_b_i[l], mlstm_b_f[l], gdn_A_log[l], gdn_dt_bias[l])
        gt_p = _gates_rows(proj[:tp], chunk)
        gt_s = _gates_rows(proj_s, SAMPLE_PAD)
        mix_a, *st_ap = mlstm_call(proj, gt_p, brow, bcol, mlstm_norm_g[l], pseq, l, depth, None, None, st_ap)
        smp_a, *st_as = mlstm_call(proj_s, gt_s, brow, bcol, mlstm_norm_g[l], sseq, l, depth, a_init, None, st_as)
        gdn_w = (brow, bcol, arow, acol, gdn_conv_w[l], gdn_norm_g[l])
        mix_b, *st_bp = gdn_call(proj, gt_p, *gdn_w, pseq, l, depth, None, None, None, st_bp)
        smp_b, *st_bs = gdn_call(proj_s, gt_s, *gdn_w, sseq, l, depth, [state_gdn_S], state_gdn_conv, None, st_bs)
        mix_c, *st_cp = hgrn_call(proj, lbs[l], hgrn_norm_g[l], pseq, hgrn_chunk, l, depth, None, None, st_cp)
        smp_c, *st_cs = hgrn_call(proj_s, lbs[l], hgrn_norm_g[l], sseq, SAMPLE_PAD, l, depth, [state_hgrn_S],
                                  None, st_cs)
        valid = lambda m: m.reshape(bs, SAMPLE_PAD, m.shape[1])[:, :ls].reshape(ts, m.shape[1])
        mix_a, mix_b, mix_c = (m.at[tp:].set(valid(s)) for m, s in ((mix_a, smp_a), (mix_b, smp_b), (mix_c, smp_c)))
        bufs_p.append(jnp.stack([proj[(b + 1) * lp - (CONV_W - 1):(b + 1) * lp, B_OFF:B_OFF + wb]
                                 for b in range(bp)], axis=0))
        bufs_s.append(proj[tp:, B_OFF:B_OFF + wb].reshape(bs, ls, wb)[:, ls - (CONV_W - 1):])
        x = matmul_call([mix_a, mix_b, mix_c], wo_all, l, res=x, tm=tm, name="out_proj")
        t_bf = rmsnorm_call(x, norm_ffn_g[l], BF16, tb=tb)
        qry = matmul_call([t_bf], wq_all, l, tm=tm, name="peer_query")
        s1, s2, e1, e2, tau = route_call(qry, peer_subkeys[l], tb=tb)
        x = peer_call(t_bf, x, u_all, v_all, l, s1, s2, e1, e2, tau, tm=tm_peer, eb=eb_peer)
    y_prompt = rmsnorm_call(x, final_norm_g, F32, tb=tb, row0=0, rows=tp).reshape(bp, lp, d)
    y_sample = rmsnorm_call(x, final_norm_g, F32, tb=tb, row0=tp, rows=ts).reshape(bs, ls, d)
    m_p, m_s = st_ap[2].reshape(depth, bp, H_A), st_as[2].reshape(depth, bs, H_A)
    return (y_prompt, y_sample, st_ap[0], st_as[0], st_ap[1], st_as[1], m_p, m_s, st_bp[0], st_bs[0],
            jnp.stack(bufs_p, axis=0), jnp.stack(bufs_s, axis=0), st_cp[0], st_cs[0])
```

```python
import functools

import jax
import jax.numpy as jnp
from jax import lax
from jax.experimental import pallas as pl
from jax.experimental.pallas import tpu as pltpu

F32 = jnp.float32
BF16 = jnp.bfloat16
EPS = 1e-6
NEG_INF = float("-inf")

H_A, DK_A, DV_A = 6, 128, 256
H_B, DK_B, DV_B = 12, 128, 128
H_C, DK_C, DV_C = 8, 128, 128
CONV_W = 4
PEER_HEADS, N_KEYS, PEER_TOPK = 8, 128, 16
SAMPLE_PAD = 8
SAMPLE_SEQS = 4
LANE = 128
VMEM_LIMIT = 56 * 1024 * 1024

A_W = H_A * DV_A
B_OFF = 3 * A_W
IN_TN = 512
G_OFF = B_OFF + 4 * H_B * DV_B
C_OFF = G_OFF + IN_TN
NP = C_OFF + 4 * H_C * DV_C
GATE_ROWS = 40


def _cparams(sem):
    return pltpu.CompilerParams(dimension_semantics=sem, vmem_limit_bytes=VMEM_LIMIT)


def _mm(a, b):
    return jnp.dot(a.astype(BF16), b.astype(BF16), preferred_element_type=F32)


def _mmf_nt(a, b):
    return lax.dot_general(a, b, (((1,), (1,)), ((), ())), preferred_element_type=F32,
                           precision=lax.Precision.HIGHEST)


def _softplus(x):
    return jnp.maximum(x, 0.0) + jnp.log1p(jnp.exp(-jnp.abs(x)))


def _log_sigmoid(x):
    return -_softplus(-x)


def _sigmoid(x):
    return 1.0 / (1.0 + jnp.exp(-x))


def _silu(x):
    return x * _sigmoid(x)


def _head_norm(x, g_row):
    return x * lax.rsqrt(jnp.mean(x * x, axis=-1, keepdims=True) + EPS) * g_row


def _rms_kernel(x_ref, g_ref, h_ref):
    x = x_ref[...]
    y = x * lax.rsqrt(jnp.mean(x * x, axis=-1, keepdims=True) + EPS)
    h_ref[...] = (y * g_ref[...]).astype(h_ref.dtype)


def rmsnorm_call(x, g, out_dtype, tb=256, row0=0, rows=None):
    d = x.shape[1]
    rows = x.shape[0] if rows is None else rows
    rb = row0 // tb
    return pl.pallas_call(
        _rms_kernel, grid=(rows // tb,),
        in_specs=[pl.BlockSpec((tb, d), lambda i: (rb + i, 0)), pl.BlockSpec((1, d), lambda i: (0, 0))],
        out_specs=pl.BlockSpec((tb, d), lambda i: (i, 0)),
        out_shape=jax.ShapeDtypeStruct((rows, d), out_dtype),
        compiler_params=_cparams(("parallel",)), name="rmsnorm",
    )(x, g.reshape(1, d).astype(F32))


def _matmul_kernel(*refs, n_pairs, has_res):
    a_refs, w_refs = refs[:n_pairs], refs[n_pairs:2 * n_pairs]
    o_ref = refs[-1]
    acc = _mm(a_refs[0][...], w_refs[0][...])
    for a_ref, w_ref in zip(a_refs[1:], w_refs[1:]):
        acc = acc + _mm(a_ref[...], w_ref[...])
    if has_res:
        acc = acc + refs[2 * n_pairs][...]
    o_ref[...] = acc


def matmul_call(a_list, w, layer, res=None, tm=512, tn=512, name="matmul"):
    t = a_list[0].shape[0]
    n = w.shape[2]
    in_specs = [pl.BlockSpec((tm, a.shape[1]), lambda i, j: (i, 0)) for a in a_list]
    row = 0
    for a in a_list:
        k = a.shape[1]
        in_specs.append(pl.BlockSpec((None, k, tn), functools.partial(lambda i, j, rb: (layer, rb, j), rb=row // k)))
        row += k
    args = list(a_list) + [w] * len(a_list)
    if res is not None:
        in_specs.append(pl.BlockSpec((tm, tn), lambda i, j: (i, j)))
        args.append(res)
    return pl.pallas_call(
        functools.partial(_matmul_kernel, n_pairs=len(a_list), has_res=res is not None),
        grid=(t // tm, n // tn), in_specs=in_specs,
        out_specs=pl.BlockSpec((tm, tn), lambda i, j: (i, j)),
        out_shape=jax.ShapeDtypeStruct((t, n), F32),
        compiler_params=_cparams(("parallel", "parallel")), name=name,
    )(*args)


A_BLOCKS = B_OFF // IN_TN
G_BLOCK = G_OFF // IN_TN
B_SHIFT = 2 * H_A
C_SHIFT = 2 * H_A + 2 * H_B
W_ROWS = 512


def _in_proj_kernel(a_ref, wm_ref, we_ref, o_ref, wb_ref):
    j, i = pl.program_id(0), pl.program_id(1)
    k = wm_ref.shape[0]

    def convert(shift):
        def body(r, carry):
            rs = pl.ds(pl.multiple_of(r * W_ROWS, W_ROWS), W_ROWS)
            if shift == 0:
                blk = wm_ref[rs, :]
            else:
                blk = jnp.concatenate([wm_ref[rs, shift:], we_ref[rs, :shift]], axis=1)
            wb_ref[rs, :] = blk.astype(BF16)
            return carry
        lax.fori_loop(0, k // W_ROWS, body, 0)

    first = i == 0

    @pl.when(first & (j < A_BLOCKS))
    def _():
        convert(0)

    @pl.when(first & (j >= A_BLOCKS) & (j < G_BLOCK))
    def _():
        convert(B_SHIFT)

    @pl.when(first & (j == G_BLOCK))
    def _():
        wb_ref[:, 0:LANE] = wm_ref[:, 0:LANE].astype(BF16)
        wb_ref[:, LANE:2 * LANE] = we_ref[...].astype(BF16)
        wb_ref[:, 2 * LANE:] = jnp.zeros((k, IN_TN - 2 * LANE), BF16)

    @pl.when(first & (j > G_BLOCK))
    def _():
        convert(C_SHIFT)

    o_ref[...] = jnp.dot(a_ref[...], wb_ref[...], preferred_element_type=F32)


def in_proj_call(a, w_in, layer, tm):
    t, k = a.shape
    lanes_per_tile = IN_TN // LANE
    main_blk = lambda j: jnp.where(j <= G_BLOCK, j, j - 1)
    extra_blk = lambda j: jnp.where(j < G_BLOCK, lanes_per_tile * (j + 1),
                                    jnp.where(j == G_BLOCK, B_OFF // LANE, lanes_per_tile * j))
    return pl.pallas_call(
        _in_proj_kernel, grid=(NP // IN_TN, t // tm),
        in_specs=[pl.BlockSpec((tm, k), lambda j, i: (i, 0)),
                  pl.BlockSpec((None, k, IN_TN), lambda j, i: (layer, 0, main_blk(j))),
                  pl.BlockSpec((None, k, LANE), lambda j, i: (layer, 0, extra_blk(j)))],
        out_specs=pl.BlockSpec((tm, IN_TN), lambda j, i: (i, j)),
        out_shape=jax.ShapeDtypeStruct((t, NP), F32),
        scratch_shapes=[pltpu.VMEM((k, IN_TN), BF16)],
        compiler_params=_cparams(("parallel", "arbitrary")), name="in_proj",
    )(a, w_in, w_in)


def _tri_masks(c):
    row = lax.broadcasted_iota(jnp.int32, (c, c), 0)
    col = lax.broadcasted_iota(jnp.int32, (c, c), 1)
    return row, col


def _mlstm_kernel(qk_ref, v_ref, o_ref, gc_ref, gt_ref, brow_ref, bcol_ref, ng_ref, c0_ref, n0_ref, m0_ref,
                  out_ref, c_ref, n_ref, m_ref, *, c, nseq, n_valid, has_init):
    ci = pl.program_id(1)

    @pl.when(ci == 0)
    def _():
        if has_init:
            c_ref[...] = c0_ref[...]
            n_ref[...] = n0_ref[...]
            m_ref[...] = m0_ref[...]
        else:
            c_ref[...] = jnp.zeros_like(c_ref)
            n_ref[...] = jnp.zeros_like(n_ref)
            m_ref[...] = jnp.zeros_like(m_ref)

    gc = gc_ref[...] + brow_ref[...]
    gt = gt_ref[...] + bcol_ref[...]
    row, col = _tri_masks(c)
    tri = row >= col
    padded = n_valid < c
    if padded:
        valid_c = lax.broadcasted_iota(jnp.int32, (c, 1), 0) < n_valid
        valid_r = lax.broadcasted_iota(jnp.int32, (1, c), 1) < n_valid
    sh = [(s, h) for s in range(nseq) for h in range(H_A)]
    rows = lambda s: slice(s * c, (s + 1) * c)
    heads_c = lambda x, off: jnp.stack([x[rows(s), off + h:off + h + 1] for s, h in sh], axis=0)
    heads_r = lambda x, off: jnp.stack([x[s, off + h:off + h + 1, :] for s, h in sh], axis=0)
    ig_c, lf_c = heads_c(gc, 0), heads_c(_log_sigmoid(gc), H_A)
    ig_r, lf_r = heads_r(gt, 0), heads_r(_log_sigmoid(gt), H_A)
    if padded:
        ig_c, lf_c = jnp.where(valid_c, ig_c, NEG_INF), jnp.where(valid_c, lf_c, 0.0)
        ig_r, lf_r = jnp.where(valid_r, ig_r, NEG_INF), jnp.where(valid_r, lf_r, 0.0)
    q = jnp.stack([qk_ref[rows(s), h * DK_A:(h + 1) * DK_A] for s, h in sh], axis=0) * (DK_A ** -0.5)
    k = jnp.stack([qk_ref[rows(s), (H_A + h) * DK_A:(H_A + h + 1) * DK_A] for s, h in sh], axis=0)
    v = jnp.stack([v_ref[rows(s), h * DV_A:(h + 1) * DV_A] for s, h in sh], axis=0)
    cst = c_ref[...].reshape(nseq * H_A, DK_A, DV_A)
    nst = jnp.stack([n_ref[s, h:h + 1, :] for s, h in sh], axis=0)
    m_prev = jnp.stack([m_ref[s, :, h:h + 1] for s, h in sh], axis=0)
    b_c = jnp.sum(jnp.where(tri, lf_r, 0.0), axis=2, keepdims=True)
    b_r = jnp.sum(jnp.where(row <= col, lf_c, 0.0), axis=1, keepdims=True)
    log_d = jnp.where(tri, b_c - b_r + ig_r, NEG_INF)
    m_inter = b_c + m_prev
    m_t = jnp.maximum(m_inter, jnp.max(log_d, axis=2, keepdims=True))
    s = _bmm_nt(q, k) * jnp.exp(log_d - m_t)
    inter = jnp.exp(m_inter - m_t)
    num = _bmm(s, v) + inter * _bmm(q, cst)
    den = jnp.sum(s, axis=2, keepdims=True) + inter * jnp.sum(q * nst, axis=2, keepdims=True)
    hh = num / jnp.maximum(jnp.abs(den), jnp.exp(-m_t))
    m_new = m_t[:, c - 1:c, :]
    w = jnp.exp(b_c[:, c - 1:c, :] - b_c + ig_c - m_new)
    decay = inter[:, c - 1:c, :]
    kw = k * w
    c_ref[...] = (decay * cst + _bmm_tn(kw, v)).reshape(nseq, H_A, DK_A, DV_A)
    n_new = decay * nst + jnp.sum(kw, axis=1, keepdims=True)
    for i, (s, h) in enumerate(sh):
        hs = slice(h * DV_A, (h + 1) * DV_A)
        n_ref[s, h:h + 1, :] = n_new[i]
        m_ref[s, :, h:h + 1] = m_new[i]
        out_ref[rows(s), hs] = _head_norm(hh[i], ng_ref[:, hs]) * _sigmoid(o_ref[rows(s), hs])


def _mixer_call(body, name, proj, seqs, layer, depth, tok_blocks, extra_inputs, init_states, st_shapes,
                mix_w, prev_mix, prev_states, scratch_shapes=()):
    row0, nb, nchunk, c, nseq = seqs["row0"], seqs["nb"], seqs["nchunk"], seqs["c"], seqs["nseq"]
    assert nseq == 1 or nchunk == 1
    rows = nseq * c
    rb = row0 // rows
    tok = lambda b, ci: rb + b * nchunk + ci
    st_spec = lambda shape, lyr: pl.BlockSpec(
        (None, nseq) + shape, functools.partial(lambda b, ci, n: (lyr, b) + (0,) * n, n=len(shape)))
    in_specs = [pl.BlockSpec((rows, w), functools.partial(lambda b, ci, cb: (tok(b, ci), cb), cb=cb))
                for w, cb in tok_blocks]
    args = [proj] * len(tok_blocks)
    for arr, spec in extra_inputs:
        in_specs.append(spec)
        args.append(arr)
    if init_states is None:
        for shape in st_shapes:
            in_specs.append(pl.BlockSpec((None, nseq) + shape, functools.partial(
                lambda b, ci, n: (0,) * (n + 2), n=len(shape))))
            args.append(jnp.zeros((1, nseq) + shape, F32))
    else:
        for shape, arr in zip(st_shapes, init_states):
            in_specs.append(st_spec(shape, layer))
            args.append(arr)
    n_in = len(args)
    out_specs = [pl.BlockSpec((rows, mix_w), lambda b, ci: (tok(b, ci), 0))]
    out_shape = [jax.ShapeDtypeStruct((proj.shape[0], mix_w), F32)]
    for shape in st_shapes:
        out_specs.append(st_spec(shape, layer))
        out_shape.append(jax.ShapeDtypeStruct((depth, nb) + shape, F32))
    prev = [prev_mix] + (list(prev_states) if prev_states is not None else [None] * len(st_shapes))
    aliases = {}
    for k, arr in enumerate(prev):
        if arr is not None:
            aliases[len(args)] = k
            in_specs.append(pl.BlockSpec(memory_space=pl.ANY))
            args.append(arr)
    n_alias = len(args) - n_in

    def kernel_fn(*refs):
        body(*refs[:n_in], *refs[n_in + n_alias:])

    return pl.pallas_call(
        kernel_fn, grid=(nb // nseq, nchunk), in_specs=in_specs, out_specs=out_specs, out_shape=out_shape,
        input_output_aliases=aliases, scratch_shapes=list(scratch_shapes),
        compiler_params=_cparams(("parallel", "arbitrary")), name=name)(*args)


def _const_spec(shape):
    return pl.BlockSpec(shape, functools.partial(lambda b, ci, n: (0,) * n, n=len(shape)))


def _gates_spec(seqs):
    nchunk, c, nseq = seqs["nchunk"], seqs["c"], seqs["nseq"]
    return pl.BlockSpec((nseq, GATE_ROWS, c), lambda b, ci: (b * nchunk + ci, 0, 0))


def mlstm_call(proj, gates_t, brow, bcol, norm_g, seqs, layer, depth, init_states, prev_mix, prev_states):
    body = functools.partial(_mlstm_kernel, c=seqs["c"], nseq=seqs["nseq"], n_valid=seqs["n_valid"],
                             has_init=init_states is not None)
    extra = [(gates_t, _gates_spec(seqs)), (brow, _const_spec((1, LANE))), (bcol, _const_spec((GATE_ROWS, 1))),
             (norm_g.reshape(1, A_W).astype(F32), _const_spec((1, A_W)))]
    return _mixer_call(body, "mlstm", proj, seqs, layer, depth,
                       [(A_W, 0), (A_W, 1), (A_W, 2), (LANE, G_OFF // LANE + 1)], extra, init_states,
                       [(H_A, DK_A, DV_A), (H_A, DK_A), (1, H_A)], A_W, prev_mix, prev_states)


def _bdot(a, b, dims):
    return lax.dot_general(a, b, (dims, ((0,), (0,))), preferred_element_type=F32)


def _bmm(a, b):
    return _bdot(a.astype(BF16), b.astype(BF16), ((2,), (1,)))


def _bmm_nt(a, b):
    return _bdot(a.astype(BF16), b.astype(BF16), ((2,), (2,)))


def _bmm_tn(a, b):
    return _bdot(jnp.swapaxes(a, 1, 2).astype(BF16), b.astype(BF16), ((2,), (1,)))


def _unit_lower_inverse(a, row, col, c):
    d = jnp.where(row == col, 1.0, 0.0)
    s = 1
    while s < c:
        blk = jnp.where(((row // (2 * s)) == (col // (2 * s))) & ((row % (2 * s)) >= s) & ((col % (2 * s)) < s),
                        a, 0.0)
        if s == 1:
            d = d - blk
        else:
            d = d - _bmm(_bmm(d, blk), d)
        s *= 2
    return d


def _gdn_kernel(qkv_ref, z_ref, gc_ref, gt_ref, brow_ref, bcol_ref, arow_ref, acol_ref, cw_ref, buf0_ref, ng_ref,
                s0_ref, out_ref, s_ref, xp_ref, *, c, nseq, n_valid, has_init):
    ci = pl.program_id(1)
    hk = H_B * DK_B
    rows = lambda s: slice(s * c, (s + 1) * c)

    @pl.when(ci == 0)
    def _():
        xp_ref[:, 0:8, :] = jnp.zeros((nseq, 8, xp_ref.shape[2]), F32)
        if has_init:
            s_ref[...] = s0_ref[...]
            xp_ref[:, 8 - (CONV_W - 1):8, :] = buf0_ref[...]
        else:
            s_ref[...] = jnp.zeros_like(s_ref)

    @pl.when(ci > 0)
    def _():
        xp_ref[:, 0:8, :] = xp_ref[:, c:c + 8, :]

    for s in range(nseq):
        xp_ref[s, 8:8 + c, :] = qkv_ref[rows(s), :]

    gc = gc_ref[...] + brow_ref[...]
    gt = gt_ref[...] + bcol_ref[...]
    row, col = _tri_masks(c)
    tri = row >= col
    strict = row > col
    padded = n_valid < c
    if padded:
        valid_c = lax.broadcasted_iota(jnp.int32, (c, 1), 0) < n_valid
        valid_r = lax.broadcasted_iota(jnp.int32, (1, c), 1) < n_valid
    ga, gb = 2 * H_A, 2 * H_A + H_B

    sh = [(s, h) for s in range(nseq) for h in range(H_B)]

    def conv_act(c0):
        slabs = []
        for s, h in sh:
            cs = slice(c0 + h * DK_B, c0 + (h + 1) * DK_B)
            y = cw_ref[0:1, cs] * xp_ref[s, 8 - 3:8 - 3 + c, cs]
            for j in range(1, CONV_W):
                y = y + cw_ref[j:j + 1, cs] * xp_ref[s, 8 - 3 + j:8 - 3 + j + c, cs]
            slabs.append(_silu(y))
        return jnp.stack(slabs, axis=0)

    heads_c = lambda x, off: jnp.stack([x[rows(s), off + h:off + h + 1] for s, h in sh], axis=0)
    heads_r = lambda x, off: jnp.stack([x[s, off + h:off + h + 1, :] for s, h in sh], axis=0)
    g_c = heads_c(-jnp.exp(arow_ref[...]) * _softplus(gc), ga)
    g_r = heads_r(-jnp.exp(acol_ref[...]) * _softplus(gt), ga)
    beta = heads_c(_sigmoid(gc), gb)
    if padded:
        g_c, g_r = jnp.where(valid_c, g_c, 0.0), jnp.where(valid_r, g_r, 0.0)
        beta = jnp.where(valid_c, beta, 0.0)
    q = conv_act(0)
    k = conv_act(hk)
    v = conv_act(2 * hk)
    q = q * lax.rsqrt(jnp.sum(q * q, axis=2, keepdims=True) + EPS) * (DK_B ** -0.5)
    k = k * lax.rsqrt(jnp.sum(k * k, axis=2, keepdims=True) + EPS)
    sst = s_ref[...].reshape(nseq * H_B, DK_B, DV_B)
    gcum_c = jnp.sum(jnp.where(tri, g_r, 0.0), axis=2, keepdims=True)
    gcum_r = jnp.sum(jnp.where(row <= col, g_c, 0.0), axis=1, keepdims=True)
    gam = jnp.exp(jnp.where(tri, gcum_c - gcum_r, NEG_INF))
    eg = jnp.exp(gcum_c)
    a = jnp.where(strict, beta * _bmm_nt(k, k) * gam, 0.0)
    tinv = _unit_lower_inverse(a, row, col, c)
    rhs = jnp.concatenate([beta * v, (beta * eg) * k], axis=2)
    sol = _bmm(tinv, rhs)
    u = sol[:, :, :DV_B] - _bmm(sol[:, :, DV_B:], sst)
    o = eg * _bmm(q, sst) + _bmm(_bmm_nt(q, k) * gam, u)
    g_last = gcum_c[:, c - 1:c, :]
    s_new = jnp.exp(g_last) * sst + _bmm_tn(k * jnp.exp(g_last - gcum_c), u)
    s_ref[...] = s_new.reshape(nseq, H_B, DK_B, DV_B)
    for i, (s, h) in enumerate(sh):
        hs = slice(h * DV_B, (h + 1) * DV_B)
        out_ref[rows(s), hs] = _head_norm(o[i], ng_ref[:, hs]) * _silu(z_ref[rows(s), hs])


def gdn_call(proj, gates_t, brow, bcol, arow, acol, conv_w, norm_g, seqs, layer, depth, init_states, buf0,
             prev_mix, prev_states):
    wb = 3 * H_B * DK_B
    has_init = init_states is not None
    nseq = seqs["nseq"]
    body = functools.partial(_gdn_kernel, c=seqs["c"], nseq=nseq, n_valid=seqs["n_valid"], has_init=has_init)
    if has_init:
        buf_spec = pl.BlockSpec((None, nseq, CONV_W - 1, wb), lambda b, ci: (layer, b, 0, 0))
    else:
        buf0 = jnp.zeros((1, nseq, CONV_W - 1, wb), F32)
        buf_spec = pl.BlockSpec((None, nseq, CONV_W - 1, wb), lambda b, ci: (0, 0, 0, 0))
    extra = [(gates_t, _gates_spec(seqs)), (brow, _const_spec((1, LANE))), (bcol, _const_spec((GATE_ROWS, 1))),
             (arow, _const_spec((1, LANE))), (acol, _const_spec((GATE_ROWS, 1))),
             (conv_w.astype(F32), _const_spec((CONV_W, wb))), (buf0, buf_spec),
             (norm_g.reshape(1, H_B * DV_B).astype(F32), _const_spec((1, H_B * DV_B)))]
    return _mixer_call(body, "gdn", proj, seqs, layer, depth,
                       [(wb, B_OFF // wb), (A_W, (B_OFF + wb) // A_W), (LANE, G_OFF // LANE)], extra, init_states,
                       [(H_B, DK_B, DV_B)], H_B * DV_B, prev_mix, prev_states,
                       scratch_shapes=[pltpu.VMEM((nseq, seqs["c"] + 8, wb), F32)])


def _hgrn_kernel(q_ref, f_ref, i_ref, g_ref, lb_ref, ng_ref, s0_ref, out_ref, s_ref, st_ref, *,
                 c, sub, nseq, n_valid, has_init, nchunk):
    ci = pl.program_id(1)

    sh = [(s, h) for s in range(nseq) for h in range(H_C)]

    @pl.when(ci == 0)
    def _():
        for i, (s, h) in enumerate(sh):
            st_ref[i] = s0_ref[s, h].T if has_init else jnp.zeros((DV_C, DK_C), F32)

    padded = n_valid < sub
    rowi = lax.broadcasted_iota(jnp.int32, (1, sub, 1), 1)
    heads = lambda ref, j: jnp.stack(
        [ref[s * c + j * sub:s * c + (j + 1) * sub, h * DK_C:(h + 1) * DK_C] for s, h in sh], axis=0)
    lb = jnp.stack([lb_ref[:, h * DK_C:(h + 1) * DK_C] for s, h in sh], axis=0)
    log_1mlb, log_lb = jnp.log1p(-lb), jnp.log(lb)
    sst = st_ref[...]
    for j in range(c // sub):
        cf = heads(f_ref, j)
        la = log_1mlb + _log_sigmoid(cf)
        logf = jnp.maximum(la, log_lb) + jnp.log1p(jnp.exp(-jnp.abs(la - log_lb)))
        k = (1.0 - lb) * _sigmoid(-cf)
        if padded:
            logf, k = jnp.where(rowi < n_valid, logf, 0.0), jnp.where(rowi < n_valid, k, 0.0)
        q = _silu(heads(q_ref, j))
        iv = heads(i_ref, j)
        bc = jnp.zeros((nseq * H_C, sub, DK_C), F32)
        for t in range(sub):
            bc = bc + jnp.where(rowi >= t, logf[:, t:t + 1, :], 0.0)
        o = _bmm_nt(q * jnp.exp(bc), sst)
        for t in range(sub):
            dec = jnp.exp(jnp.where(rowi >= t, bc - bc[:, t:t + 1, :], NEG_INF))
            att = jnp.sum(q * dec * k[:, t:t + 1, :], axis=2, keepdims=True)
            o = o + att * iv[:, t:t + 1, :]
        b_last = bc[:, sub - 1:sub, :]
        sst = jnp.exp(b_last) * sst + _bmm_tn(iv, k * jnp.exp(b_last - bc))
        for i, (s, h) in enumerate(sh):
            rows = slice(s * c + j * sub, s * c + (j + 1) * sub)
            hs = slice(h * DK_C, (h + 1) * DK_C)
            out_ref[rows, hs] = _head_norm(o[i], ng_ref[:, hs]) * _silu(g_ref[rows, hs])
    st_ref[...] = sst

    @pl.when(ci == nchunk - 1)
    def _():
        for i, (s, h) in enumerate(sh):
            s_ref[s, h] = st_ref[i].T


def hgrn_call(proj, lb, norm_g, seqs, sub, layer, depth, init_states, prev_mix, prev_states):
    w = H_C * DK_C
    cb = C_OFF // w
    body = functools.partial(_hgrn_kernel, c=seqs["c"], sub=sub, nseq=seqs["nseq"],
                             n_valid=min(seqs["n_valid"], sub),
                             has_init=init_states is not None, nchunk=seqs["nchunk"])
    extra = [(lb.reshape(1, w).astype(F32), _const_spec((1, w))),
             (norm_g.reshape(1, w).astype(F32), _const_spec((1, w)))]
    return _mixer_call(body, "hgrn", proj, seqs, layer, depth,
                       [(w, cb), (w, cb + 1), (w, cb + 2), (w, cb + 3)], extra, init_states,
                       [(H_C, DK_C, DV_C)], w, prev_mix, prev_states,
                       scratch_shapes=[pltpu.VMEM((seqs["nseq"] * H_C, DV_C, DK_C), F32)])


def _top_values(x, k):
    rows = []
    for _ in range(k):
        mx = jnp.max(x, axis=0, keepdims=True)
        rows.append(mx)
        x = jnp.where(x == mx, NEG_INF, x)
    return jnp.concatenate(rows, axis=0)


def _route_kernel(q_ref, sk_ref, s1_ref, s2_ref, e1_ref, e2_ref, tau_ref):
    for h in range(PEER_HEADS):
        s1 = _mmf_nt(sk_ref[h, 0], q_ref[:, (2 * h) * LANE:(2 * h + 1) * LANE])
        s2 = _mmf_nt(sk_ref[h, 1], q_ref[:, (2 * h + 1) * LANE:(2 * h + 2) * LANE])
        a = _top_values(s1, PEER_TOPK)
        b = _top_values(s2, PEER_TOPK)
        cand = jnp.concatenate([a[0:1, :] + b] + [a[i:i + 1, :] + b[:PEER_TOPK // 2, :]
                                                  for i in range(1, PEER_TOPK)], axis=0)
        best = _top_values(cand, PEER_TOPK)
        z = jnp.sum(jnp.exp(best - best[0:1, :]), axis=0, keepdims=True)
        s1_ref[h] = s1
        s2_ref[h] = s2
        e1_ref[h] = jnp.exp(s1 - a[0:1, :])
        e2_ref[h] = jnp.exp(s2 - b[0:1, :]) / z
        tau_ref[h:h + 1, :] = best[PEER_TOPK - 1:PEER_TOPK, :]


def route_call(qry, subkeys, tb=256):
    t = qry.shape[0]
    big = jax.ShapeDtypeStruct((PEER_HEADS, N_KEYS, t), F32)
    big_spec = pl.BlockSpec((PEER_HEADS, N_KEYS, tb), lambda i: (0, 0, i))
    return pl.pallas_call(
        _route_kernel, grid=(t // tb,),
        in_specs=[pl.BlockSpec((tb, qry.shape[1]), lambda i: (i, 0)),
                  pl.BlockSpec(subkeys.shape, lambda i: (0, 0, 0, 0))],
        out_specs=[big_spec] * 4 + [pl.BlockSpec((PEER_HEADS, tb), lambda i: (0, i))],
        out_shape=[big] * 4 + [jax.ShapeDtypeStruct((PEER_HEADS, t), F32)],
        compiler_params=_cparams(("parallel",)), name="peer_route",
    )(qry, subkeys.astype(F32))


PEER_SUB = 256
PEER_LANES = 256


def _peer_kernel(t_ref, x_ref, u_ref, v_ref, s1_ref, s2_ref, e1_ref, e2_ref, tau_ref, y_ref, *, eb):
    j = pl.program_id(1)
    tm = t_ref.shape[0]

    @pl.when(j == 0)
    def _():
        y_ref[...] = x_ref[...]

    for p in range(eb // PEER_SUB):
        es = slice(p * PEER_SUB, (p + 1) * PEER_SUB)
        a_t = lax.dot_general(u_ref[es, :], t_ref[...], (((1,), (1,)), ((), ())), preferred_element_type=F32)
        rows = []
        for r in range(PEER_SUB // N_KEYS):
            i1 = j * (eb // N_KEYS) + p * (PEER_SUB // N_KEYS) + r
            s1_rows = [s1_ref[h, pl.ds(i1, 1), :] for h in range(PEER_HEADS)]
            e1_rows = [e1_ref[h, pl.ds(i1, 1), :] for h in range(PEER_HEADS)]
            tiles = []
            for tc in range(tm // PEER_LANES):
                ls = slice(tc * PEER_LANES, (tc + 1) * PEER_LANES)
                gsum = None
                for h in range(PEER_HEADS):
                    sc = s1_rows[h][:, ls] + s2_ref[h, :, ls]
                    gh = jnp.where(sc >= tau_ref[h:h + 1, ls], e1_rows[h][:, ls] * e2_ref[h, :, ls], 0.0)
                    gsum = gh if gsum is None else gsum + gh
                tiles.append(jax.nn.gelu(a_t[r * N_KEYS:(r + 1) * N_KEYS, ls]) * gsum)
            rows.append(jnp.concatenate(tiles, axis=1))
        coef = jnp.concatenate(rows, axis=0)
        y_ref[...] += jnp.dot(coef.T.astype(BF16), v_ref[es, :], preferred_element_type=F32)


def peer_call(t_bf, x, u_bf, v_bf, layer, s1, s2, e1, e2, tau, tm=512, eb=512):
    t, d = t_bf.shape
    ne = u_bf.shape[1]
    once = pl.Buffered(1)
    big_spec = pl.BlockSpec((PEER_HEADS, N_KEYS, tm), lambda i, j: (0, 0, i), pipeline_mode=once)
    return pl.pallas_call(
        functools.partial(_peer_kernel, eb=eb),
        grid=(t // tm, ne // eb),
        in_specs=[pl.BlockSpec((tm, d), lambda i, j: (i, 0), pipeline_mode=once),
                  pl.BlockSpec((tm, d), lambda i, j: (i, 0), pipeline_mode=once),
                  pl.BlockSpec((None, eb, d), lambda i, j: (layer, j, 0)),
                  pl.BlockSpec((None, eb, d), lambda i, j: (layer, j, 0)),
                  big_spec, big_spec, big_spec, big_spec,
                  pl.BlockSpec((PEER_HEADS, tm), lambda i, j: (0, i), pipeline_mode=once)],
        out_specs=pl.BlockSpec((tm, d), lambda i, j: (i, 0)),
        out_shape=jax.ShapeDtypeStruct((t, d), F32),
        compiler_params=_cparams(("parallel", "arbitrary")), name="peer_dense",
    )(t_bf, x, u_bf, v_bf, s1, s2, e1, e2, tau)


def _gate_params(b_i, b_f, a_log, dt_bias):
    zeros = jnp.zeros((LANE - 2 * H_A - 2 * H_B,), F32)
    bias = jnp.concatenate([b_i.astype(F32), b_f.astype(F32), dt_bias.astype(F32), jnp.zeros((H_B,), F32), zeros])
    alog = jnp.concatenate([jnp.zeros((2 * H_A,), F32), a_log.astype(F32), jnp.zeros((H_B,), F32), zeros])
    return (bias.reshape(1, LANE), bias[:GATE_ROWS].reshape(GATE_ROWS, 1),
            alog.reshape(1, LANE), alog[:GATE_ROWS].reshape(GATE_ROWS, 1))


def _gates_rows(proj, c):
    na, nb = 2 * H_A, 2 * H_B
    g = jnp.concatenate([proj[:, G_OFF + LANE:G_OFF + LANE + na], proj[:, G_OFF + na:G_OFF + na + nb],
                         jnp.zeros((proj.shape[0], GATE_ROWS - na - nb), F32)], axis=1)
    return g.reshape(g.shape[0] // c, c, GATE_ROWS).transpose(0, 2, 1)


def _hgrn_lower_bounds(logits):
    p = jax.nn.softmax(logits.astype(F32), axis=0)
    cs = jnp.cumsum(p, axis=0)
    return cs - cs[0:1]


def kernel(x_prompt, x_sample, state_mlstm_C, state_mlstm_n, state_mlstm_m, state_gdn_S, state_gdn_conv,
           state_hgrn_S, norm_mix_g, w_in, mlstm_b_i, mlstm_b_f, mlstm_norm_g, gdn_conv_w, gdn_A_log,
           gdn_dt_bias, gdn_norm_g, hgrn_lb_logits, hgrn_norm_g, w_out, norm_ffn_g, peer_w_q, peer_subkeys,
           peer_u, peer_v, final_norm_g, *, chunk=128, hgrn_chunk=16, tm=512, tm_peer=512, eb_peer=512, tb=256):
    bp, lp, d = x_prompt.shape
    bs, ls, _ = x_sample.shape
    depth = w_in.shape[0]
    tp = bp * lp
    ts = bs * ls
    x = jnp.concatenate([x_prompt.reshape(tp, d), x_sample.reshape(ts, d)], axis=0).astype(F32)
    tm_in = next(c for c in (1024, 1088, tm) if (tp + ts) % c == 0)
    lbs = _hgrn_lower_bounds(hgrn_lb_logits)
    wb = 3 * H_B * DK_B
    u_all, v_all = peer_u.astype(BF16), peer_v.astype(BF16)
    wo_all, wq_all = w_out.astype(BF16), peer_w_q.astype(BF16)
    pseq = dict(row0=0, nb=bp, nchunk=lp // chunk, c=chunk, n_valid=chunk, nseq=1)
    sseq = dict(row0=0, nb=bs, nchunk=1, c=SAMPLE_PAD, n_valid=ls, nseq=SAMPLE_SEQS)
    a_init = [state_mlstm_C, state_mlstm_n, state_mlstm_m.reshape(depth, bs, 1, H_A)]
    st_ap = st_as = st_bp = st_bs = st_cp = st_cs = None
    bufs_p, bufs_s = [], []
    for l in range(depth):
        h = rmsnorm_call(x, norm_mix_g[l], BF16, tb=tb)
        proj = in_proj_call(h, w_in, l, tm=tm_in)
        proj_s = jnp.pad(proj[tp:].reshape(bs, ls, NP), ((0, 0), (0, SAMPLE_PAD - ls), (0, 0)))
        proj_s = proj_s.reshape(bs * SAMPLE_PAD, NP)
        brow, bcol, arow, acol = _gate_params(mlstm_b_i[l], mlstm_b_f[l], gdn_A_log[l], gdn_dt_bias[l])
        gt_p = _gates_rows(proj[:tp], chunk)
        gt_s = _gates_rows(proj_s, SAMPLE_PAD)
        mix_a, *st_ap = mlstm_call(proj, gt_p, brow, bcol, mlstm_norm_g[l], pseq, l, depth, None, None, st_ap)
        smp_a, *st_as = mlstm_call(proj_s, gt_s, brow, bcol, mlstm_norm_g[l], sseq, l, depth, a_init, None, st_as)
        gdn_w = (brow, bcol, arow, acol, gdn_conv_w[l], gdn_norm_g[l])
        mix_b, *st_bp = gdn_call(proj, gt_p, *gdn_w, pseq, l, depth, None, None, None, st_bp)
        smp_b, *st_bs = gdn_call(proj_s, gt_s, *gdn_w, sseq, l, depth, [state_gdn_S], state_gdn_conv, None, st_bs)
        mix_c, *st_cp = hgrn_call(proj, lbs[l], hgrn_norm_g[l], pseq, hgrn_chunk, l, depth, None, None, st_cp)
        smp_c, *st_cs = hgrn_call(proj_s, lbs[l], hgrn_norm_g[l], sseq, SAMPLE_PAD, l, depth, [state_hgrn_S],
                                  None, st_cs)
        valid = lambda m: m.reshape(bs, SAMPLE_PAD, m.shape[1])[:, :ls].reshape(ts, m.shape[1])
        mix_a, mix_b, mix_c = (m.at[tp:].set(valid(s)) for m, s in ((mix_a, smp_a), (mix_b, smp_b), (mix_c, smp_c)))
        bufs_p.append(jnp.stack([proj[(b + 1) * lp - (CONV_W - 1):(b + 1) * lp, B_OFF:B_OFF + wb]
                                 for b in range(bp)], axis=0))
        bufs_s.append(proj[tp:, B_OFF:B_OFF + wb].reshape(bs, ls, wb)[:, ls - (CONV_W - 1):])
        x = matmul_call([mix_a, mix_b, mix_c], wo_all, l, res=x, tm=tm, name="out_proj")
        t_bf = rmsnorm_call(x, norm_ffn_g[l], BF16, tb=tb)
        qry = matmul_call([t_bf], wq_all, l, tm=tm, name="peer_query")
        s1, s2, e1, e2, tau = route_call(qry, peer_subkeys[l], tb=tb)
        x = peer_call(t_bf, x, u_all, v_all, l, s1, s2, e1, e2, tau, tm=tm_peer, eb=eb_peer)
    y_prompt = rmsnorm_call(x, final_norm_g, F32, tb=tb, row0=0, rows=tp).reshape(bp, lp, d)
    y_sample = rmsnorm_call(x, final_norm_g, F32, tb=tb, row0=tp, rows=ts).reshape(bs, ls, d)
    m_p, m_s = st_ap[2].reshape(depth, bp, H_A), st_as[2].reshape(depth, bs, H_A)
    return (y_prompt, y_sample, st_ap[0], st_as[0], st_ap[1], st_as[1], m_p, m_s, st_bp[0], st_bs[0],
            jnp.stack(bufs_p, axis=0), jnp.stack(bufs_s, axis=0), st_cp[0], st_cs[0])
```

```python
import functools

import jax
import jax.numpy as jnp
from jax import lax
from jax.experimental import pallas as pl
from jax.experimental.pallas import tpu as pltpu

F32 = jnp.float32
BF16 = jnp.bfloat16
EPS = 1e-6
NEG_INF = float("-inf")

H_A, DK_A, DV_A = 6, 128, 256
H_B, DK_B, DV_B = 12, 128, 128
H_C, DK_C, DV_C = 8, 128, 128
CONV_W = 4
PEER_HEADS, N_KEYS, PEER_TOPK = 8, 128, 16
SAMPLE_PAD = 8
SAMPLE_SEQS = 4
LANE = 128
VMEM_LIMIT = 56 * 1024 * 1024

A_W = H_A * DV_A
B_OFF = 3 * A_W
IN_TN = 512
G_OFF = B_OFF + 4 * H_B * DV_B
C_OFF = G_OFF + IN_TN
NP = C_OFF + 4 * H_C * DV_C
GATE_ROWS = 40


def _cparams(sem):
    return pltpu.CompilerParams(dimension_semantics=sem, vmem_limit_bytes=VMEM_LIMIT)


def _mm(a, b):
    return jnp.dot(a.astype(BF16), b.astype(BF16), preferred_element_type=F32)


def _mmf_nt(a, b):
    return lax.dot_general(a, b, (((1,), (1,)), ((), ())), preferred_element_type=F32,
                           precision=lax.Precision.HIGHEST)


def _softplus(x):
    return jnp.maximum(x, 0.0) + jnp.log1p(jnp.exp(-jnp.abs(x)))


def _log_sigmoid(x):
    return -_softplus(-x)


def _sigmoid(x):
    return 1.0 / (1.0 + jnp.exp(-x))


def _silu(x):
    return x * _sigmoid(x)


def _head_norm(x, g_row):
    return x * lax.rsqrt(jnp.mean(x * x, axis=-1, keepdims=True) + EPS) * g_row


def _rms_kernel(x_ref, g_ref, h_ref):
    x = x_ref[...]
    y = x * lax.rsqrt(jnp.mean(x * x, axis=-1, keepdims=True) + EPS)
    h_ref[...] = (y * g_ref[...]).astype(h_ref.dtype)


def rmsnorm_call(x, g, out_dtype, tb=256, row0=0, rows=None):
    d = x.shape[1]
    rows = x.shape[0] if rows is None else rows
    rb = row0 // tb
    return pl.pallas_call(
        _rms_kernel, grid=(rows // tb,),
        in_specs=[pl.BlockSpec((tb, d), lambda i: (rb + i, 0)), pl.BlockSpec((1, d), lambda i: (0, 0))],
        out_specs=pl.BlockSpec((tb, d), lambda i: (i, 0)),
        out_shape=jax.ShapeDtypeStruct((rows, d), out_dtype),
        compiler_params=_cparams(("parallel",)), name="rmsnorm",
    )(x, g.reshape(1, d).astype(F32))


def _matmul_kernel(*refs, n_pairs, has_res):
    a_refs, w_refs = refs[:n_pairs], refs[n_pairs:2 * n_pairs]
    o_ref = refs[-1]
    acc = _mm(a_refs[0][...], w_refs[0][...])
    for a_ref, w_ref in zip(a_refs[1:], w_refs[1:]):
        acc = acc + _mm(a_ref[...], w_ref[...])
    if has_res:
        acc = acc + refs[2 * n_pairs][...]
    o_ref[...] = acc


def matmul_call(a_list, w, layer, res=None, tm=512, tn=512, name="matmul"):
    t = a_list[0].shape[0]
    n = w.shape[2]
    in_specs = [pl.BlockSpec((tm, a.shape[1]), lambda i, j: (i, 0)) for a in a_list]
    row = 0
    for a in a_list:
        k = a.shape[1]
        in_specs.append(pl.BlockSpec((None, k, tn), functools.partial(lambda i, j, rb: (layer, rb, j), rb=row // k)))
        row += k
    args = list(a_list) + [w] * len(a_list)
    if res is not None:
        in_specs.append(pl.BlockSpec((tm, tn), lambda i, j: (i, j)))
        args.append(res)
    return pl.pallas_call(
        functools.partial(_matmul_kernel, n_pairs=len(a_list), has_res=res is not None),
        grid=(t // tm, n // tn), in_specs=in_specs,
        out_specs=pl.BlockSpec((tm, tn), lambda i, j: (i, j)),
        out_shape=jax.ShapeDtypeStruct((t, n), F32),
        compiler_params=_cparams(("parallel", "parallel")), name=name,
    )(*args)


A_BLOCKS = B_OFF // IN_TN
G_BLOCK = G_OFF // IN_TN
B_SHIFT = 2 * H_A
C_SHIFT = 2 * H_A + 2 * H_B
W_ROWS = 512


def _in_proj_kernel(a_ref, wm_ref, we_ref, o_ref, wb_ref):
    j, i = pl.program_id(0), pl.program_id(1)
    k = wm_ref.shape[0]

    def convert(shift):
        def body(r, carry):
            rs = pl.ds(pl.multiple_of(r * W_ROWS, W_ROWS), W_ROWS)
            if shift == 0:
                blk = wm_ref[rs, :]
            else:
                blk = jnp.concatenate([wm_ref[rs, shift:], we_ref[rs, :shift]], axis=1)
            wb_ref[rs, :] = blk.astype(BF16)
            return carry
        lax.fori_loop(0, k // W_ROWS, body, 0)

    first = i == 0

    @pl.when(first & (j < A_BLOCKS))
    def _():
        convert(0)

    @pl.when(first & (j >= A_BLOCKS) & (j < G_BLOCK))
    def _():
        convert(B_SHIFT)

    @pl.when(first & (j == G_BLOCK))
    def _():
        wb_ref[:, 0:LANE] = wm_ref[:, 0:LANE].astype(BF16)
        wb_ref[:, LANE:2 * LANE] = we_ref[...].astype(BF16)
        wb_ref[:, 2 * LANE:] = jnp.zeros((k, IN_TN - 2 * LANE), BF16)

    @pl.when(first & (j > G_BLOCK))
    def _():
        convert(C_SHIFT)

    o_ref[...] = jnp.dot(a_ref[...], wb_ref[...], preferred_element_type=F32)


def in_proj_call(a, w_in, layer, tm):
    t, k = a.shape
    lanes_per_tile = IN_TN // LANE
    main_blk = lambda j: jnp.where(j <= G_BLOCK, j, j - 1)
    extra_blk = lambda j: jnp.where(j < G_BLOCK, lanes_per_tile * (j + 1),
                                    jnp.where(j == G_BLOCK, B_OFF // LANE, lanes_per_tile * j))
    return pl.pallas_call(
        _in_proj_kernel, grid=(NP // IN_TN, t // tm),
        in_specs=[pl.BlockSpec((tm, k), lambda j, i: (i, 0)),
                  pl.BlockSpec((None, k, IN_TN), lambda j, i: (layer, 0, main_blk(j))),
                  pl.BlockSpec((None, k, LANE), lambda j, i: (layer, 0, extra_blk(j)))],
        out_specs=pl.BlockSpec((tm, IN_TN), lambda j, i: (i, j)),
        out_shape=jax.ShapeDtypeStruct((t, NP), F32),
        scratch_shapes=[pltpu.VMEM((k, IN_TN), BF16)],
        compiler_params=_cparams(("parallel", "arbitrary")), name="in_proj",
    )(a, w_in, w_in)


def _tri_masks(c):
    row = lax.broadcasted_iota(jnp.int32, (c, c), 0)
    col = lax.broadcasted_iota(jnp.int32, (c, c), 1)
    return row, col


def _mlstm_kernel(qk_ref, v_ref, o_ref, gc_ref, gt_ref, brow_ref, bcol_ref, ng_ref, c0_ref, n0_ref, m0_ref,
                  out_ref, c_ref, n_ref, m_ref, *, c, nseq, n_valid, has_init):
    ci = pl.program_id(1)

    @pl.when(ci == 0)
    def _():
        if has_init:
            c_ref[...] = c0_ref[...]
            n_ref[...] = n0_ref[...]
            m_ref[...] = m0_ref[...]
        else:
            c_ref[...] = jnp.zeros_like(c_ref)
            n_ref[...] = jnp.zeros_like(n_ref)
            m_ref[...] = jnp.zeros_like(m_ref)

    gc = gc_ref[...] + brow_ref[...]
    gt = gt_ref[...] + bcol_ref[...]
    row, col = _tri_masks(c)
    tri = row >= col
    padded = n_valid < c
    if padded:
        valid_c = lax.broadcasted_iota(jnp.int32, (c, 1), 0) < n_valid
        valid_r = lax.broadcasted_iota(jnp.int32, (1, c), 1) < n_valid
    sh = [(s, h) for s in range(nseq) for h in range(H_A)]
    rows = lambda s: slice(s * c, (s + 1) * c)
    heads_c = lambda x, off: jnp.stack([x[rows(s), off + h:off + h + 1] for s, h in sh], axis=0)
    heads_r = lambda x, off: jnp.stack([x[s, off + h:off + h + 1, :] for s, h in sh], axis=0)
    ig_c, lf_c = heads_c(gc, 0), heads_c(_log_sigmoid(gc), H_A)
    ig_r, lf_r = heads_r(gt, 0), heads_r(_log_sigmoid(gt), H_A)
    if padded:
        ig_c, lf_c = jnp.where(valid_c, ig_c, NEG_INF), jnp.where(valid_c, lf_c, 0.0)
        ig_r, lf_r = jnp.where(valid_r, ig_r, NEG_INF), jnp.where(valid_r, lf_r, 0.0)
    q = jnp.stack([qk_ref[rows(s), h * DK_A:(h + 1) * DK_A] for s, h in sh], axis=0) * (DK_A ** -0.5)
    k = jnp.stack([qk_ref[rows(s), (H_A + h) * DK_A:(H_A + h + 1) * DK_A] for s, h in sh], axis=0)
    v = jnp.stack([v_ref[rows(s), h * DV_A:(h + 1) * DV_A] for s, h in sh], axis=0)
    cst = c_ref[...].reshape(nseq * H_A, DK_A, DV_A)
    nst = jnp.stack([n_ref[s, h:h + 1, :] for s, h in sh], axis=0)
    m_prev = jnp.stack([m_ref[s, :, h:h + 1] for s, h in sh], axis=0)
    b_c = jnp.sum(jnp.where(tri, lf_r, 0.0), axis=2, keepdims=True)
    b_r = jnp.sum(jnp.where(row <= col, lf_c, 0.0), axis=1, keepdims=True)
    log_d = jnp.where(tri, b_c - b_r + ig_r, NEG_INF)
    m_inter = b_c + m_prev
    m_t = jnp.maximum(m_inter, jnp.max(log_d, axis=2, keepdims=True))
    s = _bmm_nt(q, k) * jnp.exp(log_d - m_t)
    inter = jnp.exp(m_inter - m_t)
    num = _bmm(s, v) + inter * _bmm(q, cst)
    den = jnp.sum(s, axis=2, keepdims=True) + inter * jnp.sum(q * nst, axis=2, keepdims=True)
    hh = num / jnp.maximum(jnp.abs(den), jnp.exp(-m_t))
    m_new = m_t[:, c - 1:c, :]
    w = jnp.exp(b_c[:, c - 1:c, :] - b_c + ig_c - m_new)
    decay = inter[:, c - 1:c, :]
    kw = k * w
    c_ref[...] = (decay * cst + _bmm_tn(kw, v)).reshape(nseq, H_A, DK_A, DV_A)
    n_new = decay * nst + jnp.sum(kw, axis=1, keepdims=True)
    for i, (s, h) in enumerate(sh):
        hs = slice(h * DV_A, (h + 1) * DV_A)
        n_ref[s, h:h + 1, :] = n_new[i]
        m_ref[s, :, h:h + 1] = m_new[i]
        out_ref[rows(s), hs] = _head_norm(hh[i], ng_ref[:, hs]) * _sigmoid(o_ref[rows(s), hs])


def _mixer_call(body, name, proj, seqs, layer, depth, tok_blocks, extra_inputs, init_states, st_shapes,
                mix_w, prev_mix, prev_states, scratch_shapes=()):
    row0, nb, nchunk, c, nseq = seqs["row0"], seqs["nb"], seqs["nchunk"], seqs["c"], seqs["nseq"]
    assert nseq == 1 or nchunk == 1
    rows = nseq * c
    rb = row0 // rows
    tok = lambda b, ci: rb + b * nchunk + ci
    st_spec = lambda shape, lyr: pl.BlockSpec(
        (None, nseq) + shape, functools.partial(lambda b, ci, n: (lyr, b) + (0,) * n, n=len(shape)))
    in_specs = [pl.BlockSpec((rows, w), functools.partial(lambda b, ci, cb: (tok(b, ci), cb), cb=cb))
                for w, cb in tok_blocks]
    args = [proj] * len(tok_blocks)
    for arr, spec in extra_inputs:
        in_specs.append(spec)
        args.append(arr)
    if init_states is None:
        for shape in st_shapes:
            in_specs.append(pl.BlockSpec((None, nseq) + shape, functools.partial(
                lambda b, ci, n: (0,) * (n + 2), n=len(shape))))
            args.append(jnp.zeros((1, nseq) + shape, F32))
    else:
        for shape, arr in zip(st_shapes, init_states):
            in_specs.append(st_spec(shape, layer))
            args.append(arr)
    n_in = len(args)
    out_specs = [pl.BlockSpec((rows, mix_w), lambda b, ci: (tok(b, ci), 0))]
    out_shape = [jax.ShapeDtypeStruct((proj.shape[0], mix_w), F32)]
    for shape in st_shapes:
        out_specs.append(st_spec(shape, layer))
        out_shape.append(jax.ShapeDtypeStruct((depth, nb) + shape, F32))
    prev = [prev_mix] + (list(prev_states) if prev_states is not None else [None] * len(st_shapes))
    aliases = {}
    for k, arr in enumerate(prev):
        if arr is not None:
            aliases[len(args)] = k
            in_specs.append(pl.BlockSpec(memory_space=pl.ANY))
            args.append(arr)
    n_alias = len(args) - n_in

    def kernel_fn(*refs):
        body(*refs[:n_in], *refs[n_in + n_alias:])

    return pl.pallas_call(
        kernel_fn, grid=(nb // nseq, nchunk), in_specs=in_specs, out_specs=out_specs, out_shape=out_shape,
        input_output_aliases=aliases, scratch_shapes=list(scratch_shapes),
        compiler_params=_cparams(("parallel", "arbitrary")), name=name)(*args)


def _const_spec(shape):
    return pl.BlockSpec(shape, functools.partial(lambda b, ci, n: (0,) * n, n=len(shape)))


def _gates_spec(seqs):
    nchunk, c, nseq = seqs["nchunk"], seqs["c"], seqs["nseq"]
    return pl.BlockSpec((nseq, GATE_ROWS, c), lambda b, ci: (b * nchunk + ci, 0, 0))


def mlstm_call(proj, gates_t, brow, bcol, norm_g, seqs, layer, depth, init_states, prev_mix, prev_states):
    body = functools.partial(_mlstm_kernel, c=seqs["c"], nseq=seqs["nseq"], n_valid=seqs["n_valid"],
                             has_init=init_states is not None)
    extra = [(gates_t, _gates_spec(seqs)), (brow, _const_spec((1, LANE))), (bcol, _const_spec((GATE_ROWS, 1))),
             (norm_g.reshape(1, A_W).astype(F32), _const_spec((1, A_W)))]
    return _mixer_call(body, "mlstm", proj, seqs, layer, depth,
                       [(A_W, 0), (A_W, 1), (A_W, 2), (LANE, G_OFF // LANE + 1)], extra, init_states,
                       [(H_A, DK_A, DV_A), (H_A, DK_A), (1, H_A)], A_W, prev_mix, prev_states)


def _bdot(a, b, dims):
    return lax.dot_general(a, b, (dims, ((0,), (0,))), preferred_element_type=F32)


def _bmm(a, b):
    return _bdot(a.astype(BF16), b.astype(BF16), ((2,), (1,)))


def _bmm_nt(a, b):
    return _bdot(a.astype(BF16), b.astype(BF16), ((2,), (2,)))


def _bmm_tn(a, b):
    return _bdot(jnp.swapaxes(a, 1, 2).astype(BF16), b.astype(BF16), ((2,), (1,)))


def _unit_lower_inverse(a, row, col, c):
    d = jnp.where(row == col, 1.0, 0.0)
    s = 1
    while s < c:
        blk = jnp.where(((row // (2 * s)) == (col // (2 * s))) & ((row % (2 * s)) >= s) & ((col % (2 * s)) < s),
                        a, 0.0)
        if s == 1:
            d = d - blk
        else:
            d = d - _bmm(_bmm(d, blk), d)
        s *= 2
    return d


def _gdn_kernel(qkv_ref, z_ref, gc_ref, gt_ref, brow_ref, bcol_ref, arow_ref, acol_ref, cw_ref, buf0_ref, ng_ref,
                s0_ref, out_ref, s_ref, xp_ref, *, c, nseq, n_valid, has_init):
    ci = pl.program_id(1)
    hk = H_B * DK_B
    rows = lambda s: slice(s * c, (s + 1) * c)

    @pl.when(ci == 0)
    def _():
        xp_ref[:, 0:8, :] = jnp.zeros((nseq, 8, xp_ref.shape[2]), F32)
        if has_init:
            s_ref[...] = s0_ref[...]
            xp_ref[:, 8 - (CONV_W - 1):8, :] = buf0_ref[...]
        else:
            s_ref[...] = jnp.zeros_like(s_ref)

    @pl.when(ci > 0)
    def _():
        xp_ref[:, 0:8, :] = xp_ref[:, c:c + 8, :]

    for s in range(nseq):
        xp_ref[s, 8:8 + c, :] = qkv_ref[rows(s), :]

    gc = gc_ref[...] + brow_ref[...]
    gt = gt_ref[...] + bcol_ref[...]
    row, col = _tri_masks(c)
    tri = row >= col
    strict = row > col
    padded = n_valid < c
    if padded:
        valid_c = lax.broadcasted_iota(jnp.int32, (c, 1), 0) < n_valid
        valid_r = lax.broadcasted_iota(jnp.int32, (1, c), 1) < n_valid
    ga, gb = 2 * H_A, 2 * H_A + H_B

    sh = [(s, h) for s in range(nseq) for h in range(H_B)]

    def conv_act(c0):
        slabs = []
        for s, h in sh:
            cs = slice(c0 + h * DK_B, c0 + (h + 1) * DK_B)
            y = cw_ref[0:1, cs] * xp_ref[s, 8 - 3:8 - 3 + c, cs]
            for j in range(1, CONV_W):
                y = y + cw_ref[j:j + 1, cs] * xp_ref[s, 8 - 3 + j:8 - 3 + j + c, cs]
            slabs.append(_silu(y))
        return jnp.stack(slabs, axis=0)

    heads_c = lambda x, off: jnp.stack([x[rows(s), off + h:off + h + 1] for s, h in sh], axis=0)
    heads_r = lambda x, off: jnp.stack([x[s, off + h:off + h + 1, :] for s, h in sh], axis=0)
    g_c = heads_c(-jnp.exp(arow_ref[...]) * _softplus(gc), ga)
    g_r = heads_r(-jnp.exp(acol_ref[...]) * _softplus(gt), ga)
    beta = heads_c(_sigmoid(gc), gb)
    if padded:
        g_c, g_r = jnp.where(valid_c, g_c, 0.0), jnp.where(valid_r, g_r, 0.0)
        beta = jnp.where(valid_c, beta, 0.0)
    q = conv_act(0)
    k = conv_act(hk)
    v = conv_act(2 * hk)
    q = q * lax.rsqrt(jnp.sum(q * q, axis=2, keepdims=True) + EPS) * (DK_B ** -0.5)
    k = k * lax.rsqrt(jnp.sum(k * k, axis=2, keepdims=True) + EPS)
    sst = s_ref[...].reshape(nseq * H_B, DK_B, DV_B)
    gcum_c = jnp.sum(jnp.where(tri, g_r, 0.0), axis=2, keepdims=True)
    gcum_r = jnp.sum(jnp.where(row <= col, g_c, 0.0), axis=1, keepdims=True)
    gam = jnp.exp(jnp.where(tri, gcum_c - gcum_r, NEG_INF))
    eg = jnp.exp(gcum_c)
    a = jnp.where(strict, beta * _bmm_nt(k, k) * gam, 0.0)
    tinv = _unit_lower_inverse(a, row, col, c)
    rhs = jnp.concatenate([beta * v, (beta * eg) * k], axis=2)
    sol = _bmm(tinv, rhs)
    u = sol[:, :, :DV_B] - _bmm(sol[:, :, DV_B:], sst)
    o = eg * _bmm(q, sst) + _bmm(_bmm_nt(q, k) * gam, u)
    g_last = gcum_c[:, c - 1:c, :]
    s_new = jnp.exp(g_last) * sst + _bmm_tn(k * jnp.exp(g_last - gcum_c), u)
    s_ref[...] = s_new.reshape(nseq, H_B, DK_B, DV_B)
    for i, (s, h) in enumerate(sh):
        hs = slice(h * DV_B, (h + 1) * DV_B)
        out_ref[rows(s), hs] = _head_norm(o[i], ng_ref[:, hs]) * _silu(z_ref[rows(s), hs])


def gdn_call(proj, gates_t, brow, bcol, arow, acol, conv_w, norm_g, seqs, layer, depth, init_states, buf0,
             prev_mix, prev_states):
    wb = 3 * H_B * DK_B
    has_init = init_states is not None
    nseq = seqs["nseq"]
    body = functools.partial(_gdn_kernel, c=seqs["c"], nseq=nseq, n_valid=seqs["n_valid"], has_init=has_init)
    if has_init:
        buf_spec = pl.BlockSpec((None, nseq, CONV_W - 1, wb), lambda b, ci: (layer, b, 0, 0))
    else:
        buf0 = jnp.zeros((1, nseq, CONV_W - 1, wb), F32)
        buf_spec = pl.BlockSpec((None, nseq, CONV_W - 1, wb), lambda b, ci: (0, 0, 0, 0))
    extra = [(gates_t, _gates_spec(seqs)), (brow, _const_spec((1, LANE))), (bcol, _const_spec((GATE_ROWS, 1))),
             (arow, _const_spec((1, LANE))), (acol, _const_spec((GATE_ROWS, 1))),
             (conv_w.astype(F32), _const_spec((CONV_W, wb))), (buf0, buf_spec),
             (norm_g.reshape(1, H_B * DV_B).astype(F32), _const_spec((1, H_B * DV_B)))]
    return _mixer_call(body, "gdn", proj, seqs, layer, depth,
                       [(wb, B_OFF // wb), (A_W, (B_OFF + wb) // A_W), (LANE, G_OFF // LANE)], extra, init_states,
                       [(H_B, DK_B, DV_B)], H_B * DV_B, prev_mix, prev_states,
                       scratch_shapes=[pltpu.VMEM((nseq, seqs["c"] + 8, wb), F32)])


def _hgrn_kernel(q_ref, f_ref, i_ref, g_ref, lb_ref, ng_ref, s0_ref, out_ref, s_ref, st_ref, *,
                 c, sub, nseq, n_valid, has_init, nchunk):
    ci = pl.program_id(1)

    sh = [(s, h) for s in range(nseq) for h in range(H_C)]

    @pl.when(ci == 0)
    def _():
        for i, (s, h) in enumerate(sh):
            st_ref[i] = s0_ref[s, h].T if has_init else jnp.zeros((DV_C, DK_C), F32)

    padded = n_valid < sub
    rowi = lax.broadcasted_iota(jnp.int32, (1, sub, 1), 1)
    heads = lambda ref, j: jnp.stack(
        [ref[s * c + j * sub:s * c + (j + 1) * sub, h * DK_C:(h + 1) * DK_C] for s, h in sh], axis=0)
    lb = jnp.stack([lb_ref[:, h * DK_C:(h + 1) * DK_C] for s, h in sh], axis=0)
    log_1mlb, log_lb = jnp.log1p(-lb), jnp.log(lb)
    sst = st_ref[...]
    for j in range(c // sub):
        cf = heads(f_ref, j)
        la = log_1mlb + _log_sigmoid(cf)
        logf = jnp.maximum(la, log_lb) + jnp.log1p(jnp.exp(-jnp.abs(la - log_lb)))
        k = (1.0 - lb) * _sigmoid(-cf)
        if padded:
            logf, k = jnp.where(rowi < n_valid, logf, 0.0), jnp.where(rowi < n_valid, k, 0.0)
        q = _silu(heads(q_ref, j))
        iv = heads(i_ref, j)
        bc = jnp.zeros((nseq * H_C, sub, DK_C), F32)
        for t in range(sub):
            bc = bc + jnp.where(rowi >= t, logf[:, t:t + 1, :], 0.0)
        o = _bmm_nt(q * jnp.exp(bc), sst)
        for t in range(sub):
            dec = jnp.exp(jnp.where(rowi >= t, bc - bc[:, t:t + 1, :], NEG_INF))
            att = jnp.sum(q * dec * k[:, t:t + 1, :], axis=2, keepdims=True)
            o = o + att * iv[:, t:t + 1, :]
        b_last = bc[:, sub - 1:sub, :]
        sst = jnp.exp(b_last) * sst + _bmm_tn(iv, k * jnp.exp(b_last - bc))
        for i, (s, h) in enumerate(sh):
            rows = slice(s * c + j * sub, s * c + (j + 1) * sub)
            hs = slice(h * DK_C, (h + 1) * DK_C)
            out_ref[rows, hs] = _head_norm(o[i], ng_ref[:, hs]) * _silu(g_ref[rows, hs])
    st_ref[...] = sst

    @pl.when(ci == nchunk - 1)
    def _():
        for i, (s, h) in enumerate(sh):
            s_ref[s, h] = st_ref[i].T


def hgrn_call(proj, lb, norm_g, seqs, sub, layer, depth, init_states, prev_mix, prev_states):
    w = H_C * DK_C
    cb = C_OFF // w
    body = functools.partial(_hgrn_kernel, c=seqs["c"], sub=sub, nseq=seqs["nseq"],
                             n_valid=min(seqs["n_valid"], sub),
                             has_init=init_states is not None, nchunk=seqs["nchunk"])
    extra = [(lb.reshape(1, w).astype(F32), _const_spec((1, w))),
             (norm_g.reshape(1, w).astype(F32), _const_spec((1, w)))]
    return _mixer_call(body, "hgrn", proj, seqs, layer, depth,
                       [(w, cb), (w, cb + 1), (w, cb + 2), (w, cb + 3)], extra, init_states,
                       [(H_C, DK_C, DV_C)], w, prev_mix, prev_states,
                       scratch_shapes=[pltpu.VMEM((seqs["nseq"] * H_C, DV_C, DK_C), F32)])


def _top_values(x, k):
    rows = []
    for _ in range(k):
        mx = jnp.max(x, axis=0, keepdims=True)
        rows.append(mx)
        x = jnp.where(x == mx, NEG_INF, x)
    return jnp.concatenate(rows, axis=0)


def _route_kernel(q_ref, sk_ref, s1_ref, s2_ref, e1_ref, e2_ref, tau_ref):
    for h in range(PEER_HEADS):
        s1 = _mmf_nt(sk_ref[h, 0], q_ref[:, (2 * h) * LANE:(2 * h + 1) * LANE])
        s2 = _mmf_nt(sk_ref[h, 1], q_ref[:, (2 * h + 1) * LANE:(2 * h + 2) * LANE])
        a = _top_values(s1, PEER_TOPK)
        b = _top_values(s2, PEER_TOPK)
        cand = jnp.concatenate([a[0:1, :] + b] + [a[i:i + 1, :] + b[:PEER_TOPK // 2, :]
                                                  for i in range(1, PEER_TOPK)], axis=0)
        best = _top_values(cand, PEER_TOPK)
        z = jnp.sum(jnp.exp(best - best[0:1, :]), axis=0, keepdims=True)
        s1_ref[h] = s1
        s2_ref[h] = s2
        e1_ref[h] = jnp.exp(s1 - a[0:1, :])
        e2_ref[h] = jnp.exp(s2 - b[0:1, :]) / z
        tau_ref[h:h + 1, :] = best[PEER_TOPK - 1:PEER_TOPK, :]


def route_call(qry, subkeys, tb=256):
    t = qry.shape[0]
    big = jax.ShapeDtypeStruct((PEER_HEADS, N_KEYS, t), F32)
    big_spec = pl.BlockSpec((PEER_HEADS, N_KEYS, tb), lambda i: (0, 0, i))
    return pl.pallas_call(
        _route_kernel, grid=(t // tb,),
        in_specs=[pl.BlockSpec((tb, qry.shape[1]), lambda i: (i, 0)),
                  pl.BlockSpec(subkeys.shape, lambda i: (0, 0, 0, 0))],
        out_specs=[big_spec] * 4 + [pl.BlockSpec((PEER_HEADS, tb), lambda i: (0, i))],
        out_shape=[big] * 4 + [jax.ShapeDtypeStruct((PEER_HEADS, t), F32)],
        compiler_params=_cparams(("parallel",)), name="peer_route",
    )(qry, subkeys.astype(F32))


PEER_SUB = 512
PEER_LANES = 256


def _peer_kernel(t_ref, x_ref, u_ref, v_ref, s1_ref, s2_ref, e1_ref, e2_ref, tau_ref, y_ref, *, eb):
    j = pl.program_id(1)
    tm = t_ref.shape[0]

    @pl.when(j == 0)
    def _():
        y_ref[...] = x_ref[...]

    for p in range(eb // PEER_SUB):
        es = slice(p * PEER_SUB, (p + 1) * PEER_SUB)
        a_t = lax.dot_general(u_ref[es, :], t_ref[...], (((1,), (1,)), ((), ())), preferred_element_type=F32)
        rows = []
        for r in range(PEER_SUB // N_KEYS):
            i1 = j * (eb // N_KEYS) + p * (PEER_SUB // N_KEYS) + r
            s1_rows = [s1_ref[h, pl.ds(i1, 1), :] for h in range(PEER_HEADS)]
            e1_rows = [e1_ref[h, pl.ds(i1, 1), :] for h in range(PEER_HEADS)]
            tiles = []
            for tc in range(tm // PEER_LANES):
                ls = slice(tc * PEER_LANES, (tc + 1) * PEER_LANES)
                gsum = None
                for h in range(PEER_HEADS):
                    sc = s1_rows[h][:, ls] + s2_ref[h, :, ls]
                    gh = jnp.where(sc >= tau_ref[h:h + 1, ls], e1_rows[h][:, ls] * e2_ref[h, :, ls], 0.0)
                    gsum = gh if gsum is None else gsum + gh
                tiles.append(jax.nn.gelu(a_t[r * N_KEYS:(r + 1) * N_KEYS, ls]) * gsum)
            rows.append(jnp.concatenate(tiles, axis=1))
        coef = jnp.concatenate(rows, axis=0)
        y_ref[...] += jnp.dot(coef.T.astype(BF16), v_ref[es, :], preferred_element_type=F32)


def peer_call(t_bf, x, u_bf, v_bf, layer, s1, s2, e1, e2, tau, tm=512, eb=512):
    t, d = t_bf.shape
    ne = u_bf.shape[1]
    once = pl.Buffered(1)
    big_spec = pl.BlockSpec((PEER_HEADS, N_KEYS, tm), lambda i, j: (0, 0, i), pipeline_mode=once)
    return pl.pallas_call(
        functools.partial(_peer_kernel, eb=eb),
        grid=(t // tm, ne // eb),
        in_specs=[pl.BlockSpec((tm, d), lambda i, j: (i, 0), pipeline_mode=once),
                  pl.BlockSpec((tm, d), lambda i, j: (i, 0), pipeline_mode=once),
                  pl.BlockSpec((None, eb, d), lambda i, j: (layer, j, 0)),
                  pl.BlockSpec((None, eb, d), lambda i, j: (layer, j, 0)),
                  big_spec, big_spec, big_spec, big_spec,
                  pl.BlockSpec((PEER_HEADS, tm), lambda i, j: (0, i), pipeline_mode=once)],
        out_specs=pl.BlockSpec((tm, d), lambda i, j: (i, 0)),
        out_shape=jax.ShapeDtypeStruct((t, d), F32),
        compiler_params=_cparams(("parallel", "arbitrary")), name="peer_dense",
    )(t_bf, x, u_bf, v_bf, s1, s2, e1, e2, tau)


def _gate_params(b_i, b_f, a_log, dt_bias):
    zeros = jnp.zeros((LANE - 2 * H_A - 2 * H_B,), F32)
    bias = jnp.concatenate([b_i.astype(F32), b_f.astype(F32), dt_bias.astype(F32), jnp.zeros((H_B,), F32), zeros])
    alog = jnp.concatenate([jnp.zeros((2 * H_A,), F32), a_log.astype(F32), jnp.zeros((H_B,), F32), zeros])
    return (bias.reshape(1, LANE), bias[:GATE_ROWS].reshape(GATE_ROWS, 1),
            alog.reshape(1, LANE), alog[:GATE_ROWS].reshape(GATE_ROWS, 1))


def _gates_rows(proj, c):
    na, nb = 2 * H_A, 2 * H_B
    g = jnp.concatenate([proj[:, G_OFF + LANE:G_OFF + LANE + na], proj[:, G_OFF + na:G_OFF + na + nb],
                         jnp.zeros((proj.shape[0], GATE_ROWS - na - nb), F32)], axis=1)
    return g.reshape(g.shape[0] // c, c, GATE_ROWS).transpose(0, 2, 1)


def _hgrn_lower_bounds(logits):
    p = jax.nn.softmax(logits.astype(F32), axis=0)
    cs = jnp.cumsum(p, axis=0)
    return cs - cs[0:1]


def kernel(x_prompt, x_sample, state_mlstm_C, state_mlstm_n, state_mlstm_m, state_gdn_S, state_gdn_conv,
           state_hgrn_S, norm_mix_g, w_in, mlstm_b_i, mlstm_b_f, mlstm_norm_g, gdn_conv_w, gdn_A_log,
           gdn_dt_bias, gdn_norm_g, hgrn_lb_logits, hgrn_norm_g, w_out, norm_ffn_g, peer_w_q, peer_subkeys,
           peer_u, peer_v, final_norm_g, *, chunk=128, hgrn_chunk=16, tm=512, tm_peer=512, eb_peer=512, tb=256):
    bp, lp, d = x_prompt.shape
    bs, ls, _ = x_sample.shape
    depth = w_in.shape[0]
    tp = bp * lp
    ts = bs * ls
    x = jnp.concatenate([x_prompt.reshape(tp, d), x_sample.reshape(ts, d)], axis=0).astype(F32)
    tm_in = next(c for c in (1024, 1088, tm) if (tp + ts) % c == 0)
    lbs = _hgrn_lower_bounds(hgrn_lb_logits)
    wb = 3 * H_B * DK_B
    u_all, v_all = peer_u.astype(BF16), peer_v.astype(BF16)
    wo_all, wq_all = w_out.astype(BF16), peer_w_q.astype(BF16)
    pseq = dict(row0=0, nb=bp, nchunk=lp // chunk, c=chunk, n_valid=chunk, nseq=1)
    sseq = dict(row0=0, nb=bs, nchunk=1, c=SAMPLE_PAD, n_valid=ls, nseq=SAMPLE_SEQS)
    a_init = [state_mlstm_C, state_mlstm_n, state_mlstm_m.reshape(depth, bs, 1, H_A)]
    st_ap = st_as = st_bp = st_bs = st_cp = st_cs = None
    bufs_p, bufs_s = [], []
    for l in range(depth):
        h = rmsnorm_call(x, norm_mix_g[l], BF16, tb=tb)
        proj = in_proj_call(h, w_in, l, tm=tm_in)
        proj_s = jnp.pad(proj[tp:].reshape(bs, ls, NP), ((0, 0), (0, SAMPLE_PAD - ls), (0, 0)))
        proj_s = proj_s.reshape(bs * SAMPLE_PAD, NP)
        brow, bcol, arow, acol = _gate_params(mlstm_b_i[l], mlstm_b_f[l], gdn_A_log[l], gdn_dt_bias[l])
        gt_p = _gates_rows(proj[:tp], chunk)
        gt_s = _gates_rows(proj_s, SAMPLE_PAD)
        mix_a, *st_ap = mlstm_call(proj, gt_p, brow, bcol, mlstm_norm_g[l], pseq, l, depth, None, None, st_ap)
        smp_a, *st_as = mlstm_call(proj_s, gt_s, brow, bcol, mlstm_norm_g[l], sseq, l, depth, a_init, None, st_as)
        gdn_w = (brow, bcol, arow, acol, gdn_conv_w[l], gdn_norm_g[l])
        mix_b, *st_bp = gdn_call(proj, gt_p, *gdn_w, pseq, l, depth, None, None, None, st_bp)
        smp_b, *st_bs = gdn_call(proj_s, gt_s, *gdn_w, sseq, l, depth, [state_gdn_S], state_gdn_conv, None, st_bs)
        mix_c, *st_cp = hgrn_call(proj, lbs[l], hgrn_norm_g[l], pseq, hgrn_chunk, l, depth, None, None, st_cp)
        smp_c, *st_cs = hgrn_call(proj_s, lbs[l], hgrn_norm_g[l], sseq, SAMPLE_PAD, l, depth, [state_hgrn_S],
                                  None, st_cs)
        valid = lambda m: m.reshape(bs, SAMPLE_PAD, m.shape[1])[:, :ls].reshape(ts, m.shape[1])
        mix_a, mix_b, mix_c = (m.at[tp:].set(valid(s)) for m, s in ((mix_a, smp_a), (mix_b, smp_b), (mix_c, smp_c)))
        bufs_p.append(jnp.stack([proj[(b + 1) * lp - (CONV_W - 1):(b + 1) * lp, B_OFF:B_OFF + wb]
                                 for b in range(bp)], axis=0))
        bufs_s.append(proj[tp:, B_OFF:B_OFF + wb].reshape(bs, ls, wb)[:, ls - (CONV_W - 1):])
        x = matmul_call([mix_a, mix_b, mix_c], wo_all, l, res=x, tm=tm, name="out_proj")
        t_bf = rmsnorm_call(x, norm_ffn_g[l], BF16, tb=tb)
        qry = matmul_call([t_bf], wq_all, l, tm=tm, name="peer_query")
        s1, s2, e1, e2, tau = route_call(qry, peer_subkeys[l], tb=tb)
        x = peer_call(t_bf, x, u_all, v_all, l, s1, s2, e1, e2, tau, tm=tm_peer, eb=eb_peer)
    y_prompt = rmsnorm_call(x, final_norm_g, F32, tb=tb, row0=0, rows=tp).reshape(bp, lp, d)
    y_sample = rmsnorm_call(x, final_norm_g, F32, tb=tb, row0=tp, rows=ts).reshape(bs, ls, d)
    m_p, m_s = st_ap[2].reshape(depth, bp, H_A), st_as[2].reshape(depth, bs, H_A)
    return (y_prompt, y_sample, st_ap[0], st_as[0], st_ap[1], st_as[1], m_p, m_s, st_bp[0], st_bs[0],
            jnp.stack(bufs_p, axis=0), jnp.stack(bufs_s, axis=0), st_cp[0], st_cs[0])
```

```python
import functools

import jax
import jax.numpy as jnp
from jax import lax
from jax.experimental import pallas as pl
from jax.experimental.pallas import tpu as pltpu

F32 = jnp.float32
BF16 = jnp.bfloat16
EPS = 1e-6
NEG_INF = float("-inf")

H_A, DK_A, DV_A = 6, 128, 256
H_B, DK_B, DV_B = 12, 128, 128
H_C, DK_C, DV_C = 8, 128, 128
CONV_W = 4
PEER_HEADS, N_KEYS, PEER_TOPK = 8, 128, 16
SAMPLE_PAD = 8
SAMPLE_SEQS = 4
LANE = 128
VMEM_LIMIT = 56 * 1024 * 1024

A_W = H_A * DV_A
B_OFF = 3 * A_W
IN_TN = 512
G_OFF = B_OFF + 4 * H_B * DV_B
C_OFF = G_OFF + IN_TN
NP = C_OFF + 4 * H_C * DV_C
GATE_ROWS = 40


def _cparams(sem):
    return pltpu.CompilerParams(dimension_semantics=sem, vmem_limit_bytes=VMEM_LIMIT)


def _mm(a, b):
    return jnp.dot(a.astype(BF16), b.astype(BF16), preferred_element_type=F32)


def _mmf_nt(a, b):
    return lax.dot_general(a, b, (((1,), (1,)), ((), ())), preferred_element_type=F32,
                           precision=lax.Precision.HIGHEST)


def _softplus(x):
    return jnp.maximum(x, 0.0) + jnp.log1p(jnp.exp(-jnp.abs(x)))


def _log_sigmoid(x):
    return -_softplus(-x)


def _sigmoid(x):
    return 1.0 / (1.0 + jnp.exp(-x))


def _silu(x):
    return x * _sigmoid(x)


def _head_norm(x, g_row):
    return x * lax.rsqrt(jnp.mean(x * x, axis=-1, keepdims=True) + EPS) * g_row


def _rms_kernel(x_ref, g_ref, h_ref):
    x = x_ref[...]
    y = x * lax.rsqrt(jnp.mean(x * x, axis=-1, keepdims=True) + EPS)
    h_ref[...] = (y * g_ref[...]).astype(h_ref.dtype)


def rmsnorm_call(x, g, out_dtype, tb=256, row0=0, rows=None):
    d = x.shape[1]
    rows = x.shape[0] if rows is None else rows
    rb = row0 // tb
    return pl.pallas_call(
        _rms_kernel, grid=(rows // tb,),
        in_specs=[pl.BlockSpec((tb, d), lambda i: (rb + i, 0)), pl.BlockSpec((1, d), lambda i: (0, 0))],
        out_specs=pl.BlockSpec((tb, d), lambda i: (i, 0)),
        out_shape=jax.ShapeDtypeStruct((rows, d), out_dtype),
        compiler_params=_cparams(("parallel",)), name="rmsnorm",
    )(x, g.reshape(1, d).astype(F32))


def _matmul_kernel(*refs, n_pairs, has_res):
    a_refs, w_refs = refs[:n_pairs], refs[n_pairs:2 * n_pairs]
    o_ref = refs[-1]
    acc = _mm(a_refs[0][...], w_refs[0][...])
    for a_ref, w_ref in zip(a_refs[1:], w_refs[1:]):
        acc = acc + _mm(a_ref[...], w_ref[...])
    if has_res:
        acc = acc + refs[2 * n_pairs][...]
    o_ref[...] = acc


def matmul_call(a_list, w, layer, res=None, tm=512, tn=512, name="matmul"):
    t = a_list[0].shape[0]
    n = w.shape[2]
    in_specs = [pl.BlockSpec((tm, a.shape[1]), lambda i, j: (i, 0)) for a in a_list]
    row = 0
    for a in a_list:
        k = a.shape[1]
        in_specs.append(pl.BlockSpec((None, k, tn), functools.partial(lambda i, j, rb: (layer, rb, j), rb=row // k)))
        row += k
    args = list(a_list) + [w] * len(a_list)
    if res is not None:
        in_specs.append(pl.BlockSpec((tm, tn), lambda i, j: (i, j)))
        args.append(res)
    return pl.pallas_call(
        functools.partial(_matmul_kernel, n_pairs=len(a_list), has_res=res is not None),
        grid=(t // tm, n // tn), in_specs=in_specs,
        out_specs=pl.BlockSpec((tm, tn), lambda i, j: (i, j)),
        out_shape=jax.ShapeDtypeStruct((t, n), F32),
        compiler_params=_cparams(("parallel", "parallel")), name=name,
    )(*args)


A_BLOCKS = B_OFF // IN_TN
G_BLOCK = G_OFF // IN_TN
B_SHIFT = 2 * H_A
C_SHIFT = 2 * H_A + 2 * H_B
W_ROWS = 512


def _in_proj_kernel(a_ref, wm_ref, we_ref, o_ref, wb_ref):
    j, i = pl.program_id(0), pl.program_id(1)
    k = wm_ref.shape[0]

    def convert(shift):
        def body(r, carry):
            rs = pl.ds(pl.multiple_of(r * W_ROWS, W_ROWS), W_ROWS)
            if shift == 0:
                blk = wm_ref[rs, :]
            else:
                blk = jnp.concatenate([wm_ref[rs, shift:], we_ref[rs, :shift]], axis=1)
            wb_ref[rs, :] = blk.astype(BF16)
            return carry
        lax.fori_loop(0, k // W_ROWS, body, 0)

    first = i == 0

    @pl.when(first & (j < A_BLOCKS))
    def _():
        convert(0)

    @pl.when(first & (j >= A_BLOCKS) & (j < G_BLOCK))
    def _():
        convert(B_SHIFT)

    @pl.when(first & (j == G_BLOCK))
    def _():
        wb_ref[:, 0:LANE] = wm_ref[:, 0:LANE].astype(BF16)
        wb_ref[:, LANE:2 * LANE] = we_ref[...].astype(BF16)
        wb_ref[:, 2 * LANE:] = jnp.zeros((k, IN_TN - 2 * LANE), BF16)

    @pl.when(first & (j > G_BLOCK))
    def _():
        convert(C_SHIFT)

    o_ref[...] = jnp.dot(a_ref[...], wb_ref[...], preferred_element_type=F32)


def in_proj_call(a, w_in, layer, tm):
    t, k = a.shape
    lanes_per_tile = IN_TN // LANE
    main_blk = lambda j: jnp.where(j <= G_BLOCK, j, j - 1)
    extra_blk = lambda j: jnp.where(j < G_BLOCK, lanes_per_tile * (j + 1),
                                    jnp.where(j == G_BLOCK, B_OFF // LANE, lanes_per_tile * j))
    return pl.pallas_call(
        _in_proj_kernel, grid=(NP // IN_TN, t // tm),
        in_specs=[pl.BlockSpec((tm, k), lambda j, i: (i, 0)),
                  pl.BlockSpec((None, k, IN_TN), lambda j, i: (layer, 0, main_blk(j))),
                  pl.BlockSpec((None, k, LANE), lambda j, i: (layer, 0, extra_blk(j)))],
        out_specs=pl.BlockSpec((tm, IN_TN), lambda j, i: (i, j)),
        out_shape=jax.ShapeDtypeStruct((t, NP), F32),
        scratch_shapes=[pltpu.VMEM((k, IN_TN), BF16)],
        compiler_params=_cparams(("parallel", "arbitrary")), name="in_proj",
    )(a, w_in, w_in)


def _tri_masks(c):
    row = lax.broadcasted_iota(jnp.int32, (c, c), 0)
    col = lax.broadcasted_iota(jnp.int32, (c, c), 1)
    return row, col


def _mlstm_kernel(qk_ref, v_ref, o_ref, gc_ref, gt_ref, brow_ref, bcol_ref, ng_ref, c0_ref, n0_ref, m0_ref,
                  out_ref, c_ref, n_ref, m_ref, *, c, nseq, n_valid, has_init):
    ci = pl.program_id(1)

    @pl.when(ci == 0)
    def _():
        if has_init:
            c_ref[...] = c0_ref[...]
            n_ref[...] = n0_ref[...]
            m_ref[...] = m0_ref[...]
        else:
            c_ref[...] = jnp.zeros_like(c_ref)
            n_ref[...] = jnp.zeros_like(n_ref)
            m_ref[...] = jnp.zeros_like(m_ref)

    gc = gc_ref[...] + brow_ref[...]
    gt = gt_ref[...] + bcol_ref[...]
    row, col = _tri_masks(c)
    tri = row >= col
    padded = n_valid < c
    if padded:
        valid_c = lax.broadcasted_iota(jnp.int32, (c, 1), 0) < n_valid
        valid_r = lax.broadcasted_iota(jnp.int32, (1, c), 1) < n_valid
    sh = [(s, h) for s in range(nseq) for h in range(H_A)]
    rows = lambda s: slice(s * c, (s + 1) * c)
    heads_c = lambda x, off: jnp.stack([x[rows(s), off + h:off + h + 1] for s, h in sh], axis=0)
    heads_r = lambda x, off: jnp.stack([x[s, off + h:off + h + 1, :] for s, h in sh], axis=0)
    ig_c, lf_c = heads_c(gc, 0), heads_c(_log_sigmoid(gc), H_A)
    ig_r, lf_r = heads_r(gt, 0), heads_r(_log_sigmoid(gt), H_A)
    if padded:
        ig_c, lf_c = jnp.where(valid_c, ig_c, NEG_INF), jnp.where(valid_c, lf_c, 0.0)
        ig_r, lf_r = jnp.where(valid_r, ig_r, NEG_INF), jnp.where(valid_r, lf_r, 0.0)
    q = jnp.stack([qk_ref[rows(s), h * DK_A:(h + 1) * DK_A] for s, h in sh], axis=0) * (DK_A ** -0.5)
    k = jnp.stack([qk_ref[rows(s), (H_A + h) * DK_A:(H_A + h + 1) * DK_A] for s, h in sh], axis=0)
    v = jnp.stack([v_ref[rows(s), h * DV_A:(h + 1) * DV_A] for s, h in sh], axis=0)
    cst = c_ref[...].reshape(nseq * H_A, DK_A, DV_A)
    nst = jnp.stack([n_ref[s, h:h + 1, :] for s, h in sh], axis=0)
    m_prev = jnp.stack([m_ref[s, :, h:h + 1] for s, h in sh], axis=0)
    b_c = jnp.sum(jnp.where(tri, lf_r, 0.0), axis=2, keepdims=True)
    b_r = jnp.sum(jnp.where(row <= col, lf_c, 0.0), axis=1, keepdims=True)
    log_d = jnp.where(tri, b_c - b_r + ig_r, NEG_INF)
    m_inter = b_c + m_prev
    m_t = jnp.maximum(m_inter, jnp.max(log_d, axis=2, keepdims=True))
    s = _bmm_nt(q, k) * jnp.exp(log_d - m_t)
    inter = jnp.exp(m_inter - m_t)
    num = _bmm(s, v) + inter * _bmm(q, cst)
    den = jnp.sum(s, axis=2, keepdims=True) + inter * jnp.sum(q * nst, axis=2, keepdims=True)
    hh = num / jnp.maximum(jnp.abs(den), jnp.exp(-m_t))
    m_new = m_t[:, c - 1:c, :]
    w = jnp.exp(b_c[:, c - 1:c, :] - b_c + ig_c - m_new)
    decay = inter[:, c - 1:c, :]
    kw = k * w
    c_ref[...] = (decay * cst + _bmm_tn(kw, v)).reshape(nseq, H_A, DK_A, DV_A)
    n_new = decay * nst + jnp.sum(kw, axis=1, keepdims=True)
    for i, (s, h) in enumerate(sh):
        hs = slice(h * DV_A, (h + 1) * DV_A)
        n_ref[s, h:h + 1, :] = n_new[i]
        m_ref[s, :, h:h + 1] = m_new[i]
        out_ref[rows(s), hs] = _head_norm(hh[i], ng_ref[:, hs]) * _sigmoid(o_ref[rows(s), hs])


def _mixer_call(body, name, proj, seqs, layer, depth, tok_blocks, extra_inputs, init_states, st_shapes,
                mix_w, prev_mix, prev_states, scratch_shapes=()):
    row0, nb, nchunk, c, nseq = seqs["row0"], seqs["nb"], seqs["nchunk"], seqs["c"], seqs["nseq"]
    assert nseq == 1 or nchunk == 1
    rows = nseq * c
    rb = row0 // rows
    tok = lambda b, ci: rb + b * nchunk + ci
    st_spec = lambda shape, lyr: pl.BlockSpec(
        (None, nseq) + shape, functools.partial(lambda b, ci, n: (lyr, b) + (0,) * n, n=len(shape)))
    in_specs = [pl.BlockSpec((rows, w), functools.partial(lambda b, ci, cb: (tok(b, ci), cb), cb=cb))
                for w, cb in tok_blocks]
    args = [proj] * len(tok_blocks)
    for arr, spec in extra_inputs:
        in_specs.append(spec)
        args.append(arr)
    if init_states is None:
        for shape in st_shapes:
            in_specs.append(pl.BlockSpec((None, nseq) + shape, functools.partial(
                lambda b, ci, n: (0,) * (n + 2), n=len(shape))))
            args.append(jnp.zeros((1, nseq) + shape, F32))
    else:
        for shape, arr in zip(st_shapes, init_states):
            in_specs.append(st_spec(shape, layer))
            args.append(arr)
    n_in = len(args)
    out_specs = [pl.BlockSpec((rows, mix_w), lambda b, ci: (tok(b, ci), 0))]
    out_shape = [jax.ShapeDtypeStruct((proj.shape[0], mix_w), F32)]
    for shape in st_shapes:
        out_specs.append(st_spec(shape, layer))
        out_shape.append(jax.ShapeDtypeStruct((depth, nb) + shape, F32))
    prev = [prev_mix] + (list(prev_states) if prev_states is not None else [None] * len(st_shapes))
    aliases = {}
    for k, arr in enumerate(prev):
        if arr is not None:
            aliases[len(args)] = k
            in_specs.append(pl.BlockSpec(memory_space=pl.ANY))
            args.append(arr)
    n_alias = len(args) - n_in

    def kernel_fn(*refs):
        body(*refs[:n_in], *refs[n_in + n_alias:])

    return pl.pallas_call(
        kernel_fn, grid=(nb // nseq, nchunk), in_specs=in_specs, out_specs=out_specs, out_shape=out_shape,
        input_output_aliases=aliases, scratch_shapes=list(scratch_shapes),
        compiler_params=_cparams(("parallel", "arbitrary")), name=name)(*args)


def _const_spec(shape):
    return pl.BlockSpec(shape, functools.partial(lambda b, ci, n: (0,) * n, n=len(shape)))


def _gates_spec(seqs):
    nchunk, c, nseq = seqs["nchunk"], seqs["c"], seqs["nseq"]
    return pl.BlockSpec((nseq, GATE_ROWS, c), lambda b, ci: (b * nchunk + ci, 0, 0))


def mlstm_call(proj, gates_t, brow, bcol, norm_g, seqs, layer, depth, init_states, prev_mix, prev_states):
    body = functools.partial(_mlstm_kernel, c=seqs["c"], nseq=seqs["nseq"], n_valid=seqs["n_valid"],
                             has_init=init_states is not None)
    extra = [(gates_t, _gates_spec(seqs)), (brow, _const_spec((1, LANE))), (bcol, _const_spec((GATE_ROWS, 1))),
             (norm_g.reshape(1, A_W).astype(F32), _const_spec((1, A_W)))]
    return _mixer_call(body, "mlstm", proj, seqs, layer, depth,
                       [(A_W, 0), (A_W, 1), (A_W, 2), (LANE, G_OFF // LANE + 1)], extra, init_states,
                       [(H_A, DK_A, DV_A), (H_A, DK_A), (1, H_A)], A_W, prev_mix, prev_states)


def _bdot(a, b, dims):
    return lax.dot_general(a, b, (dims, ((0,), (0,))), preferred_element_type=F32)


def _bmm(a, b):
    return _bdot(a.astype(BF16), b.astype(BF16), ((2,), (1,)))


def _bmm_nt(a, b):
    return _bdot(a.astype(BF16), b.astype(BF16), ((2,), (2,)))


def _bmm_tn(a, b):
    return _bdot(jnp.swapaxes(a, 1, 2).astype(BF16), b.astype(BF16), ((2,), (1,)))


def _unit_lower_inverse(a, row, col, c):
    d = jnp.where(row == col, 1.0, 0.0)
    s = 1
    while s < c:
        blk = jnp.where(((row // (2 * s)) == (col // (2 * s))) & ((row % (2 * s)) >= s) & ((col % (2 * s)) < s),
                        a, 0.0)
        if s == 1:
            d = d - blk
        else:
            d = d - _bmm(_bmm(d, blk), d)
        s *= 2
    return d


def _gdn_kernel(qkv_ref, z_ref, gc_ref, gt_ref, brow_ref, bcol_ref, arow_ref, acol_ref, cw_ref, buf0_ref, ng_ref,
                s0_ref, out_ref, s_ref, xp_ref, *, c, nseq, n_valid, has_init):
    ci = pl.program_id(1)
    hk = H_B * DK_B
    rows = lambda s: slice(s * c, (s + 1) * c)

    @pl.when(ci == 0)
    def _():
        xp_ref[:, 0:8, :] = jnp.zeros((nseq, 8, xp_ref.shape[2]), F32)
        if has_init:
            s_ref[...] = s0_ref[...]
            xp_ref[:, 8 - (CONV_W - 1):8, :] = buf0_ref[...]
        else:
            s_ref[...] = jnp.zeros_like(s_ref)

    @pl.when(ci > 0)
    def _():
        xp_ref[:, 0:8, :] = xp_ref[:, c:c + 8, :]

    for s in range(nseq):
        xp_ref[s, 8:8 + c, :] = qkv_ref[rows(s), :]

    gc = gc_ref[...] + brow_ref[...]
    gt = gt_ref[...] + bcol_ref[...]
    row, col = _tri_masks(c)
    tri = row >= col
    strict = row > col
    padded = n_valid < c
    if padded:
        valid_c = lax.broadcasted_iota(jnp.int32, (c, 1), 0) < n_valid
        valid_r = lax.broadcasted_iota(jnp.int32, (1, c), 1) < n_valid
    ga, gb = 2 * H_A, 2 * H_A + H_B

    sh = [(s, h) for s in range(nseq) for h in range(H_B)]

    def conv_act(c0):
        slabs = []
        for s, h in sh:
            cs = slice(c0 + h * DK_B, c0 + (h + 1) * DK_B)
            y = cw_ref[0:1, cs] * xp_ref[s, 8 - 3:8 - 3 + c, cs]
            for j in range(1, CONV_W):
                y = y + cw_ref[j:j + 1, cs] * xp_ref[s, 8 - 3 + j:8 - 3 + j + c, cs]
            slabs.append(_silu(y))
        return jnp.stack(slabs, axis=0)

    heads_c = lambda x, off: jnp.stack([x[rows(s), off + h:off + h + 1] for s, h in sh], axis=0)
    heads_r = lambda x, off: jnp.stack([x[s, off + h:off + h + 1, :] for s, h in sh], axis=0)
    g_c = heads_c(-jnp.exp(arow_ref[...]) * _softplus(gc), ga)
    g_r = heads_r(-jnp.exp(acol_ref[...]) * _softplus(gt), ga)
    beta = heads_c(_sigmoid(gc), gb)
    if padded:
        g_c, g_r = jnp.where(valid_c, g_c, 0.0), jnp.where(valid_r, g_r, 0.0)
        beta = jnp.where(valid_c, beta, 0.0)
    q = conv_act(0)
    k = conv_act(hk)
    v = conv_act(2 * hk)
    q = q * lax.rsqrt(jnp.sum(q * q, axis=2, keepdims=True) + EPS) * (DK_B ** -0.5)
    k = k * lax.rsqrt(jnp.sum(k * k, axis=2, keepdims=True) + EPS)
    sst = s_ref[...].reshape(nseq * H_B, DK_B, DV_B)
    gcum_c = jnp.sum(jnp.where(tri, g_r, 0.0), axis=2, keepdims=True)
    gcum_r = jnp.sum(jnp.where(row <= col, g_c, 0.0), axis=1, keepdims=True)
    gam = jnp.exp(jnp.where(tri, gcum_c - gcum_r, NEG_INF))
    eg = jnp.exp(gcum_c)
    a = jnp.where(strict, beta * _bmm_nt(k, k) * gam, 0.0)
    tinv = _unit_lower_inverse(a, row, col, c)
    rhs = jnp.concatenate([beta * v, (beta * eg) * k], axis=2)
    sol = _bmm(tinv, rhs)
    u = sol[:, :, :DV_B] - _bmm(sol[:, :, DV_B:], sst)
    o = eg * _bmm(q, sst) + _bmm(_bmm_nt(q, k) * gam, u)
    g_last = gcum_c[:, c - 1:c, :]
    s_new = jnp.exp(g_last) * sst + _bmm_tn(k * jnp.exp(g_last - gcum_c), u)
    s_ref[...] = s_new.reshape(nseq, H_B, DK_B, DV_B)
    for i, (s, h) in enumerate(sh):
        hs = slice(h * DV_B, (h + 1) * DV_B)
        out_ref[rows(s), hs] = _head_norm(o[i], ng_ref[:, hs]) * _silu(z_ref[rows(s), hs])


def gdn_call(proj, gates_t, brow, bcol, arow, acol, conv_w, norm_g, seqs, layer, depth, init_states, buf0,
             prev_mix, prev_states):
    wb = 3 * H_B * DK_B
    has_init = init_states is not None
    nseq = seqs["nseq"]
    body = functools.partial(_gdn_kernel, c=seqs["c"], nseq=nseq, n_valid=seqs["n_valid"], has_init=has_init)
    if has_init:
        buf_spec = pl.BlockSpec((None, nseq, CONV_W - 1, wb), lambda b, ci: (layer, b, 0, 0))
    else:
        buf0 = jnp.zeros((1, nseq, CONV_W - 1, wb), F32)
        buf_spec = pl.BlockSpec((None, nseq, CONV_W - 1, wb), lambda b, ci: (0, 0, 0, 0))
    extra = [(gates_t, _gates_spec(seqs)), (brow, _const_spec((1, LANE))), (bcol, _const_spec((GATE_ROWS, 1))),
             (arow, _const_spec((1, LANE))), (acol, _const_spec((GATE_ROWS, 1))),
             (conv_w.astype(F32), _const_spec((CONV_W, wb))), (buf0, buf_spec),
             (norm_g.reshape(1, H_B * DV_B).astype(F32), _const_spec((1, H_B * DV_B)))]
    return _mixer_call(body, "gdn", proj, seqs, layer, depth,
                       [(wb, B_OFF // wb), (A_W, (B_OFF + wb) // A_W), (LANE, G_OFF // LANE)], extra, init_states,
                       [(H_B, DK_B, DV_B)], H_B * DV_B, prev_mix, prev_states,
                       scratch_shapes=[pltpu.VMEM((nseq, seqs["c"] + 8, wb), F32)])


def _hgrn_kernel(q_ref, f_ref, i_ref, g_ref, lb_ref, ng_ref, s0_ref, out_ref, s_ref, st_ref, *,
                 c, sub, nseq, n_valid, has_init, nchunk):
    ci = pl.program_id(1)

    sh = [(s, h) for s in range(nseq) for h in range(H_C)]

    @pl.when(ci == 0)
    def _():
        for i, (s, h) in enumerate(sh):
            st_ref[i] = s0_ref[s, h].T if has_init else jnp.zeros((DV_C, DK_C), F32)

    padded = n_valid < sub
    rowi = lax.broadcasted_iota(jnp.int32, (1, sub, 1), 1)
    heads = lambda ref, j: jnp.stack(
        [ref[s * c + j * sub:s * c + (j + 1) * sub, h * DK_C:(h + 1) * DK_C] for s, h in sh], axis=0)
    lb = jnp.stack([lb_ref[:, h * DK_C:(h + 1) * DK_C] for s, h in sh], axis=0)
    log_1mlb, log_lb = jnp.log1p(-lb), jnp.log(lb)
    sst = st_ref[...]
    for j in range(c // sub):
        cf = heads(f_ref, j)
        la = log_1mlb + _log_sigmoid(cf)
        logf = jnp.maximum(la, log_lb) + jnp.log1p(jnp.exp(-jnp.abs(la - log_lb)))
        k = (1.0 - lb) * _sigmoid(-cf)
        if padded:
            logf, k = jnp.where(rowi < n_valid, logf, 0.0), jnp.where(rowi < n_valid, k, 0.0)
        q = _silu(heads(q_ref, j))
        iv = heads(i_ref, j)
        bc = jnp.zeros((nseq * H_C, sub, DK_C), F32)
        for t in range(sub):
            bc = bc + jnp.where(rowi >= t, logf[:, t:t + 1, :], 0.0)
        o = _bmm_nt(q * jnp.exp(bc), sst)
        for t in range(sub):
            dec = jnp.exp(jnp.where(rowi >= t, bc - bc[:, t:t + 1, :], NEG_INF))
            att = jnp.sum(q * dec * k[:, t:t + 1, :], axis=2, keepdims=True)
            o = o + att * iv[:, t:t + 1, :]
        b_last = bc[:, sub - 1:sub, :]
        sst = jnp.exp(b_last) * sst + _bmm_tn(iv, k * jnp.exp(b_last - bc))
        for i, (s, h) in enumerate(sh):
            rows = slice(s * c + j * sub, s * c + (j + 1) * sub)
            hs = slice(h * DK_C, (h + 1) * DK_C)
            out_ref[rows, hs] = _head_norm(o[i], ng_ref[:, hs]) * _silu(g_ref[rows, hs])
    st_ref[...] = sst

    @pl.when(ci == nchunk - 1)
    def _():
        for i, (s, h) in enumerate(sh):
            s_ref[s, h] = st_ref[i].T


def hgrn_call(proj, lb, norm_g, seqs, sub, layer, depth, init_states, prev_mix, prev_states):
    w = H_C * DK_C
    cb = C_OFF // w
    body = functools.partial(_hgrn_kernel, c=seqs["c"], sub=sub, nseq=seqs["nseq"],
                             n_valid=min(seqs["n_valid"], sub),
                             has_init=init_states is not None, nchunk=seqs["nchunk"])
    extra = [(lb.reshape(1, w).astype(F32), _const_spec((1, w))),
             (norm_g.reshape(1, w).astype(F32), _const_spec((1, w)))]
    return _mixer_call(body, "hgrn", proj, seqs, layer, depth,
                       [(w, cb), (w, cb + 1), (w, cb + 2), (w, cb + 3)], extra, init_states,
                       [(H_C, DK_C, DV_C)], w, prev_mix, prev_states,
                       scratch_shapes=[pltpu.VMEM((seqs["nseq"] * H_C, DV_C, DK_C), F32)])


def _top_values(x, k):
    rows = []
    for _ in range(k):
        mx = jnp.max(x, axis=0, keepdims=True)
        rows.append(mx)
        x = jnp.where(x == mx, NEG_INF, x)
    return jnp.concatenate(rows, axis=0)


def _route_kernel(q_ref, sk_ref, s1_ref, s2_ref, e1_ref, e2_ref, tau_ref):
    for h in range(PEER_HEADS):
        s1 = _mmf_nt(sk_ref[h, 0], q_ref[:, (2 * h) * LANE:(2 * h + 1) * LANE])
        s2 = _mmf_nt(sk_ref[h, 1], q_ref[:, (2 * h + 1) * LANE:(2 * h + 2) * LANE])
        a = _top_values(s1, PEER_TOPK)
        b = _top_values(s2, PEER_TOPK)
        cand = jnp.concatenate([a[0:1, :] + b] + [a[i:i + 1, :] + b[:PEER_TOPK // 2, :]
                                                  for i in range(1, PEER_TOPK)], axis=0)
        best = _top_values(cand, PEER_TOPK)
        z = jnp.sum(jnp.exp(best - best[0:1, :]), axis=0, keepdims=True)
        tau = best[PEER_TOPK - 1:PEER_TOPK, :]
        thr = None
        for jj in range(PEER_TOPK):
            bj = b[jj:jj + 1, :]
            val = jnp.where(s1 + bj >= tau, bj, jnp.inf)
            thr = val if thr is None else jnp.minimum(thr, val)
        s1_ref[h] = thr
        s2_ref[h] = s2
        e1_ref[h] = jnp.exp(s1 - a[0:1, :])
        e2_ref[h] = jnp.exp(s2 - b[0:1, :]) / z
        tau_ref[h:h + 1, :] = best[PEER_TOPK - 1:PEER_TOPK, :]


def route_call(qry, subkeys, tb=256):
    t = qry.shape[0]
    big = jax.ShapeDtypeStruct((PEER_HEADS, N_KEYS, t), F32)
    big_spec = pl.BlockSpec((PEER_HEADS, N_KEYS, tb), lambda i: (0, 0, i))
    return pl.pallas_call(
        _route_kernel, grid=(t // tb,),
        in_specs=[pl.BlockSpec((tb, qry.shape[1]), lambda i: (i, 0)),
                  pl.BlockSpec(subkeys.shape, lambda i: (0, 0, 0, 0))],
        out_specs=[big_spec] * 4 + [pl.BlockSpec((PEER_HEADS, tb), lambda i: (0, i))],
        out_shape=[big] * 4 + [jax.ShapeDtypeStruct((PEER_HEADS, t), F32)],
        compiler_params=_cparams(("parallel",)), name="peer_route",
    )(qry, subkeys.astype(F32))


PEER_SUB = 512
PEER_LANES = 256


def _peer_kernel(t_ref, x_ref, u_ref, v_ref, s1_ref, s2_ref, e1_ref, e2_ref, tau_ref, y_ref, *, eb):
    j = pl.program_id(1)
    tm = t_ref.shape[0]

    @pl.when(j == 0)
    def _():
        y_ref[...] = x_ref[...]

    for p in range(eb // PEER_SUB):
        es = slice(p * PEER_SUB, (p + 1) * PEER_SUB)
        a_t = lax.dot_general(u_ref[es, :], t_ref[...], (((1,), (1,)), ((), ())), preferred_element_type=F32)
        rows = []
        for r in range(PEER_SUB // N_KEYS):
            i1 = j * (eb // N_KEYS) + p * (PEER_SUB // N_KEYS) + r
            s1_rows = [s1_ref[h, pl.ds(i1, 1), :] for h in range(PEER_HEADS)]
            e1_rows = [e1_ref[h, pl.ds(i1, 1), :] for h in range(PEER_HEADS)]
            tiles = []
            for tc in range(tm // PEER_LANES):
                ls = slice(tc * PEER_LANES, (tc + 1) * PEER_LANES)
                gsum = None
                for h in range(PEER_HEADS):
                    gh = jnp.where(s2_ref[h, :, ls] >= s1_rows[h][:, ls],
                                   e1_rows[h][:, ls] * e2_ref[h, :, ls], 0.0)
                    gsum = gh if gsum is None else gsum + gh
                tiles.append(jax.nn.gelu(a_t[r * N_KEYS:(r + 1) * N_KEYS, ls]) * gsum)
            rows.append(jnp.concatenate(tiles, axis=1))
        coef = jnp.concatenate(rows, axis=0)
        y_ref[...] += jnp.dot(coef.T.astype(BF16), v_ref[es, :], preferred_element_type=F32)


def peer_call(t_bf, x, u_bf, v_bf, layer, s1, s2, e1, e2, tau, tm=512, eb=512):
    t, d = t_bf.shape
    ne = u_bf.shape[1]
    once = pl.Buffered(1)
    big_spec = pl.BlockSpec((PEER_HEADS, N_KEYS, tm), lambda i, j: (0, 0, i), pipeline_mode=once)
    return pl.pallas_call(
        functools.partial(_peer_kernel, eb=eb),
        grid=(t // tm, ne // eb),
        in_specs=[pl.BlockSpec((tm, d), lambda i, j: (i, 0), pipeline_mode=once),
                  pl.BlockSpec((tm, d), lambda i, j: (i, 0), pipeline_mode=once),
                  pl.BlockSpec((None, eb, d), lambda i, j: (layer, j, 0)),
                  pl.BlockSpec((None, eb, d), lambda i, j: (layer, j, 0)),
                  big_spec, big_spec, big_spec, big_spec,
                  pl.BlockSpec((PEER_HEADS, tm), lambda i, j: (0, i), pipeline_mode=once)],
        out_specs=pl.BlockSpec((tm, d), lambda i, j: (i, 0)),
        out_shape=jax.ShapeDtypeStruct((t, d), F32),
        compiler_params=_cparams(("parallel", "arbitrary")), name="peer_dense",
    )(t_bf, x, u_bf, v_bf, s1, s2, e1, e2, tau)


def _gate_params(b_i, b_f, a_log, dt_bias):
    zeros = jnp.zeros((LANE - 2 * H_A - 2 * H_B,), F32)
    bias = jnp.concatenate([b_i.astype(F32), b_f.astype(F32), dt_bias.astype(F32), jnp.zeros((H_B,), F32), zeros])
    alog = jnp.concatenate([jnp.zeros((2 * H_A,), F32), a_log.astype(F32), jnp.zeros((H_B,), F32), zeros])
    return (bias.reshape(1, LANE), bias[:GATE_ROWS].reshape(GATE_ROWS, 1),
            alog.reshape(1, LANE), alog[:GATE_ROWS].reshape(GATE_ROWS, 1))


def _gates_rows(proj, c):
    na, nb = 2 * H_A, 2 * H_B
    g = jnp.concatenate([proj[:, G_OFF + LANE:G_OFF + LANE + na], proj[:, G_OFF + na:G_OFF + na + nb],
                         jnp.zeros((proj.shape[0], GATE_ROWS - na - nb), F32)], axis=1)
    return g.reshape(g.shape[0] // c, c, GATE_ROWS).transpose(0, 2, 1)


def _hgrn_lower_bounds(logits):
    p = jax.nn.softmax(logits.astype(F32), axis=0)
    cs = jnp.cumsum(p, axis=0)
    return cs - cs[0:1]


def kernel(x_prompt, x_sample, state_mlstm_C, state_mlstm_n, state_mlstm_m, state_gdn_S, state_gdn_conv,
           state_hgrn_S, norm_mix_g, w_in, mlstm_b_i, mlstm_b_f, mlstm_norm_g, gdn_conv_w, gdn_A_log,
           gdn_dt_bias, gdn_norm_g, hgrn_lb_logits, hgrn_norm_g, w_out, norm_ffn_g, peer_w_q, peer_subkeys,
           peer_u, peer_v, final_norm_g, *, chunk=128, hgrn_chunk=16, tm=512, tm_peer=512, eb_peer=512, tb=256):
    bp, lp, d = x_prompt.shape
    bs, ls, _ = x_sample.shape
    depth = w_in.shape[0]
    tp = bp * lp
    ts = bs * ls
    x = jnp.concatenate([x_prompt.reshape(tp, d), x_sample.reshape(ts, d)], axis=0).astype(F32)
    tm_in = next(c for c in (1024, 1088, tm) if (tp + ts) % c == 0)
    lbs = _hgrn_lower_bounds(hgrn_lb_logits)
    wb = 3 * H_B * DK_B
    u_all, v_all = peer_u.astype(BF16), peer_v.astype(BF16)
    wo_all, wq_all = w_out.astype(BF16), peer_w_q.astype(BF16)
    pseq = dict(row0=0, nb=bp, nchunk=lp // chunk, c=chunk, n_valid=chunk, nseq=1)
    sseq = dict(row0=0, nb=bs, nchunk=1, c=SAMPLE_PAD, n_valid=ls, nseq=SAMPLE_SEQS)
    a_init = [state_mlstm_C, state_mlstm_n, state_mlstm_m.reshape(depth, bs, 1, H_A)]
    st_ap = st_as = st_bp = st_bs = st_cp = st_cs = None
    bufs_p, bufs_s = [], []
    for l in range(depth):
        h = rmsnorm_call(x, norm_mix_g[l], BF16, tb=tb)
        proj = in_proj_call(h, w_in, l, tm=tm_in)
        proj_s = jnp.pad(proj[tp:].reshape(bs, ls, NP), ((0, 0), (0, SAMPLE_PAD - ls), (0, 0)))
        proj_s = proj_s.reshape(bs * SAMPLE_PAD, NP)
        brow, bcol, arow, acol = _gate_params(mlstm_b_i[l], mlstm_b_f[l], gdn_A_log[l], gdn_dt_bias[l])
        gt_p = _gates_rows(proj[:tp], chunk)
        gt_s = _gates_rows(proj_s, SAMPLE_PAD)
        mix_a, *st_ap = mlstm_call(proj, gt_p, brow, bcol, mlstm_norm_g[l], pseq, l, depth, None, None, st_ap)
        smp_a, *st_as = mlstm_call(proj_s, gt_s, brow, bcol, mlstm_norm_g[l], sseq, l, depth, a_init, None, st_as)
        gdn_w = (brow, bcol, arow, acol, gdn_conv_w[l], gdn_norm_g[l])
        mix_b, *st_bp = gdn_call(proj, gt_p, *gdn_w, pseq, l, depth, None, None, None, st_bp)
        smp_b, *st_bs = gdn_call(proj_s, gt_s, *gdn_w, sseq, l, depth, [state_gdn_S], state_gdn_conv, None, st_bs)
        mix_c, *st_cp = hgrn_call(proj, lbs[l], hgrn_norm_g[l], pseq, hgrn_chunk, l, depth, None, None, st_cp)
        smp_c, *st_cs = hgrn_call(proj_s, lbs[l], hgrn_norm_g[l], sseq, SAMPLE_PAD, l, depth, [state_hgrn_S],
                                  None, st_cs)
        valid = lambda m: m.reshape(bs, SAMPLE_PAD, m.shape[1])[:, :ls].reshape(ts, m.shape[1])
        mix_a, mix_b, mix_c = (m.at[tp:].set(valid(s)) for m, s in ((mix_a, smp_a), (mix_b, smp_b), (mix_c, smp_c)))
        bufs_p.append(jnp.stack([proj[(b + 1) * lp - (CONV_W - 1):(b + 1) * lp, B_OFF:B_OFF + wb]
                                 for b in range(bp)], axis=0))
        bufs_s.append(proj[tp:, B_OFF:B_OFF + wb].reshape(bs, ls, wb)[:, ls - (CONV_W - 1):])
        x = matmul_call([mix_a, mix_b, mix_c], wo_all, l, res=x, tm=tm, name="out_proj")
        t_bf = rmsnorm_call(x, norm_ffn_g[l], BF16, tb=tb)
        qry = matmul_call([t_bf], wq_all, l, tm=tm, name="peer_query")
        s1, s2, e1, e2, tau = route_call(qry, peer_subkeys[l], tb=tb)
        x = peer_call(t_bf, x, u_all, v_all, l, s1, s2, e1, e2, tau, tm=tm_peer, eb=eb_peer)
    y_prompt = rmsnorm_call(x, final_norm_g, F32, tb=tb, row0=0, rows=tp).reshape(bp, lp, d)
    y_sample = rmsnorm_call(x, final_norm_g, F32, tb=tb, row0=tp, rows=ts).reshape(bs, ls, d)
    m_p, m_s = st_ap[2].reshape(depth, bp, H_A), st_as[2].reshape(depth, bs, H_A)
    return (y_prompt, y_sample, st_ap[0], st_as[0], st_ap[1], st_as[1], m_p, m_s, st_bp[0], st_bs[0],
            jnp.stack(bufs_p, axis=0), jnp.stack(bufs_s, axis=0), st_cp[0], st_cs[0])
```
